```python
import jax, jax.numpy as jnp
from jax import lax
import numpy as np

D_MODEL = 1024
BATCH = 16
SEQ = 4096
DEPTH = 2

N_A_LAYERS = DEPTH // 2
N_B_LAYERS = DEPTH - N_A_LAYERS

ROPE_THETA = 10000.0
EPS = 1e-6
Q_BLOCK = 128
ADA_INIT = 0.5

A_HEADS = 16
A_LATENT = 128
A_ROPE = 32
A_VDIM = 128
A_WIDTH = A_HEADS * A_VDIM
IDX_HEADS = 8
IDX_DIM = 64
TOPK_MAX = 256
A_SIZES = (A_HEADS * A_LATENT, A_HEADS * A_ROPE, A_LATENT, A_ROPE, A_WIDTH,
           IDX_HEADS * IDX_DIM, IDX_DIM, IDX_HEADS)
A_IN = sum(A_SIZES)

B_HEADS = 8
B_HDIM = 128
B_GROUPS = ((128, 1), (512, 4), (2048, 16))
N_GROUPS = len(B_GROUPS)
B_WIDTH = B_HEADS * B_HDIM
B_IN = N_GROUPS * B_WIDTH + B_WIDTH
KV_OUT = 2 * B_WIDTH

kernel_name = "yoco_dsa_longnet_hybrid"


def rmsnorm(x, g):
    xf = x.astype(jnp.float32)
    y = xf * lax.rsqrt(jnp.mean(xf * xf, axis=-1, keepdims=True) + EPS)
    return (y * g.astype(jnp.float32)).astype(x.dtype)


def rope_tables(positions, dim):
    inv = ROPE_THETA ** (-jnp.arange(0, dim, 2, dtype=jnp.float32) / dim)
    ang = positions.astype(jnp.float32)[..., None] * inv
    return jnp.cos(ang)[:, :, None, :], jnp.sin(ang)[:, :, None, :]


def apply_rope(x, cs):
    cos, sin = cs
    xf = x.astype(jnp.float32)
    x1, x2 = jnp.split(xf, 2, axis=-1)
    return jnp.concatenate([x1 * cos - x2 * sin, x2 * cos + x1 * sin], axis=-1).astype(x.dtype)


def split_cols(t, sizes):
    out, o = [], 0
    for s in sizes:
        out.append(t[..., o:o + s])
        o += s
    return out


def ada_prenorm(h, c, g, ada_w, ada_b):
    mod = (jax.nn.silu(c) @ ada_w + ada_b)[:, None, :]
    shift, scale, gate = jnp.split(mod, 3, axis=-1)
    return rmsnorm(h, g) * (1 + scale) + shift, gate


def dsa_mixer(h, positions, w_in, kv_norm_g, w_uv, w_out):
    B_, S, _ = h.shape
    q_lat, q_rope, c_kv, k_rope, gate, q_idx, k_idx, w_idx = split_cols(h @ w_in, A_SIZES)
    rope_a = rope_tables(positions, A_ROPE)
    q_lat = q_lat.reshape(B_, S, A_HEADS, A_LATENT)
    q_rope = apply_rope(q_rope.reshape(B_, S, A_HEADS, A_ROPE), rope_a)
    c_kv = rmsnorm(c_kv, kv_norm_g)
    k_rope = apply_rope(k_rope[:, :, None, :], rope_a)[:, :, 0]
    keys = jnp.concatenate([c_kv, k_rope], axis=-1)
    qcat = jnp.concatenate([q_lat, q_rope], axis=-1)
    rope_i = rope_tables(positions, IDX_DIM)
    q_idx = apply_rope(q_idx.reshape(B_, S, IDX_HEADS, IDX_DIM), rope_i)
    k_idx = apply_rope(k_idx[:, :, None, :], rope_i)[:, :, 0]
    w_idx = w_idx * IDX_HEADS ** -0.5
    topk = min(TOPK_MAX, S // 4)
    nblk = S // Q_BLOCK
    key_pos = jnp.arange(S)
    scale = (A_LATENT + A_ROPE) ** -0.5

    def to_blocks(t):
        return t.reshape(B_, nblk, Q_BLOCK, *t.shape[2:]).swapaxes(0, 1)

    def block(args):
        qb, qib, wb, start = args
        qpos = start + jnp.arange(Q_BLOCK)
        rel = jax.nn.relu(jnp.einsum('bqhd,bsd->bqhs', qib, k_idx))
        iscore = jnp.einsum('bqh,bqhs->bqs', wb, rel).astype(jnp.float32)
        causal = key_pos[None, :] <= qpos[:, None]
        iscore = jnp.where(causal[None], iscore, -jnp.inf)
        _, sel = lax.top_k(iscore, topk)
        gk = jax.vmap(lambda kb, ib: kb[ib])(keys, sel)
        s = jnp.einsum('bqhd,bqkd->bqhk', qb, gk).astype(jnp.float32) * scale
        valid = (sel <= qpos[None, :, None])[:, :, None, :]
        p = jax.nn.softmax(jnp.where(valid, s, -jnp.inf), axis=-1)
        return jnp.einsum('bqhk,bqkc->bqhc', p.astype(gk.dtype), gk[..., :A_LATENT])

    starts = jnp.arange(nblk, dtype=jnp.int32) * Q_BLOCK
    o_lat = lax.map(block, (to_blocks(qcat), to_blocks(q_idx), to_blocks(w_idx), starts))
    o_lat = o_lat.swapaxes(0, 1).reshape(B_, S, A_HEADS, A_LATENT)
    o = jnp.einsum('bshc,hcv->bshv', o_lat, w_uv).reshape(B_, S, A_WIDTH)
    return (o * jax.nn.silu(gate)) @ w_out


def shared_kv(h, positions, g, w_kv):
    B_, S, _ = h.shape
    k, v = jnp.split(rmsnorm(h, g) @ w_kv, 2, axis=-1)
    k = apply_rope(k.reshape(B_, S, B_HEADS, B_HDIM), rope_tables(positions, B_HDIM))
    return k, v.reshape(B_, S, B_HEADS, B_HDIM)


def dilated_band_attention(q, k, v, dilation, window):
    B_, S, H, Dh = q.shape
    W = window // dilation
    n = S // dilation
    nb = -(-n // W)
    n_pad = nb * W

    def sub(t):
        t = t.reshape(B_, n, dilation, H, Dh).swapaxes(1, 2).reshape(B_ * dilation, n, H, Dh)
        return jnp.pad(t, ((0, 0), (0, n_pad - n), (0, 0), (0, 0)))

    def with_prev(t):
        tb = t.reshape(-1, nb, W, H, Dh)
        prev = jnp.pad(tb, ((0, 0), (1, 0), (0, 0), (0, 0), (0, 0)))[:, :-1]
        return jnp.concatenate([prev, tb], axis=2)

    qb = sub(q).reshape(-1, nb, W, H, Dh)
    kb, vb = with_prev(sub(k)), with_prev(sub(v))
    s = jnp.einsum('znqhd,znkhd->znhqk', qb, kb).astype(jnp.float32) * Dh ** -0.5
    qpos = jnp.arange(nb)[:, None] * W + jnp.arange(W)[None, :]
    kpos = jnp.arange(nb)[:, None] * W - W + jnp.arange(2 * W)[None, :]
    dist = qpos[:, :, None] - kpos[:, None, :]
    mask = (dist >= 0) & (dist <= W) & (kpos[:, None, :] >= 0)
    s = jnp.where(mask[None, :, None], s, -jnp.inf)
    lse = jax.nn.logsumexp(s, axis=-1)
    p = jnp.exp(s - lse[..., None])
    o = jnp.einsum('znhqk,znkhd->znqhd', p.astype(v.dtype), vb)
    o = o.reshape(B_, dilation, n_pad, H, Dh)[:, :, :n].swapaxes(1, 2).reshape(B_, S, H, Dh)
    lse = lse.transpose(0, 1, 3, 2).reshape(B_, dilation, n_pad, H)[:, :, :n]
    lse = lse.swapaxes(1, 2).reshape(B_, S, H)
    return o, lse


def dilated_mixer(h, positions, k_sh, v_sh, w_in, w_out):
    B_, S, _ = h.shape
    proj = h @ w_in
    parts = split_cols(proj, (B_WIDTH,) * N_GROUPS + (B_WIDTH,))
    gate = parts[-1]
    rope_b = rope_tables(positions, B_HDIM)
    outs, lses = [], []
    for qg, (window, dil) in zip(parts[:-1], B_GROUPS):
        q = apply_rope(qg.reshape(B_, S, B_HEADS, B_HDIM), rope_b)
        o, lse = dilated_band_attention(q, k_sh, v_sh, dil, window)
        outs.append(o.astype(jnp.float32))
        lses.append(lse)
    alpha = jax.nn.softmax(jnp.stack(lses, axis=-1), axis=-1)
    o = jnp.einsum('bshg,gbshd->bshd', alpha, jnp.stack(outs)).astype(h.dtype)
    return (o.reshape(B_, S, B_WIDTH) * jax.nn.silu(gate)) @ w_out


def setup_inputs(seed: int = 0) -> dict:
    key = jax.random.key(seed)
    ks = jax.random.split(key, 18)
    D = D_MODEL

    def nrm(k, shape, fan_in):
        return jax.random.normal(k, shape, jnp.float32) * fan_in ** -0.5

    def gain(k, shape):
        return 1.0 + 0.05 * jax.random.normal(k, shape, jnp.float32)

    positions = (jax.random.randint(ks[2], (BATCH, 1), 0, 1024, dtype=jnp.int32)
                 + jnp.arange(SEQ, dtype=jnp.int32)[None, :])
    return {
        "x": jax.random.normal(ks[0], (BATCH, SEQ, D), jnp.float32),
        "c": jax.random.normal(ks[1], (BATCH, D), jnp.float32),
        "positions": positions,
        "a_norm": gain(ks[3], (N_A_LAYERS, D)),
        "a_ada_w": nrm(ks[4], (N_A_LAYERS, D, 3 * D), D) * ADA_INIT,
        "a_ada_b": 0.02 * jax.random.normal(ks[5], (N_A_LAYERS, 3 * D), jnp.float32),
        "a_w_in": nrm(ks[6], (N_A_LAYERS, D, A_IN), D),
        "a_kv_norm": gain(ks[7], (N_A_LAYERS, A_LATENT)),
        "a_w_uv": nrm(ks[8], (N_A_LAYERS, A_HEADS, A_LATENT, A_VDIM), A_LATENT),
        "a_w_out": nrm(ks[9], (N_A_LAYERS, A_WIDTH, D), A_WIDTH),
        "kv_norm": gain(ks[10], (D,)),
        "w_kv": nrm(ks[11], (D, KV_OUT), D),
        "b_norm": gain(ks[12], (N_B_LAYERS, D)),
        "b_ada_w": nrm(ks[13], (N_B_LAYERS, D, 3 * D), D) * ADA_INIT,
        "b_ada_b": 0.02 * jax.random.normal(ks[14], (N_B_LAYERS, 3 * D), jnp.float32),
        "b_w_in": nrm(ks[15], (N_B_LAYERS, D, B_IN), D),
        "b_w_out": nrm(ks[16], (N_B_LAYERS, B_WIDTH, D), B_WIDTH),
        "final_norm": gain(ks[17], (D,)),
    }


def reference(x, c, positions, a_norm, a_ada_w, a_ada_b, a_w_in, a_kv_norm, a_w_uv, a_w_out,
              kv_norm, w_kv, b_norm, b_ada_w, b_ada_b, b_w_in, b_w_out, final_norm):
    h = x
    k_sh = v_sh = None
    for i in range(DEPTH):
        if i < N_A_LAYERS:
            hn, g = ada_prenorm(h, c, a_norm[i], a_ada_w[i], a_ada_b[i])
            h = h + g * dsa_mixer(hn, positions, a_w_in[i], a_kv_norm[i], a_w_uv[i], a_w_out[i])
        else:
            if i == N_A_LAYERS:
                k_sh, v_sh = shared_kv(h, positions, kv_norm, w_kv)
            j = i - N_A_LAYERS
            hn, g = ada_prenorm(h, c, b_norm[j], b_ada_w[j], b_ada_b[j])
            h = h + g * dilated_mixer(hn, positions, k_sh, v_sh, b_w_in[j], b_w_out[j])
    return rmsnorm(h, final_norm)
```

```python
import functools
import math

import jax
import jax.numpy as jnp
from jax import lax
from jax.experimental import pallas as pl
from jax.experimental.pallas import tpu as pltpu

ROPE_THETA = 10000.0
EPS = 1e-6
LOG2E = 1.4426950408889634

A_HEADS = 16
A_LATENT = 128
A_ROPE = 32
A_VDIM = 128
A_WIDTH = A_HEADS * A_VDIM
IDX_HEADS = 8
IDX_DIM = 64
TOPK_MAX = 256
Q_BLOCK = 128
KEY_CHUNK = 256
A_QPAD = 256

B_HEADS = 8
B_HDIM = 128
B_GROUPS = ((128, 1), (512, 4), (2048, 16))
B_WIDTH = B_HEADS * B_HDIM

LANES = 128
NEG = -1e30
INT_MIN = -(2 ** 31)
VMEM_LIMIT = 56 * 1024 * 1024

AUX_W_OFF = A_ROPE


def _bf16(t):
    return t.astype(jnp.bfloat16)


def _dot(a, b):
    return jnp.dot(a, b, preferred_element_type=jnp.float32)


def _dot_nt(a, b):
    return lax.dot_general(a, b, (((1,), (1,)), ((), ())), preferred_element_type=jnp.float32)


def _silu(t):
    return t * (1.0 / (1.0 + jnp.exp(-t)))


def _lane_iota(shape):
    return lax.broadcasted_iota(jnp.int32, shape, len(shape) - 1)


def _ada_kernel(c_ref, w_ref, b_ref, out_ref):
    out_ref[...] = _dot(_bf16(_silu(c_ref[...])), _bf16(w_ref[...])) + b_ref[...]


def _ada(c, w, b):
    bsz, d = c.shape
    out = pl.pallas_call(
        _ada_kernel,
        grid=(3,),
        in_specs=[pl.BlockSpec((bsz, d), lambda j: (0, 0)),
                  pl.BlockSpec((d, d), lambda j: (0, j)),
                  pl.BlockSpec((1, d), lambda j: (0, j))],
        out_specs=pl.BlockSpec((bsz, d), lambda j: (0, j)),
        out_shape=jax.ShapeDtypeStruct((bsz, 3 * d), jnp.float32),
        name="ada_mod",
    )(c, w, b.reshape(1, 3 * d))
    return out.reshape(bsz, 3, d)


def _rope_table_kernel(pos_ref, inv_ref, sign_ref, cos_ref, sin_ref):
    ang = pos_ref[...] * inv_ref[...]
    cos_ref[...] = jnp.cos(ang)
    sin_ref[...] = jnp.sin(ang) * sign_ref[...]


def _rope_lane_consts(dim):
    half = dim // 2
    inv = ROPE_THETA ** (-jnp.arange(0, dim, 2, dtype=jnp.float32) / dim)
    lane = jnp.arange(LANES) % dim
    return inv[lane % half], jnp.where(lane < half, -1.0, 1.0).astype(jnp.float32)


def _rope_tables(positions, dims, tm=512):
    n = positions.size
    posf = jnp.broadcast_to(positions.reshape(n, 1).astype(jnp.float32), (n, LANES))
    consts = [_rope_lane_consts(d) for d in dims]
    inv = jnp.stack([cst[0] for cst in consts]).reshape(len(dims), 1, LANES)
    sign = jnp.stack([cst[1] for cst in consts]).reshape(len(dims), 1, LANES)
    shape = jax.ShapeDtypeStruct((len(dims), n, LANES), jnp.float32)
    return pl.pallas_call(
        _rope_table_kernel,
        grid=(len(dims), n // tm),
        in_specs=[pl.BlockSpec((tm, LANES), lambda t, i: (i, 0)),
                  pl.BlockSpec((None, 1, LANES), lambda t, i: (t, 0, 0)),
                  pl.BlockSpec((None, 1, LANES), lambda t, i: (t, 0, 0))],
        out_specs=[pl.BlockSpec((None, tm, LANES), lambda t, i: (t, i, 0))] * 2,
        out_shape=[shape, shape],
        name="rope_tables",
    )(posf, inv, sign)


def _rope_lanes(xv, cos, sin_signed, dim):
    half = dim // 2
    if dim == LANES:
        partner = pltpu.roll(xv, half, axis=1)
    else:
        first = (_lane_iota(xv.shape) % dim) < half
        partner = jnp.where(first, pltpu.roll(xv, LANES - half, axis=1), pltpu.roll(xv, half, axis=1))
    return xv * cos + partner * sin_signed


def _prenorm(h, g, mod_ref):
    y = h * lax.rsqrt(jnp.mean(h * h, axis=-1, keepdims=True) + EPS) * g
    return y * (1.0 + mod_ref[1:2, :]) + mod_ref[0:1, :]


_A_COLS = {}
_o = 0
for _name, _w in (("q_lat", A_HEADS * A_LATENT), ("q_rope", A_HEADS * A_ROPE), ("c_kv", A_LATENT),
                  ("aux", LANES), ("kdup", LANES), ("gate", A_WIDTH), ("q_idx", IDX_HEADS * IDX_DIM)):
    _A_COLS[_name] = (_o, _o + _w)
    _o += _w
A_COLS_TOTAL = _o


def _a_weight(w_in):
    sizes = (A_HEADS * A_LATENT, A_HEADS * A_ROPE, A_LATENT, A_ROPE, A_WIDTH, IDX_HEADS * IDX_DIM, IDX_DIM, IDX_HEADS)
    parts, o = [], 0
    for s in sizes:
        parts.append(w_in[:, o:o + s])
        o += s
    q_lat, q_rope, c_kv, k_rope, gate, q_idx, k_idx, w_idx = parts
    d = w_in.shape[0]
    aux = jnp.concatenate([k_rope, w_idx, jnp.zeros((d, LANES - A_ROPE - IDX_HEADS), w_in.dtype)], axis=1)
    kdup = jnp.concatenate([k_idx, k_idx], axis=1)
    return _bf16(jnp.concatenate([q_lat, q_rope, c_kv, aux, kdup, gate, q_idx], axis=1))


def _proj_a_kernel(x_ref, mod_ref, g_ref, kvg_ref, w_ref, cos_ref, sin_ref,
                   qcat_ref, keys_ref, kdup_ref, qidx_ref, aux_ref, gate_ref):
    hn = _bf16(_prenorm(x_ref[...], g_ref[...], mod_ref))
    cos32, sin32 = cos_ref[0], sin_ref[0]
    cos64, sin64 = cos_ref[1], sin_ref[1]
    lane = _lane_iota(cos32.shape)
    qscale = (A_LATENT + A_ROPE) ** -0.5 * LOG2E

    def proj(name):
        lo, hi = _A_COLS[name]
        return _dot(hn, w_ref[:, lo:hi])

    q_lat = proj("q_lat") * qscale
    q_rope = proj("q_rope")
    per_slab = LANES // A_ROPE
    for h in range(A_HEADS):
        qcat_ref[h, :, 0:A_LATENT] = _bf16(q_lat[:, h * A_LATENT:(h + 1) * A_LATENT])
    for j in range(A_HEADS // per_slab):
        roped = _rope_lanes(q_rope[:, j * LANES:(j + 1) * LANES], cos32, sin32, A_ROPE) * qscale
        for t in range(per_slab):
            piece = roped if t == 0 else pltpu.roll(roped, LANES - A_ROPE * t, axis=1)
            qcat_ref[j * per_slab + t, :, A_LATENT:A_QPAD] = _bf16(jnp.where(lane < A_ROPE, piece, 0.0))

    c_kv = proj("c_kv")
    c_kv = c_kv * lax.rsqrt(jnp.mean(c_kv * c_kv, axis=-1, keepdims=True) + EPS) * kvg_ref[...]
    keys_ref[:, 0:A_LATENT] = _bf16(c_kv)
    aux = proj("aux")
    k_rope = _rope_lanes(aux, cos32, sin32, A_ROPE)
    keys_ref[:, A_LATENT:A_QPAD] = _bf16(jnp.where(lane < A_ROPE, k_rope, 0.0))
    aux_ref[...] = aux * IDX_HEADS ** -0.5

    kdup_ref[...] = _bf16(_rope_lanes(proj("kdup"), cos64, sin64, IDX_DIM))
    q_idx = proj("q_idx")
    per_slab_i = LANES // IDX_DIM
    for j in range(IDX_HEADS // per_slab_i):
        roped = _rope_lanes(q_idx[:, j * LANES:(j + 1) * LANES], cos64, sin64, IDX_DIM)
        for t in range(per_slab_i):
            keep = (lane >= t * IDX_DIM) & (lane < (t + 1) * IDX_DIM)
            qidx_ref[j * per_slab_i + t] = _bf16(jnp.where(keep, roped, 0.0))

    gate_ref[...] = _silu(proj("gate"))


def _proj_a(x2, mod, g, kvg, w, cos, sin, bsz, seq, tm=256):
    n, d = x2.shape
    tps = seq // tm
    row = lambda i: (i, 0)
    bhs = lambda i: (i // tps, 0, i % tps, 0)
    return pl.pallas_call(
        _proj_a_kernel,
        grid=(n // tm,),
        in_specs=[pl.BlockSpec((tm, d), row),
                  pl.BlockSpec((None, 3, d), lambda i: (i // tps, 0, 0)),
                  pl.BlockSpec((1, d), lambda i: (0, 0)),
                  pl.BlockSpec((1, A_LATENT), lambda i: (0, 0)),
                  pl.BlockSpec((d, A_COLS_TOTAL), lambda i: (0, 0)),
                  pl.BlockSpec((2, tm, LANES), lambda i: (0, i, 0)),
                  pl.BlockSpec((2, tm, LANES), lambda i: (0, i, 0))],
        out_specs=[pl.BlockSpec((None, A_HEADS, tm, A_QPAD), bhs),
                   pl.BlockSpec((tm, A_QPAD), row),
                   pl.BlockSpec((tm, LANES), row),
                   pl.BlockSpec((None, IDX_HEADS, tm, LANES), bhs),
                   pl.BlockSpec((tm, LANES), row),
                   pl.BlockSpec((tm, A_WIDTH), row)],
        out_shape=[jax.ShapeDtypeStruct((bsz, A_HEADS, seq, A_QPAD), jnp.bfloat16),
                   jax.ShapeDtypeStruct((n, A_QPAD), jnp.bfloat16),
                   jax.ShapeDtypeStruct((n, LANES), jnp.bfloat16),
                   jax.ShapeDtypeStruct((bsz, IDX_HEADS, seq, LANES), jnp.bfloat16),
                   jax.ShapeDtypeStruct((n, LANES), jnp.float32),
                   jax.ShapeDtypeStruct((n, A_WIDTH), jnp.float32)],
        compiler_params=pltpu.CompilerParams(vmem_limit_bytes=VMEM_LIMIT),
        name="proj_a",
    )(x2, mod, g, kvg, w, cos, sin)


def _attn_a_kernel(qcat_ref, qidx_ref, aux_ref, keys_ref, kdup_ref, gate_ref, x_ref, mod_ref, wuv_ref, wout_ref,
                   out_ref, sc_ref, bias_ref, s_ref, p_ref, m_ref, l_ref, acc_ref, *, topk):
    i = pl.program_id(1)
    n_chunks = (i + 2) // 2
    seq = keys_ref.shape[0] * KEY_CHUNK
    rows = A_HEADS * Q_BLOCK

    w_t = aux_ref[...].T
    qi_all = qidx_ref[...].reshape(IDX_HEADS * Q_BLOCK, LANES)
    qpos = i * Q_BLOCK + lax.broadcasted_iota(jnp.int32, (KEY_CHUNK, Q_BLOCK), 1)
    krow = lax.broadcasted_iota(jnp.int32, (KEY_CHUNK, Q_BLOCK), 0)

    def score_chunk(c, carry):
        rel = _dot_nt(kdup_ref[c], qi_all)
        acc = jnp.zeros((KEY_CHUNK, Q_BLOCK), jnp.float32)
        for h in range(IDX_HEADS):
            acc = acc + w_t[AUX_W_OFF + h:AUX_W_OFF + h + 1, :] * jnp.maximum(rel[:, h * Q_BLOCK:(h + 1) * Q_BLOCK], 0.0)
        bits = lax.bitcast_convert_type(acc, jnp.int32)
        key = bits ^ ((bits >> 31) & 0x7FFFFFFF)
        sc_ref[c] = jnp.where(c * KEY_CHUNK + krow <= qpos, key, INT_MIN)
        return carry

    lax.fori_loop(0, n_chunks, score_chunk, 0)

    def count(pred):
        def body(c, acc):
            hit = jnp.where(pred(sc_ref[c], c), 1, 0)
            return acc + jnp.sum(hit.reshape(KEY_CHUNK // 32, 32, Q_BLOCK), axis=0)
        acc = lax.fori_loop(0, n_chunks, body, jnp.zeros((32, Q_BLOCK), jnp.int32))
        return jnp.sum(acc, axis=0, keepdims=True)

    def search_bit(b, t):
        cand = t + lax.shift_left(jnp.int32(1), 31 - b)
        return jnp.where(count(lambda k, c: k >= cand) >= topk, cand, t)

    all_selected = (i + 1) * Q_BLOCK <= topk
    thr = lax.fori_loop(0, jnp.where(all_selected, 0, 32), search_bit,
                        jnp.full((1, Q_BLOCK), INT_MIN, jnp.int32))

    need = topk - count(lambda k, c: k > thr)
    n_eq = count(lambda k, c: k == thr)
    excess = jnp.max(jnp.where(n_eq > need, 1, 0)) > 0

    def tie_bound():
        def bit(b, bound):
            cand = bound + lax.shift_left(jnp.int32(1), 30 - b)
            below = count(lambda k, c: jnp.where(k == thr, c * KEY_CHUNK + krow, seq) < cand)
            return jnp.where(below < need, cand, bound)
        return lax.fori_loop(0, 31, bit, jnp.zeros((1, Q_BLOCK), jnp.int32))

    tie_idx = lax.cond(jnp.logical_and(excess, jnp.logical_not(all_selected)), tie_bound,
                       lambda: jnp.where(all_selected, -1, seq) + jnp.zeros((1, Q_BLOCK), jnp.int32))

    def bias_chunk(c, carry):
        k = sc_ref[c]
        tie_ok = jnp.where(c * KEY_CHUNK + krow <= tie_idx, 0.0, NEG)
        bias_t = jnp.where(k > thr, 0.0, jnp.where(k == thr, tie_ok, NEG))
        for t in range(KEY_CHUNK // Q_BLOCK):
            bias_ref[c, :, t * Q_BLOCK:(t + 1) * Q_BLOCK] = bias_t[t * Q_BLOCK:(t + 1) * Q_BLOCK, :].T
        return carry

    lax.fori_loop(0, n_chunks, bias_chunk, 0)

    q_all = qcat_ref[...].reshape(rows, A_QPAD)
    m_ref[...] = jnp.full(m_ref.shape, NEG, jnp.float32)
    l_ref[...] = jnp.zeros(l_ref.shape, jnp.float32)
    acc_ref[...] = jnp.zeros(acc_ref.shape, jnp.float32)

    def attend_chunk(c, carry):
        kc = keys_ref[c]
        s_ref[...] = _dot_nt(q_all, kc).reshape(A_HEADS, Q_BLOCK, KEY_CHUNK)
        bias = bias_ref[c]

        def head(h, carry2):
            s = s_ref[h] + bias
            m_old = m_ref[h]
            m_new = jnp.maximum(m_old, jnp.max(s, axis=-1, keepdims=True))
            alpha = jnp.exp2(m_old - m_new)
            p = jnp.exp2(s - m_new)
            l_ref[h] = alpha * l_ref[h] + jnp.sum(p, axis=-1, keepdims=True)
            m_ref[h] = m_new
            acc_ref[h] = acc_ref[h] * alpha
            p_ref[h] = _bf16(p)
            return carry2

        lax.fori_loop(0, A_HEADS, head, 0)
        pv = _dot(p_ref[...].reshape(rows, KEY_CHUNK), kc[:, 0:A_LATENT])
        acc_ref[...] += pv.reshape(A_HEADS, Q_BLOCK, A_LATENT)
        return carry

    lax.fori_loop(0, n_chunks, attend_chunk, 0)

    y = jnp.zeros(out_ref.shape, jnp.float32)
    for h in range(A_HEADS):
        o_lat = _bf16(acc_ref[h] / l_ref[h])
        o = _dot(o_lat, wuv_ref[h]) * gate_ref[:, h * A_VDIM:(h + 1) * A_VDIM]
        y = y + _dot(_bf16(o), wout_ref[h * A_VDIM:(h + 1) * A_VDIM, :])
    out_ref[...] = x_ref[...] + mod_ref[2:3, :] * y


def _attn_a(qcat, qidx, aux, keys, kdup, gate, x2, mod, w_uv, w_out, topk):
    bsz, _, seq, _ = qcat.shape
    d = x2.shape[1]
    nq = seq // Q_BLOCK
    nkc = seq // KEY_CHUNK
    row = lambda b, i: (b * nq + i, 0)
    rows = A_HEADS * Q_BLOCK
    return pl.pallas_call(
        functools.partial(_attn_a_kernel, topk=topk),
        grid=(bsz, nq),
        in_specs=[pl.BlockSpec((None, A_HEADS, Q_BLOCK, A_QPAD), lambda b, i: (b, 0, i, 0)),
                  pl.BlockSpec((None, IDX_HEADS, Q_BLOCK, LANES), lambda b, i: (b, 0, i, 0)),
                  pl.BlockSpec((Q_BLOCK, LANES), row),
                  pl.BlockSpec((None, nkc, KEY_CHUNK, A_QPAD), lambda b, i: (b, 0, 0, 0)),
                  pl.BlockSpec((None, nkc, KEY_CHUNK, LANES), lambda b, i: (b, 0, 0, 0)),
                  pl.BlockSpec((Q_BLOCK, A_WIDTH), row),
                  pl.BlockSpec((Q_BLOCK, d), row),
                  pl.BlockSpec((None, 3, d), lambda b, i: (b, 0, 0)),
                  pl.BlockSpec((A_HEADS, A_LATENT, A_VDIM), lambda b, i: (0, 0, 0)),
                  pl.BlockSpec((A_WIDTH, d), lambda b, i: (0, 0))],
        out_specs=pl.BlockSpec((Q_BLOCK, d), row),
        out_shape=jax.ShapeDtypeStruct(x2.shape, jnp.float32),
        scratch_shapes=[pltpu.VMEM((nkc, KEY_CHUNK, Q_BLOCK), jnp.int32),
                        pltpu.VMEM((nkc, Q_BLOCK, KEY_CHUNK), jnp.float32),
                        pltpu.VMEM((A_HEADS, Q_BLOCK, KEY_CHUNK), jnp.float32),
                        pltpu.VMEM((A_HEADS, Q_BLOCK, KEY_CHUNK), jnp.bfloat16),
                        pltpu.VMEM((A_HEADS, Q_BLOCK, 1), jnp.float32),
                        pltpu.VMEM((A_HEADS, Q_BLOCK, 1), jnp.float32),
                        pltpu.VMEM((A_HEADS, Q_BLOCK, A_LATENT), jnp.float32)],
        compiler_params=pltpu.CompilerParams(vmem_limit_bytes=VMEM_LIMIT),
        name="attn_a",
    )(qcat, qidx, aux, keys.reshape(bsz, nkc, KEY_CHUNK, A_QPAD), kdup.reshape(bsz, nkc, KEY_CHUNK, LANES),
      gate, x2, mod, w_uv, w_out)


def _proj_b_kernel(*refs, with_kv):
    if with_kv:
        (h_ref, mod_ref, g_ref, kvg_ref, wkv_ref, win_ref, cos_ref, sin_ref,
         k_ref, v_ref, q1_ref, q2_ref, q3_ref, gate_ref) = refs
    else:
        h_ref, mod_ref, g_ref, win_ref, cos_ref, sin_ref, q1_ref, q2_ref, q3_ref, gate_ref = refs
    h = h_ref[...]
    cos, sin = cos_ref[...], sin_ref[...]
    width = B_WIDTH

    def rope_heads(t, out_ref, scale):
        for hd in range(B_HEADS):
            sl = slice(hd * B_HDIM, (hd + 1) * B_HDIM)
            out_ref[:, sl] = _bf16(_rope_lanes(t[:, sl], cos, sin, B_HDIM) * scale)

    if with_kv:
        kvn = _bf16(h * lax.rsqrt(jnp.mean(h * h, axis=-1, keepdims=True) + EPS) * kvg_ref[...])
        rope_heads(_dot(kvn, wkv_ref[:, 0:width]), k_ref, 1.0)
        v_ref[...] = _bf16(_dot(kvn, wkv_ref[:, width:2 * width]))
    hn = _bf16(_prenorm(h, g_ref[...], mod_ref))
    qscale = B_HDIM ** -0.5 * LOG2E
    for gi, q_ref in enumerate((q1_ref, q2_ref, q3_ref)):
        rope_heads(_dot(hn, win_ref[:, gi * width:(gi + 1) * width]), q_ref, qscale)
    gate_ref[...] = _silu(_dot(hn, win_ref[:, 3 * width:4 * width]))


def _proj_b(h2, mod, g, kvg, w_kv, w_in, cos, sin, seq, with_kv, tm=256):
    n, d = h2.shape
    tps = seq // tm
    row = lambda i: (i, 0)
    const = lambda i: (0, 0)
    in_specs = [pl.BlockSpec((tm, d), row), pl.BlockSpec((None, 3, d), lambda i: (i // tps, 0, 0)),
                pl.BlockSpec((1, d), const)]
    args = [h2, mod, g]
    if with_kv:
        in_specs += [pl.BlockSpec((1, d), const), pl.BlockSpec(w_kv.shape, const)]
        args += [kvg, w_kv]
    in_specs += [pl.BlockSpec(w_in.shape, const), pl.BlockSpec((tm, LANES), row), pl.BlockSpec((tm, LANES), row)]
    args += [w_in, cos, sin]
    n_bf = 5 if with_kv else 3
    return pl.pallas_call(
        functools.partial(_proj_b_kernel, with_kv=with_kv),
        grid=(n // tm,),
        in_specs=in_specs,
        out_specs=[pl.BlockSpec((tm, B_WIDTH), row)] * (n_bf + 1),
        out_shape=[jax.ShapeDtypeStruct((n, B_WIDTH), jnp.bfloat16)] * n_bf
        + [jax.ShapeDtypeStruct((n, B_WIDTH), jnp.float32)],
        compiler_params=pltpu.CompilerParams(vmem_limit_bytes=VMEM_LIMIT),
        name="proj_b",
    )(*args)


def _dilated_kernel(q_ref, kp_ref, kc_ref, vp_ref, vc_ref, o_ref, lse_ref):
    j = pl.program_id(2)
    w = q_ref.shape[0]
    qi = lax.broadcasted_iota(jnp.int32, (w, w), 0)
    ki = lax.broadcasted_iota(jnp.int32, (w, w), 1)
    bias_prev = jnp.where(jnp.logical_and(ki >= qi, j > 0), 0.0, NEG)
    bias_cur = jnp.where(ki <= qi, 0.0, NEG)
    for hd in range(B_HEADS):
        sl = slice(hd * B_HDIM, (hd + 1) * B_HDIM)
        q = q_ref[:, sl]
        s_prev = _dot_nt(q, kp_ref[:, sl]) + bias_prev
        s_cur = _dot_nt(q, kc_ref[:, sl]) + bias_cur
        m = jnp.maximum(jnp.max(s_prev, axis=-1, keepdims=True), jnp.max(s_cur, axis=-1, keepdims=True))
        p_prev = jnp.exp2(s_prev - m)
        p_cur = jnp.exp2(s_cur - m)
        l = jnp.sum(p_prev, axis=-1, keepdims=True) + jnp.sum(p_cur, axis=-1, keepdims=True)
        o = _dot(_bf16(p_prev), vp_ref[:, sl]) + _dot(_bf16(p_cur), vc_ref[:, sl])
        o_ref[:, sl] = o / l
        lse_ref[:, sl] = jnp.broadcast_to(m + jnp.log2(l), (w, B_HDIM))


def _dilated(q, k, v, bsz, seq, window, dil):
    w = window // dil
    n = seq // dil
    nb = n // w
    view = lambda t: t.reshape(bsz, n, dil * B_WIDTH)
    cur = lambda b, r, j: (b, j, r)
    prev = lambda b, r, j: (b, jnp.maximum(j - 1, 0), r)
    blk = lambda im: pl.BlockSpec((None, w, B_WIDTH), im)
    shape = jax.ShapeDtypeStruct((bsz, n, dil * B_WIDTH), jnp.float32)
    o, lse = pl.pallas_call(
        _dilated_kernel,
        grid=(bsz, dil, nb),
        in_specs=[blk(cur), blk(prev), blk(cur), blk(prev), blk(cur)],
        out_specs=[blk(cur), blk(cur)],
        out_shape=[shape, shape],
        name=f"dilated_{dil}",
    )(view(q), view(k), view(k), view(v), view(v))
    return o.reshape(bsz * seq, B_WIDTH), lse.reshape(bsz * seq, B_WIDTH)


def _merge_kernel(o1, o2, o3, l1, l2, l3, gate_ref, h_ref, mod_ref, wout_ref, fin_ref, out_ref, *, final):
    lses = (l1[...], l2[...], l3[...])
    m = jnp.maximum(jnp.maximum(lses[0], lses[1]), lses[2])
    e = [jnp.exp2(t - m) for t in lses]
    o = (e[0] * o1[...] + e[1] * o2[...] + e[2] * o3[...]) / (e[0] + e[1] + e[2])
    y = _dot(_bf16(o * gate_ref[...]), wout_ref[...])
    h = h_ref[...] + mod_ref[2:3, :] * y
    if final:
        h = h * lax.rsqrt(jnp.mean(h * h, axis=-1, keepdims=True) + EPS) * fin_ref[...]
    out_ref[...] = h


def _merge(os_, lses, gate, h2, mod, w_out, fin, seq, final, tm=256):
    n, d = h2.shape
    tps = seq // tm
    row = lambda i: (i, 0)
    rows = pl.BlockSpec((tm, d), row)
    return pl.pallas_call(
        functools.partial(_merge_kernel, final=final),
        grid=(n // tm,),
        in_specs=[rows] * 8 + [pl.BlockSpec((None, 3, d), lambda i: (i // tps, 0, 0)),
                               pl.BlockSpec(w_out.shape, lambda i: (0, 0)),
                               pl.BlockSpec((1, d), lambda i: (0, 0))],
        out_specs=rows,
        out_shape=jax.ShapeDtypeStruct((n, d), jnp.float32),
        compiler_params=pltpu.CompilerParams(vmem_limit_bytes=VMEM_LIMIT),
        name="merge_out",
    )(*os_, *lses, gate, h2, mod, w_out, fin)


def _final_norm_kernel(h_ref, g_ref, out_ref):
    h = h_ref[...]
    out_ref[...] = h * lax.rsqrt(jnp.mean(h * h, axis=-1, keepdims=True) + EPS) * g_ref[...]


def kernel(x, c, positions, a_norm, a_ada_w, a_ada_b, a_w_in, a_kv_norm, a_w_uv, a_w_out, kv_norm, w_kv, b_norm, b_ada_w, b_ada_b, b_w_in, b_w_out, final_norm):
    bsz, seq, d = x.shape
    n = bsz * seq
    n_a, n_b = a_norm.shape[0], b_norm.shape[0]
    topk = min(TOPK_MAX, seq // 4)
    assert topk % Q_BLOCK == 0 and seq % KEY_CHUNK == 0 and d == B_WIDTH
    assert all(seq % window == 0 for window, _ in B_GROUPS)

    cos, sin = _rope_tables(positions, (A_ROPE, IDX_DIM, B_HDIM))
    h = x.reshape(n, d)
    for li in range(n_a):
        mod = _ada(c, a_ada_w[li], a_ada_b[li])
        qcat, keys, kdup, qidx, aux, gate = _proj_a(
            h, mod, a_norm[li].reshape(1, d), a_kv_norm[li].reshape(1, A_LATENT), _a_weight(a_w_in[li]),
            cos[0:2], sin[0:2], bsz, seq)
        h = _attn_a(qcat, qidx, aux, keys, kdup, gate, h, mod, _bf16(a_w_uv[li]), _bf16(a_w_out[li]), topk)
    k_sh = v_sh = None
    for li in range(n_b):
        mod = _ada(c, b_ada_w[li], b_ada_b[li])
        outs = _proj_b(h, mod, b_norm[li].reshape(1, d), kv_norm.reshape(1, d), _bf16(w_kv), _bf16(b_w_in[li]),
                       cos[2], sin[2], seq, with_kv=(li == 0))
        if li == 0:
            k_sh, v_sh = outs[0], outs[1]
            outs = outs[2:]
        q_groups, gate = outs[:3], outs[3]
        res = [_dilated(qg, k_sh, v_sh, bsz, seq, window, dil) for qg, (window, dil) in zip(q_groups, B_GROUPS)]
        h = _merge([r[0] for r in res], [r[1] for r in res], gate, h, mod, _bf16(b_w_out[li]),
                   final_norm.reshape(1, d), seq, final=(li == n_b - 1))
    if n_b == 0:
        h = pl.pallas_call(
            _final_norm_kernel, grid=(n // 256,),
            in_specs=[pl.BlockSpec((256, d), lambda i: (i, 0)), pl.BlockSpec((1, d), lambda i: (0, 0))],
            out_specs=pl.BlockSpec((256, d), lambda i: (i, 0)),
            out_shape=jax.ShapeDtypeStruct((n, d), jnp.float32), name="final_norm",
        )(h, final_norm.reshape(1, d))
    return h.reshape(bsz, seq, d)
```

```python
import functools
import math

import jax
import jax.numpy as jnp
from jax import lax
from jax.experimental import pallas as pl
from jax.experimental.pallas import tpu as pltpu

ROPE_THETA = 10000.0
EPS = 1e-6
LOG2E = 1.4426950408889634

A_HEADS = 16
A_LATENT = 128
A_ROPE = 32
A_VDIM = 128
A_WIDTH = A_HEADS * A_VDIM
IDX_HEADS = 8
IDX_DIM = 64
TOPK_MAX = 256
Q_BLOCK = 128
KEY_CHUNK = 256
A_QPAD = 256
HEADS_PER_SLAB = 2

B_HEADS = 8
B_HDIM = 128
B_GROUPS = ((128, 1), (512, 4), (2048, 16))
B_WIDTH = B_HEADS * B_HDIM

LANES = 128
NEG = -1e30
INT_MIN = -(2 ** 31)
VMEM_LIMIT = 56 * 1024 * 1024

AUX_W_OFF = A_ROPE


def _bf16(t):
    return t.astype(jnp.bfloat16)


def _dot(a, b):
    return jnp.dot(a, b, preferred_element_type=jnp.float32)


def _dot_nt(a, b):
    return lax.dot_general(a, b, (((1,), (1,)), ((), ())), preferred_element_type=jnp.float32)


def _silu(t):
    return t * (1.0 / (1.0 + jnp.exp(-t)))


def _lane_iota(shape):
    return lax.broadcasted_iota(jnp.int32, shape, len(shape) - 1)


def _ada_kernel(c_ref, w_ref, b_ref, out_ref):
    out_ref[...] = _dot(_bf16(_silu(c_ref[...])), _bf16(w_ref[...])) + b_ref[...]


def _ada(c, w, b):
    bsz, d = c.shape
    out = pl.pallas_call(
        _ada_kernel,
        grid=(3,),
        in_specs=[pl.BlockSpec((bsz, d), lambda j: (0, 0)),
                  pl.BlockSpec((d, d), lambda j: (0, j)),
                  pl.BlockSpec((1, d), lambda j: (0, j))],
        out_specs=pl.BlockSpec((bsz, d), lambda j: (0, j)),
        out_shape=jax.ShapeDtypeStruct((bsz, 3 * d), jnp.float32),
        name="ada_mod",
    )(c, w, b.reshape(1, 3 * d))
    return out.reshape(bsz, 3, d)


def _rope_table_kernel(pos_ref, inv_ref, sign_ref, cos_ref, sin_ref):
    ang = pos_ref[...] * inv_ref[...]
    cos_ref[...] = jnp.cos(ang)
    sin_ref[...] = jnp.sin(ang) * sign_ref[...]


def _rope_lane_consts(dim):
    half = dim // 2
    inv = ROPE_THETA ** (-jnp.arange(0, dim, 2, dtype=jnp.float32) / dim)
    lane = jnp.arange(LANES) % dim
    return inv[lane % half], jnp.where(lane < half, -1.0, 1.0).astype(jnp.float32)


def _rope_tables(positions, dims, tm=512):
    n = positions.size
    posf = jnp.broadcast_to(positions.reshape(n, 1).astype(jnp.float32), (n, LANES))
    consts = [_rope_lane_consts(d) for d in dims]
    inv = jnp.stack([cst[0] for cst in consts]).reshape(len(dims), 1, LANES)
    sign = jnp.stack([cst[1] for cst in consts]).reshape(len(dims), 1, LANES)
    shape = jax.ShapeDtypeStruct((len(dims), n, LANES), jnp.float32)
    return pl.pallas_call(
        _rope_table_kernel,
        grid=(len(dims), n // tm),
        in_specs=[pl.BlockSpec((tm, LANES), lambda t, i: (i, 0)),
                  pl.BlockSpec((None, 1, LANES), lambda t, i: (t, 0, 0)),
                  pl.BlockSpec((None, 1, LANES), lambda t, i: (t, 0, 0))],
        out_specs=[pl.BlockSpec((None, tm, LANES), lambda t, i: (t, i, 0))] * 2,
        out_shape=[shape, shape],
        name="rope_tables",
    )(posf, inv, sign)


def _rope_lanes(xv, cos, sin_signed, dim):
    half = dim // 2
    if dim == LANES:
        partner = pltpu.roll(xv, half, axis=1)
    else:
        first = (_lane_iota(xv.shape) % dim) < half
        partner = jnp.where(first, pltpu.roll(xv, LANES - half, axis=1), pltpu.roll(xv, half, axis=1))
    return xv * cos + partner * sin_signed


def _prenorm(h, g, mod_ref):
    y = h * lax.rsqrt(jnp.mean(h * h, axis=-1, keepdims=True) + EPS) * g
    return y * (1.0 + mod_ref[1:2, :]) + mod_ref[0:1, :]


_A_COLS = {}
_o = 0
for _name, _w in (("q_lat", A_HEADS * A_LATENT), ("q_rope", A_HEADS * A_ROPE), ("c_kv", A_LATENT),
                  ("aux", LANES), ("kdup", LANES), ("gate", A_WIDTH), ("q_idx", IDX_HEADS * IDX_DIM)):
    _A_COLS[_name] = (_o, _o + _w)
    _o += _w
A_COLS_TOTAL = _o


def _a_weight(w_in):
    sizes = (A_HEADS * A_LATENT, A_HEADS * A_ROPE, A_LATENT, A_ROPE, A_WIDTH, IDX_HEADS * IDX_DIM, IDX_DIM, IDX_HEADS)
    parts, o = [], 0
    for s in sizes:
        parts.append(w_in[:, o:o + s])
        o += s
    q_lat, q_rope, c_kv, k_rope, gate, q_idx, k_idx, w_idx = parts
    d = w_in.shape[0]
    aux = jnp.concatenate([k_rope, w_idx, jnp.zeros((d, LANES - A_ROPE - IDX_HEADS), w_in.dtype)], axis=1)
    kdup = jnp.concatenate([k_idx, k_idx], axis=1)
    return _bf16(jnp.concatenate([q_lat, q_rope, c_kv, aux, kdup, gate, q_idx], axis=1))


def _proj_a_kernel(x_ref, mod_ref, g_ref, kvg_ref, w_ref, cos_ref, sin_ref,
                   qcat_ref, keys_ref, vt_ref, kdup_ref, qidx_ref, aux_ref, gate_ref):
    hn = _bf16(_prenorm(x_ref[...], g_ref[...], mod_ref))
    cos32, sin32 = cos_ref[0], sin_ref[0]
    cos64, sin64 = cos_ref[1], sin_ref[1]
    lane = _lane_iota(cos32.shape)
    qscale = (A_LATENT + A_ROPE) ** -0.5 * LOG2E

    def proj(name):
        lo, hi = _A_COLS[name]
        return _dot(hn, w_ref[:, lo:hi])

    q_lat = proj("q_lat") * qscale
    q_rope = proj("q_rope")
    per_slab = LANES // A_ROPE
    for h in range(A_HEADS):
        qcat_ref[h, :, 0:A_LATENT] = _bf16(q_lat[:, h * A_LATENT:(h + 1) * A_LATENT])
    for j in range(A_HEADS // per_slab):
        roped = _rope_lanes(q_rope[:, j * LANES:(j + 1) * LANES], cos32, sin32, A_ROPE) * qscale
        for t in range(per_slab):
            piece = roped if t == 0 else pltpu.roll(roped, LANES - A_ROPE * t, axis=1)
            qcat_ref[j * per_slab + t, :, A_LATENT:A_QPAD] = _bf16(jnp.where(lane < A_ROPE, piece, 0.0))

    c_kv = proj("c_kv")
    c_kv = c_kv * lax.rsqrt(jnp.mean(c_kv * c_kv, axis=-1, keepdims=True) + EPS) * kvg_ref[...]
    keys_ref[:, 0:A_LATENT] = _bf16(c_kv)
    vt_ref[...] = _bf16(c_kv.T)
    aux = proj("aux")
    k_rope = _rope_lanes(aux, cos32, sin32, A_ROPE)
    keys_ref[:, A_LATENT:A_QPAD] = _bf16(jnp.where(lane < A_ROPE, k_rope, 0.0))
    aux_ref[...] = aux * IDX_HEADS ** -0.5

    kdup_ref[...] = _bf16(_rope_lanes(proj("kdup"), cos64, sin64, IDX_DIM))
    q_idx = proj("q_idx")
    per_slab_i = LANES // IDX_DIM
    for j in range(IDX_HEADS // per_slab_i):
        roped = _rope_lanes(q_idx[:, j * LANES:(j + 1) * LANES], cos64, sin64, IDX_DIM)
        for t in range(per_slab_i):
            keep = (lane >= t * IDX_DIM) & (lane < (t + 1) * IDX_DIM)
            qidx_ref[j * per_slab_i + t] = _bf16(jnp.where(keep, roped, 0.0))

    gate_ref[...] = _silu(proj("gate"))


def _proj_a(x2, mod, g, kvg, w, cos, sin, bsz, seq):
    n, d = x2.shape
    tm = KEY_CHUNK
    tps = seq // tm
    row = lambda i: (i, 0)
    bhs = lambda i: (i // tps, 0, i % tps, 0)
    return pl.pallas_call(
        _proj_a_kernel,
        grid=(n // tm,),
        in_specs=[pl.BlockSpec((tm, d), row),
                  pl.BlockSpec((None, 3, d), lambda i: (i // tps, 0, 0)),
                  pl.BlockSpec((1, d), lambda i: (0, 0)),
                  pl.BlockSpec((1, A_LATENT), lambda i: (0, 0)),
                  pl.BlockSpec((d, A_COLS_TOTAL), lambda i: (0, 0)),
                  pl.BlockSpec((2, tm, LANES), lambda i: (0, i, 0)),
                  pl.BlockSpec((2, tm, LANES), lambda i: (0, i, 0))],
        out_specs=[pl.BlockSpec((None, A_HEADS, tm, A_QPAD), bhs),
                   pl.BlockSpec((tm, A_QPAD), row),
                   pl.BlockSpec((None, None, A_LATENT, tm), lambda i: (i // tps, i % tps, 0, 0)),
                   pl.BlockSpec((tm, LANES), row),
                   pl.BlockSpec((None, IDX_HEADS, tm, LANES), bhs),
                   pl.BlockSpec((tm, LANES), row),
                   pl.BlockSpec((tm, A_WIDTH), row)],
        out_shape=[jax.ShapeDtypeStruct((bsz, A_HEADS, seq, A_QPAD), jnp.bfloat16),
                   jax.ShapeDtypeStruct((n, A_QPAD), jnp.bfloat16),
                   jax.ShapeDtypeStruct((bsz, seq // tm, A_LATENT, tm), jnp.bfloat16),
                   jax.ShapeDtypeStruct((n, LANES), jnp.bfloat16),
                   jax.ShapeDtypeStruct((bsz, IDX_HEADS, seq, LANES), jnp.bfloat16),
                   jax.ShapeDtypeStruct((n, LANES), jnp.float32),
                   jax.ShapeDtypeStruct((n, A_WIDTH), jnp.float32)],
        compiler_params=pltpu.CompilerParams(vmem_limit_bytes=VMEM_LIMIT),
        name="proj_a",
    )(x2, mod, g, kvg, w, cos, sin)


def _attn_a_kernel(qcat_ref, qidx_ref, aux_ref, keys_ref, vt_ref, kdup_ref, gate_ref, x_ref, mod_ref, wuv_ref,
                   wout_ref, out_ref, sc_ref, bias_ref, m_ref, l_ref, acc_ref, *, topk):
    i = pl.program_id(1)
    n_chunks = (i + 2) // 2
    seq = keys_ref.shape[0] * KEY_CHUNK

    w_t = aux_ref[...].T
    qi_all = qidx_ref[...].reshape(IDX_HEADS * Q_BLOCK, LANES)
    qpos = i * Q_BLOCK + lax.broadcasted_iota(jnp.int32, (KEY_CHUNK, Q_BLOCK), 1)
    krow = lax.broadcasted_iota(jnp.int32, (KEY_CHUNK, Q_BLOCK), 0)

    def score_chunk(c, carry):
        rel = _dot_nt(kdup_ref[c], qi_all)
        acc = jnp.zeros((KEY_CHUNK, Q_BLOCK), jnp.float32)
        for h in range(IDX_HEADS):
            acc = acc + w_t[AUX_W_OFF + h:AUX_W_OFF + h + 1, :] * jnp.maximum(rel[:, h * Q_BLOCK:(h + 1) * Q_BLOCK], 0.0)
        bits = lax.bitcast_convert_type(acc, jnp.int32)
        key = bits ^ ((bits >> 31) & 0x7FFFFFFF)
        sc_ref[c] = jnp.where(c * KEY_CHUNK + krow <= qpos, key, INT_MIN)
        return carry

    lax.fori_loop(0, n_chunks, score_chunk, 0)

    def count(pred):
        def body(c, acc):
            hit = jnp.where(pred(sc_ref[c], c), 1, 0)
            return acc + jnp.sum(hit.reshape(KEY_CHUNK // 32, 32, Q_BLOCK), axis=0)
        acc = lax.fori_loop(0, n_chunks, body, jnp.zeros((32, Q_BLOCK), jnp.int32))
        return jnp.sum(acc, axis=0, keepdims=True)

    def search_bit(b, t):
        cand = t + lax.shift_left(jnp.int32(1), 31 - b)
        return jnp.where(count(lambda k, c: k >= cand) >= topk, cand, t)

    all_selected = (i + 1) * Q_BLOCK <= topk
    thr = lax.fori_loop(0, jnp.where(all_selected, 0, 32), search_bit,
                        jnp.full((1, Q_BLOCK), INT_MIN, jnp.int32))

    need = topk - count(lambda k, c: k > thr)
    n_eq = count(lambda k, c: k == thr)
    excess = jnp.max(jnp.where(n_eq > need, 1, 0)) > 0

    def tie_bound():
        def bit(b, bound):
            cand = bound + lax.shift_left(jnp.int32(1), 30 - b)
            below = count(lambda k, c: jnp.where(k == thr, c * KEY_CHUNK + krow, seq) < cand)
            return jnp.where(below < need, cand, bound)
        return lax.fori_loop(0, 31, bit, jnp.zeros((1, Q_BLOCK), jnp.int32))

    tie_idx = lax.cond(jnp.logical_and(excess, jnp.logical_not(all_selected)), tie_bound,
                       lambda: jnp.where(all_selected, -1, seq) + jnp.zeros((1, Q_BLOCK), jnp.int32))

    def bias_chunk(c, carry):
        k = sc_ref[c]
        tie_ok = jnp.where(c * KEY_CHUNK + krow <= tie_idx, 0.0, NEG)
        bias_ref[c] = jnp.where(k > thr, 0.0, jnp.where(k == thr, tie_ok, NEG))
        return carry

    lax.fori_loop(0, n_chunks, bias_chunk, 0)

    m_ref[...] = jnp.full(m_ref.shape, NEG, jnp.float32)
    l_ref[...] = jnp.zeros(l_ref.shape, jnp.float32)
    acc_ref[...] = jnp.zeros(acc_ref.shape, jnp.float32)

    def attend_chunk(c, carry):
        kc = keys_ref[c]
        vt = vt_ref[c]
        bias = bias_ref[c]
        bias2 = jnp.concatenate([bias] * HEADS_PER_SLAB, axis=1)
        for pr in range(A_HEADS // HEADS_PER_SLAB):
            q_slab = qcat_ref[pr * HEADS_PER_SLAB:(pr + 1) * HEADS_PER_SLAB].reshape(HEADS_PER_SLAB * Q_BLOCK, A_QPAD)
            s = _dot_nt(kc, q_slab) + bias2
            m_old = m_ref[pr]
            m_new = jnp.maximum(m_old, jnp.max(s, axis=0, keepdims=True))
            alpha = jnp.exp2(m_old - m_new)
            p = jnp.exp2(s - m_new)
            l_ref[pr] = alpha * l_ref[pr] + jnp.sum(p, axis=0, keepdims=True)
            m_ref[pr] = m_new
            acc_ref[pr] = acc_ref[pr] * alpha + _dot(vt, _bf16(p))
        return carry

    lax.fori_loop(0, n_chunks, attend_chunk, 0)

    y = jnp.zeros(out_ref.shape, jnp.float32)
    for pr in range(A_HEADS // HEADS_PER_SLAB):
        o_t = acc_ref[pr] / l_ref[pr]
        for t in range(HEADS_PER_SLAB):
            h = pr * HEADS_PER_SLAB + t
            o_lat = _bf16(o_t[:, t * Q_BLOCK:(t + 1) * Q_BLOCK].T)
            o = _dot(o_lat, wuv_ref[h]) * gate_ref[:, h * A_VDIM:(h + 1) * A_VDIM]
            y = y + _dot(_bf16(o), wout_ref[h * A_VDIM:(h + 1) * A_VDIM, :])
    out_ref[...] = x_ref[...] + mod_ref[2:3, :] * y


def _attn_a(qcat, qidx, aux, keys, vt, kdup, gate, x2, mod, w_uv, w_out, topk):
    bsz, _, seq, _ = qcat.shape
    d = x2.shape[1]
    nq = seq // Q_BLOCK
    nkc = seq // KEY_CHUNK
    row = lambda b, i: (b * nq + i, 0)
    slabs = A_HEADS // HEADS_PER_SLAB
    slab_w = HEADS_PER_SLAB * Q_BLOCK
    return pl.pallas_call(
        functools.partial(_attn_a_kernel, topk=topk),
        grid=(bsz, nq),
        in_specs=[pl.BlockSpec((None, A_HEADS, Q_BLOCK, A_QPAD), lambda b, i: (b, 0, i, 0)),
                  pl.BlockSpec((None, IDX_HEADS, Q_BLOCK, LANES), lambda b, i: (b, 0, i, 0)),
                  pl.BlockSpec((Q_BLOCK, LANES), row),
                  pl.BlockSpec((None, nkc, KEY_CHUNK, A_QPAD), lambda b, i: (b, 0, 0, 0)),
                  pl.BlockSpec((None, nkc, A_LATENT, KEY_CHUNK), lambda b, i: (b, 0, 0, 0)),
                  pl.BlockSpec((None, nkc, KEY_CHUNK, LANES), lambda b, i: (b, 0, 0, 0)),
                  pl.BlockSpec((Q_BLOCK, A_WIDTH), row),
                  pl.BlockSpec((Q_BLOCK, d), row),
                  pl.BlockSpec((None, 3, d), lambda b, i: (b, 0, 0)),
                  pl.BlockSpec((A_HEADS, A_LATENT, A_VDIM), lambda b, i: (0, 0, 0)),
                  pl.BlockSpec((A_WIDTH, d), lambda b, i: (0, 0))],
        out_specs=pl.BlockSpec((Q_BLOCK, d), row),
        out_shape=jax.ShapeDtypeStruct(x2.shape, jnp.float32),
        scratch_shapes=[pltpu.VMEM((nkc, KEY_CHUNK, Q_BLOCK), jnp.int32),
                        pltpu.VMEM((nkc, KEY_CHUNK, Q_BLOCK), jnp.float32),
                        pltpu.VMEM((slabs, 1, slab_w), jnp.float32),
                        pltpu.VMEM((slabs, 1, slab_w), jnp.float32),
                        pltpu.VMEM((slabs, A_LATENT, slab_w), jnp.float32)],
        compiler_params=pltpu.CompilerParams(vmem_limit_bytes=VMEM_LIMIT),
        name="attn_a",
    )(qcat, qidx, aux, keys.reshape(bsz, nkc, KEY_CHUNK, A_QPAD), vt, kdup.reshape(bsz, nkc, KEY_CHUNK, LANES),
      gate, x2, mod, w_uv, w_out)


def _proj_b_kernel(*refs, with_kv):
    if with_kv:
        (h_ref, mod_ref, g_ref, kvg_ref, wkv_ref, win_ref, cos_ref, sin_ref,
         k_ref, v_ref, q1_ref, q2_ref, q3_ref, gate_ref) = refs
    else:
        h_ref, mod_ref, g_ref, win_ref, cos_ref, sin_ref, q1_ref, q2_ref, q3_ref, gate_ref = refs
    h = h_ref[...]
    cos, sin = cos_ref[...], sin_ref[...]
    width = B_WIDTH

    def rope_heads(t, out_ref, scale):
        for hd in range(B_HEADS):
            sl = slice(hd * B_HDIM, (hd + 1) * B_HDIM)
            out_ref[:, sl] = _bf16(_rope_lanes(t[:, sl], cos, sin, B_HDIM) * scale)

    if with_kv:
        kvn = _bf16(h * lax.rsqrt(jnp.mean(h * h, axis=-1, keepdims=True) + EPS) * kvg_ref[...])
        rope_heads(_dot(kvn, wkv_ref[:, 0:width]), k_ref, 1.0)
        v_ref[...] = _bf16(_dot(kvn, wkv_ref[:, width:2 * width]))
    hn = _bf16(_prenorm(h, g_ref[...], mod_ref))
    qscale = B_HDIM ** -0.5 * LOG2E
    for gi, q_ref in enumerate((q1_ref, q2_ref, q3_ref)):
        rope_heads(_dot(hn, win_ref[:, gi * width:(gi + 1) * width]), q_ref, qscale)
    gate_ref[...] = _silu(_dot(hn, win_ref[:, 3 * width:4 * width]))


def _proj_b(h2, mod, g, kvg, w_kv, w_in, cos, sin, seq, with_kv, tm=256):
    n, d = h2.shape
    tps = seq // tm
    row = lambda i: (i, 0)
    const = lambda i: (0, 0)
    in_specs = [pl.BlockSpec((tm, d), row), pl.BlockSpec((None, 3, d), lambda i: (i // tps, 0, 0)),
                pl.BlockSpec((1, d), const)]
    args = [h2, mod, g]
    if with_kv:
        in_specs += [pl.BlockSpec((1, d), const), pl.BlockSpec(w_kv.shape, const)]
        args += [kvg, w_kv]
    in_specs += [pl.BlockSpec(w_in.shape, const), pl.BlockSpec((tm, LANES), row), pl.BlockSpec((tm, LANES), row)]
    args += [w_in, cos, sin]
    n_bf = 5 if with_kv else 3
    return pl.pallas_call(
        functools.partial(_proj_b_kernel, with_kv=with_kv),
        grid=(n // tm,),
        in_specs=in_specs,
        out_specs=[pl.BlockSpec((tm, B_WIDTH), row)] * (n_bf + 1),
        out_shape=[jax.ShapeDtypeStruct((n, B_WIDTH), jnp.bfloat16)] * n_bf
        + [jax.ShapeDtypeStruct((n, B_WIDTH), jnp.float32)],
        compiler_params=pltpu.CompilerParams(vmem_limit_bytes=VMEM_LIMIT),
        name="proj_b",
    )(*args)


def _dilated_kernel(q_ref, kp_ref, kc_ref, vp_ref, vc_ref, o_ref, lse_ref):
    j = pl.program_id(2)
    w = q_ref.shape[0]
    qi = lax.broadcasted_iota(jnp.int32, (w, w), 0)
    ki = lax.broadcasted_iota(jnp.int32, (w, w), 1)
    bias_prev = jnp.where(jnp.logical_and(ki >= qi, j > 0), 0.0, NEG)
    bias_cur = jnp.where(ki <= qi, 0.0, NEG)
    for hd in range(B_HEADS):
        sl = slice(hd * B_HDIM, (hd + 1) * B_HDIM)
        q = q_ref[:, sl]
        s_prev = _dot_nt(q, kp_ref[:, sl]) + bias_prev
        s_cur = _dot_nt(q, kc_ref[:, sl]) + bias_cur
        m = jnp.maximum(jnp.max(s_prev, axis=-1, keepdims=True), jnp.max(s_cur, axis=-1, keepdims=True))
        p_prev = jnp.exp2(s_prev - m)
        p_cur = jnp.exp2(s_cur - m)
        l = jnp.sum(p_prev, axis=-1, keepdims=True) + jnp.sum(p_cur, axis=-1, keepdims=True)
        o = _dot(_bf16(p_prev), vp_ref[:, sl]) + _dot(_bf16(p_cur), vc_ref[:, sl])
        o_ref[:, sl] = o / l
        lse_ref[:, sl] = jnp.broadcast_to(m + jnp.log2(l), (w, B_HDIM))


def _dilated(q, k, v, bsz, seq, window, dil):
    w = window // dil
    n = seq // dil
    nb = n // w
    view = lambda t: t.reshape(bsz, n, dil * B_WIDTH)
    cur = lambda b, r, j: (b, j, r)
    prev = lambda b, r, j: (b, jnp.maximum(j - 1, 0), r)
    blk = lambda im: pl.BlockSpec((None, w, B_WIDTH), im)
    shape = jax.ShapeDtypeStruct((bsz, n, dil * B_WIDTH), jnp.float32)
    o, lse = pl.pallas_call(
        _dilated_kernel,
        grid=(bsz, dil, nb),
        in_specs=[blk(cur), blk(prev), blk(cur), blk(prev), blk(cur)],
        out_specs=[blk(cur), blk(cur)],
        out_shape=[shape, shape],
        name=f"dilated_{dil}",
    )(view(q), view(k), view(k), view(v), view(v))
    return o.reshape(bsz * seq, B_WIDTH), lse.reshape(bsz * seq, B_WIDTH)


def _merge_kernel(o1, o2, o3, l1, l2, l3, gate_ref, h_ref, mod_ref, wout_ref, fin_ref, out_ref, *, final):
    lses = (l1[...], l2[...], l3[...])
    m = jnp.maximum(jnp.maximum(lses[0], lses[1]), lses[2])
    e = [jnp.exp2(t - m) for t in lses]
    o = (e[0] * o1[...] + e[1] * o2[...] + e[2] * o3[...]) / (e[0] + e[1] + e[2])
    y = _dot(_bf16(o * gate_ref[...]), wout_ref[...])
    h = h_ref[...] + mod_ref[2:3, :] * y
    if final:
        h = h * lax.rsqrt(jnp.mean(h * h, axis=-1, keepdims=True) + EPS) * fin_ref[...]
    out_ref[...] = h


def _merge(os_, lses, gate, h2, mod, w_out, fin, seq, final, tm=256):
    n, d = h2.shape
    tps = seq // tm
    row = lambda i: (i, 0)
    rows = pl.BlockSpec((tm, d), row)
    return pl.pallas_call(
        functools.partial(_merge_kernel, final=final),
        grid=(n // tm,),
        in_specs=[rows] * 8 + [pl.BlockSpec((None, 3, d), lambda i: (i // tps, 0, 0)),
                               pl.BlockSpec(w_out.shape, lambda i: (0, 0)),
                               pl.BlockSpec((1, d), lambda i: (0, 0))],
        out_specs=rows,
        out_shape=jax.ShapeDtypeStruct((n, d), jnp.float32),
        compiler_params=pltpu.CompilerParams(vmem_limit_bytes=VMEM_LIMIT),
        name="merge_out",
    )(*os_, *lses, gate, h2, mod, w_out, fin)


def _final_norm_kernel(h_ref, g_ref, out_ref):
    h = h_ref[...]
    out_ref[...] = h * lax.rsqrt(jnp.mean(h * h, axis=-1, keepdims=True) + EPS) * g_ref[...]


def kernel(x, c, positions, a_norm, a_ada_w, a_ada_b, a_w_in, a_kv_norm, a_w_uv, a_w_out, kv_norm, w_kv, b_norm, b_ada_w, b_ada_b, b_w_in, b_w_out, final_norm):
    bsz, seq, d = x.shape
    n = bsz * seq
    n_a, n_b = a_norm.shape[0], b_norm.shape[0]
    topk = min(TOPK_MAX, seq // 4)
    assert topk % Q_BLOCK == 0 and seq % KEY_CHUNK == 0 and d == B_WIDTH
    assert all(seq % window == 0 for window, _ in B_GROUPS)

    cos, sin = _rope_tables(positions, (A_ROPE, IDX_DIM, B_HDIM))
    h = x.reshape(n, d)
    for li in range(n_a):
        mod = _ada(c, a_ada_w[li], a_ada_b[li])
        qcat, keys, vt, kdup, qidx, aux, gate = _proj_a(
            h, mod, a_norm[li].reshape(1, d), a_kv_norm[li].reshape(1, A_LATENT), _a_weight(a_w_in[li]),
            cos[0:2], sin[0:2], bsz, seq)
        h = _attn_a(qcat, qidx, aux, keys, vt, kdup, gate, h, mod, _bf16(a_w_uv[li]), _bf16(a_w_out[li]), topk)
    k_sh = v_sh = None
    for li in range(n_b):
        mod = _ada(c, b_ada_w[li], b_ada_b[li])
        outs = _proj_b(h, mod, b_norm[li].reshape(1, d), kv_norm.reshape(1, d), _bf16(w_kv), _bf16(b_w_in[li]),
                       cos[2], sin[2], seq, with_kv=(li == 0))
        if li == 0:
            k_sh, v_sh = outs[0], outs[1]
            outs = outs[2:]
        q_groups, gate = outs[:3], outs[3]
        res = [_dilated(qg, k_sh, v_sh, bsz, seq, window, dil) for qg, (window, dil) in zip(q_groups, B_GROUPS)]
        h = _merge([r[0] for r in res], [r[1] for r in res], gate, h, mod, _bf16(b_w_out[li]),
                   final_norm.reshape(1, d), seq, final=(li == n_b - 1))
    if n_b == 0:
        h = pl.pallas_call(
            _final_norm_kernel, grid=(n // 256,),
            in_specs=[pl.BlockSpec((256, d), lambda i: (i, 0)), pl.BlockSpec((1, d), lambda i: (0, 0))],
            out_specs=pl.BlockSpec((256, d), lambda i: (i, 0)),
            out_shape=jax.ShapeDtypeStruct((n, d), jnp.float32), name="final_norm",
        )(h, final_norm.reshape(1, d))
    return h.reshape(bsz, seq, d)
```

```python
import functools
import math

import jax
import jax.numpy as jnp
from jax import lax
from jax.experimental import pallas as pl
from jax.experimental.pallas import tpu as pltpu

ROPE_THETA = 10000.0
EPS = 1e-6
LOG2E = 1.4426950408889634

A_HEADS = 16
A_LATENT = 128
A_ROPE = 32
A_VDIM = 128
A_WIDTH = A_HEADS * A_VDIM
IDX_HEADS = 8
IDX_DIM = 64
TOPK_MAX = 256
Q_BLOCK = 128
KEY_CHUNK = 256
A_QPAD = 256
HEADS_PER_SLAB = 2

B_HEADS = 8
B_HDIM = 128
B_GROUPS = ((128, 1), (512, 4), (2048, 16))
B_WIDTH = B_HEADS * B_HDIM

LANES = 128
NEG = -1e30
INT_MIN = -(2 ** 31)
HALF = 2 ** 15
VMEM_LIMIT = 56 * 1024 * 1024

AUX_W_OFF = A_ROPE


def _bf16(t):
    return t.astype(jnp.bfloat16)


def _dot(a, b):
    return jnp.dot(a, b, preferred_element_type=jnp.float32)


def _dot_nt(a, b):
    return lax.dot_general(a, b, (((1,), (1,)), ((), ())), preferred_element_type=jnp.float32)


def _silu(t):
    return t * (1.0 / (1.0 + jnp.exp(-t)))


def _lane_iota(shape):
    return lax.broadcasted_iota(jnp.int32, shape, len(shape) - 1)


def _ada_kernel(c_ref, w_ref, b_ref, out_ref):
    out_ref[...] = _dot(_bf16(_silu(c_ref[...])), _bf16(w_ref[...])) + b_ref[...]


def _ada(c, w, b):
    bsz, d = c.shape
    out = pl.pallas_call(
        _ada_kernel,
        grid=(3,),
        in_specs=[pl.BlockSpec((bsz, d), lambda j: (0, 0)),
                  pl.BlockSpec((d, d), lambda j: (0, j)),
                  pl.BlockSpec((1, d), lambda j: (0, j))],
        out_specs=pl.BlockSpec((bsz, d), lambda j: (0, j)),
        out_shape=jax.ShapeDtypeStruct((bsz, 3 * d), jnp.float32),
        name="ada_mod",
    )(c, w, b.reshape(1, 3 * d))
    return out.reshape(bsz, 3, d)


def _rope_table_kernel(pos_ref, inv_ref, sign_ref, cos_ref, sin_ref):
    ang = pos_ref[...] * inv_ref[...]
    cos_ref[...] = jnp.cos(ang)
    sin_ref[...] = jnp.sin(ang) * sign_ref[...]


def _rope_lane_consts(dim):
    half = dim // 2
    inv = ROPE_THETA ** (-jnp.arange(0, dim, 2, dtype=jnp.float32) / dim)
    lane = jnp.arange(LANES) % dim
    return inv[lane % half], jnp.where(lane < half, -1.0, 1.0).astype(jnp.float32)


def _rope_tables(positions, dims, tm=512):
    n = positions.size
    posf = jnp.broadcast_to(positions.reshape(n, 1).astype(jnp.float32), (n, LANES))
    consts = [_rope_lane_consts(d) for d in dims]
    inv = jnp.stack([cst[0] for cst in consts]).reshape(len(dims), 1, LANES)
    sign = jnp.stack([cst[1] for cst in consts]).reshape(len(dims), 1, LANES)
    shape = jax.ShapeDtypeStruct((len(dims), n, LANES), jnp.float32)
    return pl.pallas_call(
        _rope_table_kernel,
        grid=(len(dims), n // tm),
        in_specs=[pl.BlockSpec((tm, LANES), lambda t, i: (i, 0)),
                  pl.BlockSpec((None, 1, LANES), lambda t, i: (t, 0, 0)),
                  pl.BlockSpec((None, 1, LANES), lambda t, i: (t, 0, 0))],
        out_specs=[pl.BlockSpec((None, tm, LANES), lambda t, i: (t, i, 0))] * 2,
        out_shape=[shape, shape],
        name="rope_tables",
    )(posf, inv, sign)


def _rope_lanes(xv, cos, sin_signed, dim):
    half = dim // 2
    if dim == LANES:
        partner = pltpu.roll(xv, half, axis=1)
    else:
        first = (_lane_iota(xv.shape) % dim) < half
        partner = jnp.where(first, pltpu.roll(xv, LANES - half, axis=1), pltpu.roll(xv, half, axis=1))
    return xv * cos + partner * sin_signed


def _prenorm(h, g, mod_ref):
    y = h * lax.rsqrt(jnp.mean(h * h, axis=-1, keepdims=True) + EPS) * g
    return y * (1.0 + mod_ref[1:2, :]) + mod_ref[0:1, :]


_A_COLS = {}
_o = 0
for _name, _w in (("q_lat", A_HEADS * A_LATENT), ("q_rope", A_HEADS * A_ROPE), ("c_kv", A_LATENT),
                  ("aux", LANES), ("kdup", LANES), ("gate", A_WIDTH), ("q_idx", IDX_HEADS * IDX_DIM)):
    _A_COLS[_name] = (_o, _o + _w)
    _o += _w
A_COLS_TOTAL = _o


def _a_weight(w_in):
    sizes = (A_HEADS * A_LATENT, A_HEADS * A_ROPE, A_LATENT, A_ROPE, A_WIDTH, IDX_HEADS * IDX_DIM, IDX_DIM, IDX_HEADS)
    parts, o = [], 0
    for s in sizes:
        parts.append(w_in[:, o:o + s])
        o += s
    q_lat, q_rope, c_kv, k_rope, gate, q_idx, k_idx, w_idx = parts
    d = w_in.shape[0]
    aux = jnp.concatenate([k_rope, w_idx, jnp.zeros((d, LANES - A_ROPE - IDX_HEADS), w_in.dtype)], axis=1)
    kdup = jnp.concatenate([k_idx, k_idx], axis=1)
    return _bf16(jnp.concatenate([q_lat, q_rope, c_kv, aux, kdup, gate, q_idx], axis=1))


def _proj_a_kernel(x_ref, mod_ref, g_ref, kvg_ref, w_ref, cos_ref, sin_ref,
                   qcat_ref, keys_ref, vt_ref, kdup_ref, qidx_ref, aux_ref, gate_ref):
    hn = _bf16(_prenorm(x_ref[...], g_ref[...], mod_ref))
    cos32, sin32 = cos_ref[0], sin_ref[0]
    cos64, sin64 = cos_ref[1], sin_ref[1]
    lane = _lane_iota(cos32.shape)
    qscale = (A_LATENT + A_ROPE) ** -0.5 * LOG2E

    def proj(name):
        lo, hi = _A_COLS[name]
        return _dot(hn, w_ref[:, lo:hi])

    q_lat = proj("q_lat") * qscale
    q_rope = proj("q_rope")
    per_slab = LANES // A_ROPE
    for h in range(A_HEADS):
        qcat_ref[h, :, 0:A_LATENT] = _bf16(q_lat[:, h * A_LATENT:(h + 1) * A_LATENT])
    for j in range(A_HEADS // per_slab):
        roped = _rope_lanes(q_rope[:, j * LANES:(j + 1) * LANES], cos32, sin32, A_ROPE) * qscale
        for t in range(per_slab):
            piece = roped if t == 0 else pltpu.roll(roped, LANES - A_ROPE * t, axis=1)
            qcat_ref[j * per_slab + t, :, A_LATENT:A_QPAD] = _bf16(jnp.where(lane < A_ROPE, piece, 0.0))

    c_kv = proj("c_kv")
    c_kv = c_kv * lax.rsqrt(jnp.mean(c_kv * c_kv, axis=-1, keepdims=True) + EPS) * kvg_ref[...]
    keys_ref[:, 0:A_LATENT] = _bf16(c_kv)
    vt_ref[...] = _bf16(c_kv.T)
    aux = proj("aux")
    k_rope = _rope_lanes(aux, cos32, sin32, A_ROPE)
    keys_ref[:, A_LATENT:A_QPAD] = _bf16(jnp.where(lane < A_ROPE, k_rope, 0.0))
    aux_ref[...] = aux * IDX_HEADS ** -0.5

    kdup_ref[...] = _bf16(_rope_lanes(proj("kdup"), cos64, sin64, IDX_DIM))
    q_idx = proj("q_idx")
    per_slab_i = LANES // IDX_DIM
    for j in range(IDX_HEADS // per_slab_i):
        roped = _rope_lanes(q_idx[:, j * LANES:(j + 1) * LANES], cos64, sin64, IDX_DIM)
        for t in range(per_slab_i):
            keep = (lane >= t * IDX_DIM) & (lane < (t + 1) * IDX_DIM)
            qidx_ref[j * per_slab_i + t] = _bf16(jnp.where(keep, roped, 0.0))

    gate_ref[...] = _silu(proj("gate"))


def _proj_a(x2, mod, g, kvg, w, cos, sin, bsz, seq):
    n, d = x2.shape
    tm = KEY_CHUNK
    tps = seq // tm
    row = lambda i: (i, 0)
    bhs = lambda i: (i // tps, 0, i % tps, 0)
    return pl.pallas_call(
        _proj_a_kernel,
        grid=(n // tm,),
        in_specs=[pl.BlockSpec((tm, d), row),
                  pl.BlockSpec((None, 3, d), lambda i: (i // tps, 0, 0)),
                  pl.BlockSpec((1, d), lambda i: (0, 0)),
                  pl.BlockSpec((1, A_LATENT), lambda i: (0, 0)),
                  pl.BlockSpec((d, A_COLS_TOTAL), lambda i: (0, 0)),
                  pl.BlockSpec((2, tm, LANES), lambda i: (0, i, 0)),
                  pl.BlockSpec((2, tm, LANES), lambda i: (0, i, 0))],
        out_specs=[pl.BlockSpec((None, A_HEADS, tm, A_QPAD), bhs),
                   pl.BlockSpec((tm, A_QPAD), row),
                   pl.BlockSpec((None, None, A_LATENT, tm), lambda i: (i // tps, i % tps, 0, 0)),
                   pl.BlockSpec((tm, LANES), row),
                   pl.BlockSpec((None, IDX_HEADS, tm, LANES), bhs),
                   pl.BlockSpec((tm, LANES), row),
                   pl.BlockSpec((tm, A_WIDTH), row)],
        out_shape=[jax.ShapeDtypeStruct((bsz, A_HEADS, seq, A_QPAD), jnp.bfloat16),
                   jax.ShapeDtypeStruct((n, A_QPAD), jnp.bfloat16),
                   jax.ShapeDtypeStruct((bsz, seq // tm, A_LATENT, tm), jnp.bfloat16),
                   jax.ShapeDtypeStruct((n, LANES), jnp.bfloat16),
                   jax.ShapeDtypeStruct((bsz, IDX_HEADS, seq, LANES), jnp.bfloat16),
                   jax.ShapeDtypeStruct((n, LANES), jnp.float32),
                   jax.ShapeDtypeStruct((n, A_WIDTH), jnp.float32)],
        compiler_params=pltpu.CompilerParams(vmem_limit_bytes=VMEM_LIMIT),
        name="proj_a",
    )(x2, mod, g, kvg, w, cos, sin)


def _attn_a_kernel(qcat_ref, qidx_ref, aux_ref, keys_ref, vt_ref, kdup_ref, gate_ref, x_ref, mod_ref, wuv_ref,
                   wout_ref, out_ref, sc_ref, hi_ref, lo_ref, lo2_ref, bias_ref, sa_ref, sb_ref, mxa_ref, mxb_ref, m_ref, l_ref, acc_ref, og_ref, *, topk):
    i = pl.program_id(1)
    n_chunks = (i + 2) // 2
    seq = keys_ref.shape[0] * KEY_CHUNK

    w_t = aux_ref[...].T
    qi_all = qidx_ref[...].reshape(IDX_HEADS * Q_BLOCK, LANES)
    qpos = i * Q_BLOCK + lax.broadcasted_iota(jnp.int32, (KEY_CHUNK, Q_BLOCK), 1)
    krow = lax.broadcasted_iota(jnp.int32, (KEY_CHUNK, Q_BLOCK), 0)

    n_pairs = (n_chunks + 1) // 2

    def for_chunk_pairs(fn, init):
        def body(t, carry):
            return fn(2 * t + 1, fn(2 * t, carry))
        return lax.fori_loop(0, n_pairs, body, init)

    def score_chunk(c, carry):
        rel = _dot_nt(kdup_ref[c], qi_all)
        acc = jnp.zeros((KEY_CHUNK, Q_BLOCK), jnp.float32)
        for h in range(IDX_HEADS):
            acc = acc + w_t[AUX_W_OFF + h:AUX_W_OFF + h + 1, :] * jnp.maximum(rel[:, h * Q_BLOCK:(h + 1) * Q_BLOCK], 0.0)
        bits = lax.bitcast_convert_type(acc, jnp.int32)
        key = bits ^ ((bits >> 31) & 0x7FFFFFFF)
        key = jnp.where(c * KEY_CHUNK + krow <= qpos, key, INT_MIN)
        sc_ref[c] = key
        hi_ref[c] = (key >> 16).astype(jnp.int16)
        lo_ref[c] = ((key & 0xFFFF) - HALF).astype(jnp.int16)
        return carry

    for_chunk_pairs(score_chunk, 0)

    def count(pred):
        def body(c, acc):
            hit = jnp.where(pred(sc_ref[c], c), 1, 0)
            return acc + jnp.sum(hit.reshape(KEY_CHUNK // 32, 32, Q_BLOCK), axis=0)
        acc = for_chunk_pairs(body, jnp.zeros((32, Q_BLOCK), jnp.int32))
        return jnp.sum(acc, axis=0, keepdims=True)

    def count16(ref16, pred):
        def body(c, acc):
            hit = jnp.where(pred(ref16[c]), jnp.int16(1), jnp.int16(0))
            words = pltpu.bitcast(hit, jnp.int32)
            return acc + jnp.sum(words.reshape(KEY_CHUNK // 64, 32, Q_BLOCK), axis=0)
        acc = for_chunk_pairs(body, jnp.zeros((32, Q_BLOCK), jnp.int32))
        acc = jnp.sum(acc, axis=0, keepdims=True)
        return (acc & 0xFFFF) + (acc >> 16)

    def bisect16(ref16, target, n_bits):
        def bit(b, t):
            cand = t + lax.shift_left(jnp.int32(1), 15 - b)
            cand16 = cand.astype(jnp.int16)
            return jnp.where(count16(ref16, lambda v: v >= cand16) >= target, cand, t)
        return lax.fori_loop(0, n_bits, bit, jnp.full((1, Q_BLOCK), -HALF, jnp.int32))

    all_selected = (i + 1) * Q_BLOCK <= topk
    n_bits = jnp.where(all_selected, 0, 16)
    t_hi = bisect16(hi_ref, topk, n_bits)
    t_hi16 = t_hi.astype(jnp.int16)
    above = count16(hi_ref, lambda v: v > t_hi16)

    def low_chunk(c, carry):
        lo2_ref[c] = jnp.where(hi_ref[c] == t_hi16, lo_ref[c], jnp.int16(-HALF))
        return carry

    for_chunk_pairs(low_chunk, 0)
    t_lo = bisect16(lo2_ref, topk - above, n_bits)
    thr = lax.shift_left(t_hi, 16) + (t_lo + HALF)

    need = topk - count(lambda k, c: k > thr)
    n_eq = count(lambda k, c: k == thr)
    excess = jnp.max(jnp.where(n_eq > need, 1, 0)) > 0
    idx_bits = (seq - 1).bit_length()

    def tie_bound():
        def bit(b, bound):
            cand = bound + lax.shift_left(jnp.int32(1), idx_bits - 1 - b)
            below = count(lambda k, c: jnp.where(k == thr, c * KEY_CHUNK + krow, seq) < cand)
            return jnp.where(below < need, cand, bound)
        return lax.fori_loop(0, idx_bits, bit, jnp.zeros((1, Q_BLOCK), jnp.int32))

    tie_idx = lax.cond(jnp.logical_and(excess, jnp.logical_not(all_selected)), tie_bound,
                       lambda: jnp.where(all_selected, -1, seq) + jnp.zeros((1, Q_BLOCK), jnp.int32))

    def bias_chunk(c, carry):
        k = sc_ref[c]
        tie_ok = jnp.where(c * KEY_CHUNK + krow <= tie_idx, 0.0, NEG)
        bias_ref[c] = jnp.where(k > thr, 0.0, jnp.where(k == thr, tie_ok, NEG))
        return carry

    for_chunk_pairs(bias_chunk, 0)

    m_ref[...] = jnp.full(m_ref.shape, NEG, jnp.float32)
    l_ref[...] = jnp.zeros(l_ref.shape, jnp.float32)
    acc_ref[...] = jnp.zeros(acc_ref.shape, jnp.float32)

    slabs = A_HEADS // HEADS_PER_SLAB
    last_chunk = 2 * n_pairs - 1

    def scores(c, s_buf, mx_buf):
        c = jnp.minimum(c, last_chunk)
        kc = keys_ref[c]
        bias2 = jnp.concatenate([bias_ref[c]] * HEADS_PER_SLAB, axis=1)
        for pr in range(slabs):
            q_slab = qcat_ref[pr * HEADS_PER_SLAB:(pr + 1) * HEADS_PER_SLAB].reshape(HEADS_PER_SLAB * Q_BLOCK, A_QPAD)
            s = _dot_nt(kc, q_slab) + bias2
            s_buf[pr] = s
            mx_buf[pr] = jnp.max(s, axis=0, keepdims=True)

    def accumulate(c, s_buf, mx_buf):
        vt = vt_ref[c]
        for pr in range(slabs):
            m_old = m_ref[pr]
            m_new = jnp.maximum(m_old, mx_buf[pr])
            alpha = jnp.exp2(m_old - m_new)
            p = jnp.exp2(s_buf[pr] - m_new)
            l_ref[pr] = alpha * l_ref[pr] + jnp.sum(p, axis=0, keepdims=True)
            m_ref[pr] = m_new
            acc_ref[pr] = acc_ref[pr] * alpha + _dot(vt, _bf16(p))

    scores(0, sa_ref, mxa_ref)

    def attend_pair(t, carry):
        scores(2 * t + 1, sb_ref, mxb_ref)
        accumulate(2 * t, sa_ref, mxa_ref)
        scores(2 * t + 2, sa_ref, mxa_ref)
        accumulate(2 * t + 1, sb_ref, mxb_ref)
        return carry

    lax.fori_loop(0, n_pairs, attend_pair, 0)

    for pr in range(slabs):
        o_t = acc_ref[pr] / l_ref[pr]
        for t in range(HEADS_PER_SLAB):
            h = pr * HEADS_PER_SLAB + t
            o_lat = _bf16(o_t[:, t * Q_BLOCK:(t + 1) * Q_BLOCK].T)
            o = _dot(o_lat, wuv_ref[h]) * gate_ref[:, h * A_VDIM:(h + 1) * A_VDIM]
            og_ref[:, h * A_VDIM:(h + 1) * A_VDIM] = _bf16(o)
    out_ref[...] = x_ref[...] + mod_ref[2:3, :] * _dot(og_ref[...], wout_ref[...])


def _attn_a(qcat, qidx, aux, keys, vt, kdup, gate, x2, mod, w_uv, w_out, topk):
    bsz, _, seq, _ = qcat.shape
    d = x2.shape[1]
    nq = seq // Q_BLOCK
    nkc = seq // KEY_CHUNK
    row = lambda b, i: (b * nq + i, 0)
    slabs = A_HEADS // HEADS_PER_SLAB
    slab_w = HEADS_PER_SLAB * Q_BLOCK
    return pl.pallas_call(
        functools.partial(_attn_a_kernel, topk=topk),
        grid=(bsz, nq),
        in_specs=[pl.BlockSpec((None, A_HEADS, Q_BLOCK, A_QPAD), lambda b, i: (b, 0, i, 0)),
                  pl.BlockSpec((None, IDX_HEADS, Q_BLOCK, LANES), lambda b, i: (b, 0, i, 0)),
                  pl.BlockSpec((Q_BLOCK, LANES), row),
                  pl.BlockSpec((None, nkc, KEY_CHUNK, A_QPAD), lambda b, i: (b, 0, 0, 0)),
                  pl.BlockSpec((None, nkc, A_LATENT, KEY_CHUNK), lambda b, i: (b, 0, 0, 0)),
                  pl.BlockSpec((None, nkc, KEY_CHUNK, LANES), lambda b, i: (b, 0, 0, 0)),
                  pl.BlockSpec((Q_BLOCK, A_WIDTH), row),
                  pl.BlockSpec((Q_BLOCK, d), row),
                  pl.BlockSpec((None, 3, d), lambda b, i: (b, 0, 0)),
                  pl.BlockSpec((A_HEADS, A_LATENT, A_VDIM), lambda b, i: (0, 0, 0)),
                  pl.BlockSpec((A_WIDTH, d), lambda b, i: (0, 0))],
        out_specs=pl.BlockSpec((Q_BLOCK, d), row),
        out_shape=jax.ShapeDtypeStruct(x2.shape, jnp.float32),
        scratch_shapes=[pltpu.VMEM((nkc, KEY_CHUNK, Q_BLOCK), jnp.int32),
                        pltpu.VMEM((nkc, KEY_CHUNK, Q_BLOCK), jnp.int16),
                        pltpu.VMEM((nkc, KEY_CHUNK, Q_BLOCK), jnp.int16),
                        pltpu.VMEM((nkc, KEY_CHUNK, Q_BLOCK), jnp.int16),
                        pltpu.VMEM((nkc, KEY_CHUNK, Q_BLOCK), jnp.float32),
                        pltpu.VMEM((slabs, KEY_CHUNK, slab_w), jnp.float32),
                        pltpu.VMEM((slabs, KEY_CHUNK, slab_w), jnp.float32),
                        pltpu.VMEM((slabs, 1, slab_w), jnp.float32),
                        pltpu.VMEM((slabs, 1, slab_w), jnp.float32),
                        pltpu.VMEM((slabs, 1, slab_w), jnp.float32),
                        pltpu.VMEM((slabs, 1, slab_w), jnp.float32),
                        pltpu.VMEM((slabs, A_LATENT, slab_w), jnp.float32),
                        pltpu.VMEM((Q_BLOCK, A_WIDTH), jnp.bfloat16)],
        compiler_params=pltpu.CompilerParams(vmem_limit_bytes=VMEM_LIMIT),
        name="attn_a",
    )(qcat, qidx, aux, keys.reshape(bsz, nkc, KEY_CHUNK, A_QPAD), vt, kdup.reshape(bsz, nkc, KEY_CHUNK, LANES),
      gate, x2, mod, w_uv, w_out)


def _proj_b_kernel(*refs, with_kv):
    if with_kv:
        (h_ref, mod_ref, g_ref, kvg_ref, wkv_ref, win_ref, cos_ref, sin_ref,
         k_ref, v_ref, q1_ref, q2_ref, q3_ref, gate_ref) = refs
    else:
        h_ref, mod_ref, g_ref, win_ref, cos_ref, sin_ref, q1_ref, q2_ref, q3_ref, gate_ref = refs
    h = h_ref[...]
    cos, sin = cos_ref[...], sin_ref[...]
    width = B_WIDTH

    def rope_heads(t, out_ref, scale):
        for hd in range(B_HEADS):
            sl = slice(hd * B_HDIM, (hd + 1) * B_HDIM)
            out_ref[:, sl] = _bf16(_rope_lanes(t[:, sl], cos, sin, B_HDIM) * scale)

    if with_kv:
        kvn = _bf16(h * lax.rsqrt(jnp.mean(h * h, axis=-1, keepdims=True) + EPS) * kvg_ref[...])
        rope_heads(_dot(kvn, wkv_ref[:, 0:width]), k_ref, 1.0)
        v_ref[...] = _bf16(_dot(kvn, wkv_ref[:, width:2 * width]))
    hn = _bf16(_prenorm(h, g_ref[...], mod_ref))
    qscale = B_HDIM ** -0.5 * LOG2E
    for gi, q_ref in enumerate((q1_ref, q2_ref, q3_ref)):
        rope_heads(_dot(hn, win_ref[:, gi * width:(gi + 1) * width]), q_ref, qscale)
    gate_ref[...] = _silu(_dot(hn, win_ref[:, 3 * width:4 * width]))


def _proj_b(h2, mod, g, kvg, w_kv, w_in, cos, sin, seq, with_kv, tm=256):
    n, d = h2.shape
    tps = seq // tm
    row = lambda i: (i, 0)
    const = lambda i: (0, 0)
    in_specs = [pl.BlockSpec((tm, d), row), pl.BlockSpec((None, 3, d), lambda i: (i // tps, 0, 0)),
                pl.BlockSpec((1, d), const)]
    args = [h2, mod, g]
    if with_kv:
        in_specs += [pl.BlockSpec((1, d), const), pl.BlockSpec(w_kv.shape, const)]
        args += [kvg, w_kv]
    in_specs += [pl.BlockSpec(w_in.shape, const), pl.BlockSpec((tm, LANES), row), pl.BlockSpec((tm, LANES), row)]
    args += [w_in, cos, sin]
    n_bf = 5 if with_kv else 3
    return pl.pallas_call(
        functools.partial(_proj_b_kernel, with_kv=with_kv),
        grid=(n // tm,),
        in_specs=in_specs,
        out_specs=[pl.BlockSpec((tm, B_WIDTH), row)] * (n_bf + 1),
        out_shape=[jax.ShapeDtypeStruct((n, B_WIDTH), jnp.bfloat16)] * n_bf
        + [jax.ShapeDtypeStruct((n, B_WIDTH), jnp.float32)],
        compiler_params=pltpu.CompilerParams(vmem_limit_bytes=VMEM_LIMIT),
        name="proj_b",
    )(*args)


def _dilated_kernel(q_ref, kp_ref, kc_ref, vp_ref, vc_ref, o_ref, lse_ref):
    j = pl.program_id(2)
    w = q_ref.shape[0]
    qi = lax.broadcasted_iota(jnp.int32, (w, w), 0)
    ki = lax.broadcasted_iota(jnp.int32, (w, w), 1)
    bias_prev = jnp.where(jnp.logical_and(ki >= qi, j > 0), 0.0, NEG)
    bias_cur = jnp.where(ki <= qi, 0.0, NEG)
    for hd in range(B_HEADS):
        sl = slice(hd * B_HDIM, (hd + 1) * B_HDIM)
        q = q_ref[:, sl]
        s_prev = _dot_nt(q, kp_ref[:, sl]) + bias_prev
        s_cur = _dot_nt(q, kc_ref[:, sl]) + bias_cur
        m = jnp.maximum(jnp.max(s_prev, axis=-1, keepdims=True), jnp.max(s_cur, axis=-1, keepdims=True))
        p_prev = jnp.exp2(s_prev - m)
        p_cur = jnp.exp2(s_cur - m)
        l = jnp.sum(p_prev, axis=-1, keepdims=True) + jnp.sum(p_cur, axis=-1, keepdims=True)
        o = _dot(_bf16(p_prev), vp_ref[:, sl]) + _dot(_bf16(p_cur), vc_ref[:, sl])
        o_ref[:, sl] = o / l
        lse_ref[:, sl] = jnp.broadcast_to(m + jnp.log2(l), (w, B_HDIM))


def _dilated(q, k, v, bsz, seq, window, dil):
    w = window // dil
    n = seq // dil
    nb = n // w
    view = lambda t: t.reshape(bsz, n, dil * B_WIDTH)
    cur = lambda b, r, j: (b, j, r)
    prev = lambda b, r, j: (b, jnp.maximum(j - 1, 0), r)
    blk = lambda im: pl.BlockSpec((None, w, B_WIDTH), im)
    shape = jax.ShapeDtypeStruct((bsz, n, dil * B_WIDTH), jnp.float32)
    o, lse = pl.pallas_call(
        _dilated_kernel,
        grid=(bsz, dil, nb),
        in_specs=[blk(cur), blk(prev), blk(cur), blk(prev), blk(cur)],
        out_specs=[blk(cur), blk(cur)],
        out_shape=[shape, shape],
        name=f"dilated_{dil}",
    )(view(q), view(k), view(k), view(v), view(v))
    return o.reshape(bsz * seq, B_WIDTH), lse.reshape(bsz * seq, B_WIDTH)


def _merge_kernel(o1, o2, o3, l1, l2, l3, gate_ref, h_ref, mod_ref, wout_ref, fin_ref, out_ref, *, final):
    lses = (l1[...], l2[...], l3[...])
    m = jnp.maximum(jnp.maximum(lses[0], lses[1]), lses[2])
    e = [jnp.exp2(t - m) for t in lses]
    o = (e[0] * o1[...] + e[1] * o2[...] + e[2] * o3[...]) / (e[0] + e[1] + e[2])
    y = _dot(_bf16(o * gate_ref[...]), wout_ref[...])
    h = h_ref[...] + mod_ref[2:3, :] * y
    if final:
        h = h * lax.rsqrt(jnp.mean(h * h, axis=-1, keepdims=True) + EPS) * fin_ref[...]
    out_ref[...] = h


def _merge(os_, lses, gate, h2, mod, w_out, fin, seq, final, tm=256):
    n, d = h2.shape
    tps = seq // tm
    row = lambda i: (i, 0)
    rows = pl.BlockSpec((tm, d), row)
    return pl.pallas_call(
        functools.partial(_merge_kernel, final=final),
        grid=(n // tm,),
        in_specs=[rows] * 8 + [pl.BlockSpec((None, 3, d), lambda i: (i // tps, 0, 0)),
                               pl.BlockSpec(w_out.shape, lambda i: (0, 0)),
                               pl.BlockSpec((1, d), lambda i: (0, 0))],
        out_specs=rows,
        out_shape=jax.ShapeDtypeStruct((n, d), jnp.float32),
        compiler_params=pltpu.CompilerParams(vmem_limit_bytes=VMEM_LIMIT),
        name="merge_out",
    )(*os_, *lses, gate, h2, mod, w_out, fin)


def _final_norm_kernel(h_ref, g_ref, out_ref):
    h = h_ref[...]
    out_ref[...] = h * lax.rsqrt(jnp.mean(h * h, axis=-1, keepdims=True) + EPS) * g_ref[...]


def kernel(x, c, positions, a_norm, a_ada_w, a_ada_b, a_w_in, a_kv_norm, a_w_uv, a_w_out, kv_norm, w_kv, b_norm, b_ada_w, b_ada_b, b_w_in, b_w_out, final_norm):
    bsz, seq, d = x.shape
    n = bsz * seq
    n_a, n_b = a_norm.shape[0], b_norm.shape[0]
    topk = min(TOPK_MAX, seq // 4)
    assert topk % Q_BLOCK == 0 and seq % KEY_CHUNK == 0 and d == B_WIDTH
    assert all(seq % window == 0 for window, _ in B_GROUPS)

    cos, sin = _rope_tables(positions, (A_ROPE, IDX_DIM, B_HDIM))
    h = x.reshape(n, d)
    for li in range(n_a):
        mod = _ada(c, a_ada_w[li], a_ada_b[li])
        qcat, keys, vt, kdup, qidx, aux, gate = _proj_a(
            h, mod, a_norm[li].reshape(1, d), a_kv_norm[li].reshape(1, A_LATENT), _a_weight(a_w_in[li]),
            cos[0:2], sin[0:2], bsz, seq)
        h = _attn_a(qcat, qidx, aux, keys, vt, kdup, gate, h, mod, _bf16(a_w_uv[li]), _bf16(a_w_out[li]), topk)
    k_sh = v_sh = None
    for li in range(n_b):
        mod = _ada(c, b_ada_w[li], b_ada_b[li])
        outs = _proj_b(h, mod, b_norm[li].reshape(1, d), kv_norm.reshape(1, d), _bf16(w_kv), _bf16(b_w_in[li]),
                       cos[2], sin[2], seq, with_kv=(li == 0))
        if li == 0:
            k_sh, v_sh = outs[0], outs[1]
            outs = outs[2:]
        q_groups, gate = outs[:3], outs[3]
        res = [_dilated(qg, k_sh, v_sh, bsz, seq, window, dil) for qg, (window, dil) in zip(q_groups, B_GROUPS)]
        h = _merge([r[0] for r in res], [r[1] for r in res], gate, h, mod, _bf16(b_w_out[li]),
                   final_norm.reshape(1, d), seq, final=(li == n_b - 1))
    if n_b == 0:
        h = pl.pallas_call(
            _final_norm_kernel, grid=(n // 256,),
            in_specs=[pl.BlockSpec((256, d), lambda i: (i, 0)), pl.BlockSpec((1, d), lambda i: (0, 0))],
            out_specs=pl.BlockSpec((256, d), lambda i: (i, 0)),
            out_shape=jax.ShapeDtypeStruct((n, d), jnp.float32), name="final_norm",
        )(h, final_norm.reshape(1, d))
    return h.reshape(bsz, seq, d)
```

```python
import functools
import math

import jax
import jax.numpy as jnp
from jax import lax
from jax.experimental import pallas as pl
from jax.experimental.pallas import tpu as pltpu

ROPE_THETA = 10000.0
EPS = 1e-6
LOG2E = 1.4426950408889634

A_HEADS = 16
A_LATENT = 128
A_ROPE = 32
A_VDIM = 128
A_WIDTH = A_HEADS * A_VDIM
IDX_HEADS = 8
IDX_DIM = 64
TOPK_MAX = 256
Q_BLOCK = 128
KEY_CHUNK = 256
A_QPAD = 256
HEADS_PER_SLAB = 2

B_HEADS = 8
B_HDIM = 128
B_GROUPS = ((128, 1), (512, 4), (2048, 16))
B_WIDTH = B_HEADS * B_HDIM

LANES = 128
NEG = -1e30
INT_MIN = -(2 ** 31)
HALF = 2 ** 15
VMEM_LIMIT = 56 * 1024 * 1024

AUX_W_OFF = A_ROPE


def _bf16(t):
    return t.astype(jnp.bfloat16)


def _dot(a, b):
    return jnp.dot(a, b, preferred_element_type=jnp.float32)


def _dot_nt(a, b):
    return lax.dot_general(a, b, (((1,), (1,)), ((), ())), preferred_element_type=jnp.float32)


def _silu(t):
    return t * (1.0 / (1.0 + jnp.exp(-t)))


def _lane_iota(shape):
    return lax.broadcasted_iota(jnp.int32, shape, len(shape) - 1)


def _ada_kernel(c_ref, w_ref, b_ref, out_ref):
    out_ref[...] = _dot(_bf16(_silu(c_ref[...])), _bf16(w_ref[...])) + b_ref[...]


def _ada(c, w, b):
    bsz, d = c.shape
    out = pl.pallas_call(
        _ada_kernel,
        grid=(3,),
        in_specs=[pl.BlockSpec((bsz, d), lambda j: (0, 0)),
                  pl.BlockSpec((d, d), lambda j: (0, j)),
                  pl.BlockSpec((1, d), lambda j: (0, j))],
        out_specs=pl.BlockSpec((bsz, d), lambda j: (0, j)),
        out_shape=jax.ShapeDtypeStruct((bsz, 3 * d), jnp.float32),
        name="ada_mod",
    )(c, w, b.reshape(1, 3 * d))
    return out.reshape(bsz, 3, d)


def _rope_table_kernel(pos_ref, inv_ref, sign_ref, cos_ref, sin_ref):
    ang = pos_ref[...] * inv_ref[...]
    cos_ref[...] = jnp.cos(ang)
    sin_ref[...] = jnp.sin(ang) * sign_ref[...]


def _rope_lane_consts(dim):
    half = dim // 2
    inv = ROPE_THETA ** (-jnp.arange(0, dim, 2, dtype=jnp.float32) / dim)
    lane = jnp.arange(LANES) % dim
    return inv[lane % half], jnp.where(lane < half, -1.0, 1.0).astype(jnp.float32)


def _rope_tables(positions, dims, tm=512):
    n = positions.size
    posf = jnp.broadcast_to(positions.reshape(n, 1).astype(jnp.float32), (n, LANES))
    consts = [_rope_lane_consts(d) for d in dims]
    inv = jnp.stack([cst[0] for cst in consts]).reshape(len(dims), 1, LANES)
    sign = jnp.stack([cst[1] for cst in consts]).reshape(len(dims), 1, LANES)
    shape = jax.ShapeDtypeStruct((len(dims), n, LANES), jnp.float32)
    return pl.pallas_call(
        _rope_table_kernel,
        grid=(len(dims), n // tm),
        in_specs=[pl.BlockSpec((tm, LANES), lambda t, i: (i, 0)),
                  pl.BlockSpec((None, 1, LANES), lambda t, i: (t, 0, 0)),
                  pl.BlockSpec((None, 1, LANES), lambda t, i: (t, 0, 0))],
        out_specs=[pl.BlockSpec((None, tm, LANES), lambda t, i: (t, i, 0))] * 2,
        out_shape=[shape, shape],
        name="rope_tables",
    )(posf, inv, sign)


def _rope_lanes(xv, cos, sin_signed, dim):
    half = dim // 2
    if dim == LANES:
        partner = pltpu.roll(xv, half, axis=1)
    else:
        first = (_lane_iota(xv.shape) % dim) < half
        partner = jnp.where(first, pltpu.roll(xv, LANES - half, axis=1), pltpu.roll(xv, half, axis=1))
    return xv * cos + partner * sin_signed


def _prenorm(h, g, mod_ref):
    y = h * lax.rsqrt(jnp.mean(h * h, axis=-1, keepdims=True) + EPS) * g
    return y * (1.0 + mod_ref[1:2, :]) + mod_ref[0:1, :]


_A_COLS = {}
_o = 0
for _name, _w in (("q_lat", A_HEADS * A_LATENT), ("q_rope", A_HEADS * A_ROPE), ("c_kv", A_LATENT),
                  ("aux", LANES), ("kdup", LANES), ("gate", A_WIDTH), ("q_idx", IDX_HEADS * IDX_DIM)):
    _A_COLS[_name] = (_o, _o + _w)
    _o += _w
A_COLS_TOTAL = _o


def _a_weight(w_in):
    sizes = (A_HEADS * A_LATENT, A_HEADS * A_ROPE, A_LATENT, A_ROPE, A_WIDTH, IDX_HEADS * IDX_DIM, IDX_DIM, IDX_HEADS)
    parts, o = [], 0
    for s in sizes:
        parts.append(w_in[:, o:o + s])
        o += s
    q_lat, q_rope, c_kv, k_rope, gate, q_idx, k_idx, w_idx = parts
    d = w_in.shape[0]
    aux = jnp.concatenate([k_rope, w_idx, jnp.zeros((d, LANES - A_ROPE - IDX_HEADS), w_in.dtype)], axis=1)
    kdup = jnp.concatenate([k_idx, k_idx], axis=1)
    return _bf16(jnp.concatenate([q_lat, q_rope, c_kv, aux, kdup, gate, q_idx], axis=1))


def _proj_a_kernel(x_ref, mod_ref, g_ref, kvg_ref, w_ref, cos_ref, sin_ref,
                   qcat_ref, keys_ref, vt_ref, kdup_ref, qidx_ref, aux_ref, gate_ref):
    hn = _bf16(_prenorm(x_ref[...], g_ref[...], mod_ref))
    cos32, sin32 = cos_ref[0], sin_ref[0]
    cos64, sin64 = cos_ref[1], sin_ref[1]
    lane = _lane_iota(cos32.shape)
    qscale = (A_LATENT + A_ROPE) ** -0.5 * LOG2E

    def proj(name):
        lo, hi = _A_COLS[name]
        return _dot(hn, w_ref[:, lo:hi])

    q_lat = proj("q_lat") * qscale
    q_rope = proj("q_rope")
    per_slab = LANES // A_ROPE
    for h in range(A_HEADS):
        qcat_ref[h, :, 0:A_LATENT] = _bf16(q_lat[:, h * A_LATENT:(h + 1) * A_LATENT])
    for j in range(A_HEADS // per_slab):
        roped = _rope_lanes(q_rope[:, j * LANES:(j + 1) * LANES], cos32, sin32, A_ROPE) * qscale
        for t in range(per_slab):
            piece = roped if t == 0 else pltpu.roll(roped, LANES - A_ROPE * t, axis=1)
            qcat_ref[j * per_slab + t, :, A_LATENT:A_QPAD] = _bf16(jnp.where(lane < A_ROPE, piece, 0.0))

    c_kv = proj("c_kv")
    c_kv = c_kv * lax.rsqrt(jnp.mean(c_kv * c_kv, axis=-1, keepdims=True) + EPS) * kvg_ref[...]
    keys_ref[:, 0:A_LATENT] = _bf16(c_kv)
    vt_ref[...] = _bf16(c_kv.T)
    aux = proj("aux")
    k_rope = _rope_lanes(aux, cos32, sin32, A_ROPE)
    keys_ref[:, A_LATENT:A_QPAD] = _bf16(jnp.where(lane < A_ROPE, k_rope, 0.0))
    aux_ref[...] = aux * IDX_HEADS ** -0.5

    kdup_ref[...] = _bf16(_rope_lanes(proj("kdup"), cos64, sin64, IDX_DIM))
    q_idx = proj("q_idx")
    per_slab_i = LANES // IDX_DIM
    for j in range(IDX_HEADS // per_slab_i):
        roped = _rope_lanes(q_idx[:, j * LANES:(j + 1) * LANES], cos64, sin64, IDX_DIM)
        for t in range(per_slab_i):
            keep = (lane >= t * IDX_DIM) & (lane < (t + 1) * IDX_DIM)
            qidx_ref[j * per_slab_i + t] = _bf16(jnp.where(keep, roped, 0.0))

    gate_ref[...] = _bf16(_silu(proj("gate")))


def _proj_a(x2, mod, g, kvg, w, cos, sin, bsz, seq):
    n, d = x2.shape
    tm = KEY_CHUNK
    tps = seq // tm
    row = lambda i: (i, 0)
    bhs = lambda i: (i // tps, 0, i % tps, 0)
    return pl.pallas_call(
        _proj_a_kernel,
        grid=(n // tm,),
        in_specs=[pl.BlockSpec((tm, d), row),
                  pl.BlockSpec((None, 3, d), lambda i: (i // tps, 0, 0)),
                  pl.BlockSpec((1, d), lambda i: (0, 0)),
                  pl.BlockSpec((1, A_LATENT), lambda i: (0, 0)),
                  pl.BlockSpec((d, A_COLS_TOTAL), lambda i: (0, 0)),
                  pl.BlockSpec((2, tm, LANES), lambda i: (0, i, 0)),
                  pl.BlockSpec((2, tm, LANES), lambda i: (0, i, 0))],
        out_specs=[pl.BlockSpec((None, A_HEADS, tm, A_QPAD), bhs),
                   pl.BlockSpec((tm, A_QPAD), row),
                   pl.BlockSpec((None, None, A_LATENT, tm), lambda i: (i // tps, i % tps, 0, 0)),
                   pl.BlockSpec((tm, LANES), row),
                   pl.BlockSpec((None, IDX_HEADS, tm, LANES), bhs),
                   pl.BlockSpec((tm, LANES), row),
                   pl.BlockSpec((tm, A_WIDTH), row)],
        out_shape=[jax.ShapeDtypeStruct((bsz, A_HEADS, seq, A_QPAD), jnp.bfloat16),
                   jax.ShapeDtypeStruct((n, A_QPAD), jnp.bfloat16),
                   jax.ShapeDtypeStruct((bsz, seq // tm, A_LATENT, tm), jnp.bfloat16),
                   jax.ShapeDtypeStruct((n, LANES), jnp.bfloat16),
                   jax.ShapeDtypeStruct((bsz, IDX_HEADS, seq, LANES), jnp.bfloat16),
                   jax.ShapeDtypeStruct((n, LANES), jnp.float32),
                   jax.ShapeDtypeStruct((n, A_WIDTH), jnp.bfloat16)],
        compiler_params=pltpu.CompilerParams(vmem_limit_bytes=VMEM_LIMIT),
        name="proj_a",
    )(x2, mod, g, kvg, w, cos, sin)


def _attn_a_kernel(qcat_ref, qidx_ref, aux_ref, keys_ref, vt_ref, kdup_ref, gate_ref, x_ref, mod_ref, wuv_ref,
                   wout_ref, out_ref, sc_ref, hi_ref, lo_ref, lo2_ref, bias_ref, sa_ref, sb_ref, mxa_ref, mxb_ref, m_ref, l_ref, acc_ref, og_ref, *, topk):
    i = pl.program_id(1)
    n_chunks = (i + 2) // 2
    seq = keys_ref.shape[0] * KEY_CHUNK

    w_t = aux_ref[...].T
    qi_all = qidx_ref[...].reshape(IDX_HEADS * Q_BLOCK, LANES)
    qpos = i * Q_BLOCK + lax.broadcasted_iota(jnp.int32, (KEY_CHUNK, Q_BLOCK), 1)
    krow = lax.broadcasted_iota(jnp.int32, (KEY_CHUNK, Q_BLOCK), 0)

    n_pairs = (n_chunks + 1) // 2

    def for_chunk_pairs(fn, init):
        def body(t, carry):
            return fn(2 * t + 1, fn(2 * t, carry))
        return lax.fori_loop(0, n_pairs, body, init)

    def score_chunk(c, carry):
        rel = _dot_nt(kdup_ref[c], qi_all)
        acc = jnp.zeros((KEY_CHUNK, Q_BLOCK), jnp.float32)
        for h in range(IDX_HEADS):
            acc = acc + w_t[AUX_W_OFF + h:AUX_W_OFF + h + 1, :] * jnp.maximum(rel[:, h * Q_BLOCK:(h + 1) * Q_BLOCK], 0.0)
        bits = lax.bitcast_convert_type(acc, jnp.int32)
        key = bits ^ ((bits >> 31) & 0x7FFFFFFF)
        key = jnp.where(c * KEY_CHUNK + krow <= qpos, key, INT_MIN)
        sc_ref[c] = key
        hi_ref[c] = (key >> 16).astype(jnp.int16)
        lo_ref[c] = ((key & 0xFFFF) - HALF).astype(jnp.int16)
        return carry

    for_chunk_pairs(score_chunk, 0)

    def count(pred):
        def body(c, acc):
            hit = jnp.where(pred(sc_ref[c], c), 1, 0)
            return acc + jnp.sum(hit.reshape(KEY_CHUNK // 32, 32, Q_BLOCK), axis=0)
        acc = for_chunk_pairs(body, jnp.zeros((32, Q_BLOCK), jnp.int32))
        return jnp.sum(acc, axis=0, keepdims=True)

    def count16(ref16, pred):
        def body(c, acc):
            hit = jnp.where(pred(ref16[c]), jnp.int16(1), jnp.int16(0))
            words = pltpu.bitcast(hit, jnp.int32)
            return acc + jnp.sum(words.reshape(KEY_CHUNK // 64, 32, Q_BLOCK), axis=0)
        acc = for_chunk_pairs(body, jnp.zeros((32, Q_BLOCK), jnp.int32))
        acc = jnp.sum(acc, axis=0, keepdims=True)
        return (acc & 0xFFFF) + (acc >> 16)

    def bisect16(ref16, target, n_bits):
        def bit(b, t):
            cand = t + lax.shift_left(jnp.int32(1), 15 - b)
            cand16 = cand.astype(jnp.int16)
            return jnp.where(count16(ref16, lambda v: v >= cand16) >= target, cand, t)
        return lax.fori_loop(0, n_bits, bit, jnp.full((1, Q_BLOCK), -HALF, jnp.int32))

    all_selected = (i + 1) * Q_BLOCK <= topk
    n_bits = jnp.where(all_selected, 0, 16)
    t_hi = bisect16(hi_ref, topk, n_bits)
    t_hi16 = t_hi.astype(jnp.int16)
    above = count16(hi_ref, lambda v: v > t_hi16)

    def low_chunk(c, carry):
        lo2_ref[c] = jnp.where(hi_ref[c] == t_hi16, lo_ref[c], jnp.int16(-HALF))
        return carry

    for_chunk_pairs(low_chunk, 0)
    t_lo = bisect16(lo2_ref, topk - above, n_bits)
    thr = lax.shift_left(t_hi, 16) + (t_lo + HALF)

    need = topk - count(lambda k, c: k > thr)
    n_eq = count(lambda k, c: k == thr)
    excess = jnp.max(jnp.where(n_eq > need, 1, 0)) > 0
    idx_bits = (seq - 1).bit_length()

    def tie_bound():
        def bit(b, bound):
            cand = bound + lax.shift_left(jnp.int32(1), idx_bits - 1 - b)
            below = count(lambda k, c: jnp.where(k == thr, c * KEY_CHUNK + krow, seq) < cand)
            return jnp.where(below < need, cand, bound)
        return lax.fori_loop(0, idx_bits, bit, jnp.zeros((1, Q_BLOCK), jnp.int32))

    tie_idx = lax.cond(jnp.logical_and(excess, jnp.logical_not(all_selected)), tie_bound,
                       lambda: jnp.where(all_selected, -1, seq) + jnp.zeros((1, Q_BLOCK), jnp.int32))

    def bias_chunk(c, carry):
        k = sc_ref[c]
        tie_ok = jnp.where(c * KEY_CHUNK + krow <= tie_idx, 0.0, NEG)
        bias_ref[c] = jnp.where(k > thr, 0.0, jnp.where(k == thr, tie_ok, NEG))
        return carry

    for_chunk_pairs(bias_chunk, 0)

    m_ref[...] = jnp.full(m_ref.shape, NEG, jnp.float32)
    l_ref[...] = jnp.zeros(l_ref.shape, jnp.float32)
    acc_ref[...] = jnp.zeros(acc_ref.shape, jnp.float32)

    slabs = A_HEADS // HEADS_PER_SLAB
    last_chunk = 2 * n_pairs - 1

    def scores(c, s_buf, mx_buf):
        c = jnp.minimum(c, last_chunk)
        kc = keys_ref[c]
        bias2 = jnp.concatenate([bias_ref[c]] * HEADS_PER_SLAB, axis=1)
        for pr in range(slabs):
            q_slab = qcat_ref[pr * HEADS_PER_SLAB:(pr + 1) * HEADS_PER_SLAB].reshape(HEADS_PER_SLAB * Q_BLOCK, A_QPAD)
            s = _dot_nt(kc, q_slab) + bias2
            s_buf[pr] = s
            mx_buf[pr] = jnp.max(s, axis=0, keepdims=True)

    def accumulate(c, s_buf, mx_buf):
        vt = vt_ref[c]
        for pr in range(slabs):
            m_old = m_ref[pr]
            m_new = jnp.maximum(m_old, mx_buf[pr])
            alpha = jnp.exp2(m_old - m_new)
            p = jnp.exp2(s_buf[pr] - m_new)
            l_ref[pr] = alpha * l_ref[pr] + jnp.sum(p, axis=0, keepdims=True)
            m_ref[pr] = m_new
            acc_ref[pr] = acc_ref[pr] * alpha + _dot(vt, _bf16(p))

    scores(0, sa_ref, mxa_ref)

    def attend_pair(t, carry):
        scores(2 * t + 1, sb_ref, mxb_ref)
        accumulate(2 * t, sa_ref, mxa_ref)
        scores(2 * t + 2, sa_ref, mxa_ref)
        accumulate(2 * t + 1, sb_ref, mxb_ref)
        return carry

    lax.fori_loop(0, n_pairs, attend_pair, 0)

    for pr in range(slabs):
        o_t = acc_ref[pr] / l_ref[pr]
        for t in range(HEADS_PER_SLAB):
            h = pr * HEADS_PER_SLAB + t
            o_lat = _bf16(o_t[:, t * Q_BLOCK:(t + 1) * Q_BLOCK].T)
            o = _dot(o_lat, wuv_ref[h]) * gate_ref[:, h * A_VDIM:(h + 1) * A_VDIM]
            og_ref[:, h * A_VDIM:(h + 1) * A_VDIM] = _bf16(o)
    out_ref[...] = x_ref[...] + mod_ref[2:3, :] * _dot(og_ref[...], wout_ref[...])


def _attn_a(qcat, qidx, aux, keys, vt, kdup, gate, x2, mod, w_uv, w_out, topk):
    bsz, _, seq, _ = qcat.shape
    d = x2.shape[1]
    nq = seq // Q_BLOCK
    nkc = seq // KEY_CHUNK
    row = lambda b, i: (b * nq + i, 0)
    slabs = A_HEADS // HEADS_PER_SLAB
    slab_w = HEADS_PER_SLAB * Q_BLOCK
    return pl.pallas_call(
        functools.partial(_attn_a_kernel, topk=topk),
        grid=(bsz, nq),
        in_specs=[pl.BlockSpec((None, A_HEADS, Q_BLOCK, A_QPAD), lambda b, i: (b, 0, i, 0)),
                  pl.BlockSpec((None, IDX_HEADS, Q_BLOCK, LANES), lambda b, i: (b, 0, i, 0)),
                  pl.BlockSpec((Q_BLOCK, LANES), row),
                  pl.BlockSpec((None, nkc, KEY_CHUNK, A_QPAD), lambda b, i: (b, 0, 0, 0)),
                  pl.BlockSpec((None, nkc, A_LATENT, KEY_CHUNK), lambda b, i: (b, 0, 0, 0)),
                  pl.BlockSpec((None, nkc, KEY_CHUNK, LANES), lambda b, i: (b, 0, 0, 0)),
                  pl.BlockSpec((Q_BLOCK, A_WIDTH), row),
                  pl.BlockSpec((Q_BLOCK, d), row),
                  pl.BlockSpec((None, 3, d), lambda b, i: (b, 0, 0)),
                  pl.BlockSpec((A_HEADS, A_LATENT, A_VDIM), lambda b, i: (0, 0, 0)),
                  pl.BlockSpec((A_WIDTH, d), lambda b, i: (0, 0))],
        out_specs=pl.BlockSpec((Q_BLOCK, d), row),
        out_shape=jax.ShapeDtypeStruct(x2.shape, jnp.float32),
        scratch_shapes=[pltpu.VMEM((nkc, KEY_CHUNK, Q_BLOCK), jnp.int32),
                        pltpu.VMEM((nkc, KEY_CHUNK, Q_BLOCK), jnp.int16),
                        pltpu.VMEM((nkc, KEY_CHUNK, Q_BLOCK), jnp.int16),
                        pltpu.VMEM((nkc, KEY_CHUNK, Q_BLOCK), jnp.int16),
                        pltpu.VMEM((nkc, KEY_CHUNK, Q_BLOCK), jnp.float32),
                        pltpu.VMEM((slabs, KEY_CHUNK, slab_w), jnp.float32),
                        pltpu.VMEM((slabs, KEY_CHUNK, slab_w), jnp.float32),
                        pltpu.VMEM((slabs, 1, slab_w), jnp.float32),
                        pltpu.VMEM((slabs, 1, slab_w), jnp.float32),
                        pltpu.VMEM((slabs, 1, slab_w), jnp.float32),
                        pltpu.VMEM((slabs, 1, slab_w), jnp.float32),
                        pltpu.VMEM((slabs, A_LATENT, slab_w), jnp.float32),
                        pltpu.VMEM((Q_BLOCK, A_WIDTH), jnp.bfloat16)],
        compiler_params=pltpu.CompilerParams(vmem_limit_bytes=VMEM_LIMIT),
        name="attn_a",
    )(qcat, qidx, aux, keys.reshape(bsz, nkc, KEY_CHUNK, A_QPAD), vt, kdup.reshape(bsz, nkc, KEY_CHUNK, LANES),
      gate, x2, mod, w_uv, w_out)


def _store_dilated(tmp_ref, out_ref, dil):
    heads, tm, hdim = tmp_ref.shape
    rows = tm // dil
    for r in range(dil):
        for hd in range(heads):
            src = tmp_ref[hd, pl.ds(r, rows, stride=dil), :] if dil > 1 else tmp_ref[hd]
            col = r * heads * hdim + hd * hdim
            out_ref[:, col:col + hdim] = _bf16(src)


def _proj_b_kernel(*refs, with_kv):
    n_g = len(B_GROUPS)
    if with_kv:
        h_ref, mod_ref, g_ref, kvg_ref, wkv_ref, win_ref, cos_ref, sin_ref = refs[:8]
        outs = refs[8:]
        k_refs, v_refs, outs = outs[:n_g], outs[n_g:2 * n_g], outs[2 * n_g:]
    else:
        h_ref, mod_ref, g_ref, win_ref, cos_ref, sin_ref = refs[:6]
        outs = refs[6:]
    q_refs, gate_ref, tmp_ref = outs[:n_g], outs[n_g], outs[n_g + 1]
    h = h_ref[...]
    cos, sin = cos_ref[...], sin_ref[...]
    width = B_WIDTH

    def rope_to_tmp(t, scale):
        for hd in range(B_HEADS):
            tmp_ref[hd] = _rope_lanes(t[:, hd * B_HDIM:(hd + 1) * B_HDIM], cos, sin, B_HDIM) * scale

    if with_kv:
        kvn = _bf16(h * lax.rsqrt(jnp.mean(h * h, axis=-1, keepdims=True) + EPS) * kvg_ref[...])
        rope_to_tmp(_dot(kvn, wkv_ref[:, 0:width]), 1.0)
        for k_ref, (_, dil) in zip(k_refs, B_GROUPS):
            _store_dilated(tmp_ref, k_ref, dil)
        v = _dot(kvn, wkv_ref[:, width:2 * width])
        for hd in range(B_HEADS):
            tmp_ref[hd] = v[:, hd * B_HDIM:(hd + 1) * B_HDIM]
        for v_ref, (_, dil) in zip(v_refs, B_GROUPS):
            _store_dilated(tmp_ref, v_ref, dil)
    hn = _bf16(_prenorm(h, g_ref[...], mod_ref))
    qscale = B_HDIM ** -0.5 * LOG2E
    for gi, (q_ref, (_, dil)) in enumerate(zip(q_refs, B_GROUPS)):
        rope_to_tmp(_dot(hn, win_ref[:, gi * width:(gi + 1) * width]), qscale)
        _store_dilated(tmp_ref, q_ref, dil)
    gate_ref[...] = _bf16(_silu(_dot(hn, win_ref[:, n_g * width:(n_g + 1) * width])))


def _proj_b(h2, mod, g, kvg, w_kv, w_in, cos, sin, bsz, seq, with_kv, tm=256):
    n, d = h2.shape
    tps = seq // tm
    row = lambda i: (i, 0)
    const = lambda i: (0, 0)
    in_specs = [pl.BlockSpec((tm, d), row), pl.BlockSpec((None, 3, d), lambda i: (i // tps, 0, 0)),
                pl.BlockSpec((1, d), const)]
    args = [h2, mod, g]
    if with_kv:
        in_specs += [pl.BlockSpec((1, d), const), pl.BlockSpec(w_kv.shape, const)]
        args += [kvg, w_kv]
    in_specs += [pl.BlockSpec(w_in.shape, const), pl.BlockSpec((tm, LANES), row), pl.BlockSpec((tm, LANES), row)]
    args += [w_in, cos, sin]
    dil_specs = [pl.BlockSpec((None, tm // dil, dil * B_WIDTH), lambda i: (i // tps, i % tps, 0)) for _, dil in B_GROUPS]
    dil_shapes = [jax.ShapeDtypeStruct((bsz, seq // dil, dil * B_WIDTH), jnp.bfloat16) for _, dil in B_GROUPS]
    n_sets = 3 if with_kv else 1
    return pl.pallas_call(
        functools.partial(_proj_b_kernel, with_kv=with_kv),
        grid=(n // tm,),
        in_specs=in_specs,
        out_specs=dil_specs * n_sets + [pl.BlockSpec((tm, B_WIDTH), row)],
        out_shape=dil_shapes * n_sets + [jax.ShapeDtypeStruct((n, B_WIDTH), jnp.bfloat16)],
        scratch_shapes=[pltpu.VMEM((B_HEADS, tm, B_HDIM), jnp.float32)],
        compiler_params=pltpu.CompilerParams(vmem_limit_bytes=VMEM_LIMIT),
        name="proj_b",
    )(*args)


def _dilated_kernel(q_ref, kp_ref, kc_ref, vp_ref, vc_ref, o_ref, lse_ref):
    j = pl.program_id(2)
    w = q_ref.shape[0]
    qi = lax.broadcasted_iota(jnp.int32, (w, w), 0)
    ki = lax.broadcasted_iota(jnp.int32, (w, w), 1)
    bias_prev = jnp.where(jnp.logical_and(ki >= qi, j > 0), 0.0, NEG)
    bias_cur = jnp.where(ki <= qi, 0.0, NEG)
    lses = []
    for hd in range(B_HEADS):
        sl = slice(hd * B_HDIM, (hd + 1) * B_HDIM)
        q = q_ref[:, sl]
        s_prev = _dot_nt(q, kp_ref[:, sl]) + bias_prev
        s_cur = _dot_nt(q, kc_ref[:, sl]) + bias_cur
        m = jnp.maximum(jnp.max(s_prev, axis=-1, keepdims=True), jnp.max(s_cur, axis=-1, keepdims=True))
        p_prev = jnp.exp2(s_prev - m)
        p_cur = jnp.exp2(s_cur - m)
        l = jnp.sum(p_prev, axis=-1, keepdims=True) + jnp.sum(p_cur, axis=-1, keepdims=True)
        o = _dot(_bf16(p_prev), vp_ref[:, sl]) + _dot(_bf16(p_cur), vc_ref[:, sl])
        o_ref[:, sl] = _bf16(o / l)
        lses.append(m + jnp.log2(l))
    lse_ref[...] = jnp.concatenate(lses + [jnp.zeros((w, LANES - B_HEADS), jnp.float32)], axis=1)


def _dilated(q, k, v, window, dil):
    bsz, n, _ = q.shape
    w = window // dil
    nb = n // w
    cur = lambda b, r, j: (b, j, r)
    prev = lambda b, r, j: (b, jnp.maximum(j - 1, 0), r)
    blk = lambda im: pl.BlockSpec((None, w, B_WIDTH), im)
    return pl.pallas_call(
        _dilated_kernel,
        grid=(bsz, dil, nb),
        in_specs=[blk(cur), blk(prev), blk(cur), blk(prev), blk(cur)],
        out_specs=[blk(cur), pl.BlockSpec((None, None, w, LANES), lambda b, r, j: (b, r, j, 0))],
        out_shape=[jax.ShapeDtypeStruct(q.shape, jnp.bfloat16),
                   jax.ShapeDtypeStruct((bsz, dil, n, LANES), jnp.float32)],
        name=f"dilated_{dil}",
    )(q, k, k, v, v)


def _merge_kernel(*refs, final):
    n_g = len(B_GROUPS)
    o_refs, lse_refs = refs[:n_g], refs[n_g:2 * n_g]
    gate_ref, h_ref, mod_ref, wout_ref, fin_ref, out_ref, os_ref, ls_ref, og_ref = refs[2 * n_g:]
    tm = h_ref.shape[0]
    for gi, (_, dil) in enumerate(B_GROUPS):
        rows = tm // dil
        for r in range(dil):
            dst = pl.ds(r, rows, stride=dil) if dil > 1 else slice(None)
            ls_ref[gi, dst, :] = lse_refs[gi][r]
            for hd in range(B_HEADS):
                col = r * B_WIDTH + hd * B_HDIM
                os_ref[gi, hd, dst, :] = o_refs[gi][:, col:col + B_HDIM].astype(jnp.float32)
    lses = [ls_ref[gi] for gi in range(n_g)]
    m = functools.reduce(jnp.maximum, lses)
    e = [jnp.exp2(t - m) for t in lses]
    inv = 1.0 / functools.reduce(lambda a, b: a + b, e)
    for hd in range(B_HEADS):
        sl = slice(hd * B_HDIM, (hd + 1) * B_HDIM)
        o = functools.reduce(lambda a, b: a + b,
                             [(e[gi] * inv)[:, hd:hd + 1] * os_ref[gi, hd] for gi in range(n_g)])
        og_ref[:, sl] = _bf16(o * gate_ref[:, sl])
    h = h_ref[...] + mod_ref[2:3, :] * _dot(og_ref[...], wout_ref[...])
    if final:
        h = h * lax.rsqrt(jnp.mean(h * h, axis=-1, keepdims=True) + EPS) * fin_ref[...]
    out_ref[...] = h


def _merge(os_, lses, gate, h2, mod, w_out, fin, seq, final, tm=256):
    n, d = h2.shape
    tps = seq // tm
    row = lambda i: (i, 0)
    rows = pl.BlockSpec((tm, d), row)
    n_g = len(B_GROUPS)
    o_specs = [pl.BlockSpec((None, tm // dil, dil * B_WIDTH), lambda i: (i // tps, i % tps, 0)) for _, dil in B_GROUPS]
    lse_specs = [pl.BlockSpec((None, dil, tm // dil, LANES), lambda i: (i // tps, 0, i % tps, 0)) for _, dil in B_GROUPS]
    return pl.pallas_call(
        functools.partial(_merge_kernel, final=final),
        grid=(n // tm,),
        in_specs=o_specs + lse_specs + [rows, rows, pl.BlockSpec((None, 3, d), lambda i: (i // tps, 0, 0)),
                                        pl.BlockSpec(w_out.shape, lambda i: (0, 0)),
                                        pl.BlockSpec((1, d), lambda i: (0, 0))],
        out_specs=rows,
        out_shape=jax.ShapeDtypeStruct((n, d), jnp.float32),
        scratch_shapes=[pltpu.VMEM((n_g, B_HEADS, tm, B_HDIM), jnp.float32),
                        pltpu.VMEM((n_g, tm, LANES), jnp.float32),
                        pltpu.VMEM((tm, B_WIDTH), jnp.bfloat16)],
        compiler_params=pltpu.CompilerParams(vmem_limit_bytes=VMEM_LIMIT),
        name="merge_out",
    )(*os_, *lses, gate, h2, mod, w_out, fin)


def _final_norm_kernel(h_ref, g_ref, out_ref):
    h = h_ref[...]
    out_ref[...] = h * lax.rsqrt(jnp.mean(h * h, axis=-1, keepdims=True) + EPS) * g_ref[...]


def kernel(x, c, positions, a_norm, a_ada_w, a_ada_b, a_w_in, a_kv_norm, a_w_uv, a_w_out, kv_norm, w_kv, b_norm, b_ada_w, b_ada_b, b_w_in, b_w_out, final_norm):
    bsz, seq, d = x.shape
    n = bsz * seq
    n_a, n_b = a_norm.shape[0], b_norm.shape[0]
    topk = min(TOPK_MAX, seq // 4)
    assert topk % Q_BLOCK == 0 and seq % KEY_CHUNK == 0 and d == B_WIDTH
    assert all(seq % window == 0 for window, _ in B_GROUPS)

    cos, sin = _rope_tables(positions, (A_ROPE, IDX_DIM, B_HDIM))
    h = x.reshape(n, d)
    for li in range(n_a):
        mod = _ada(c, a_ada_w[li], a_ada_b[li])
        qcat, keys, vt, kdup, qidx, aux, gate = _proj_a(
            h, mod, a_norm[li].reshape(1, d), a_kv_norm[li].reshape(1, A_LATENT), _a_weight(a_w_in[li]),
            cos[0:2], sin[0:2], bsz, seq)
        h = _attn_a(qcat, qidx, aux, keys, vt, kdup, gate, h, mod, _bf16(a_w_uv[li]), _bf16(a_w_out[li]), topk)
    k_sh = v_sh = None
    for li in range(n_b):
        mod = _ada(c, b_ada_w[li], b_ada_b[li])
        outs = _proj_b(h, mod, b_norm[li].reshape(1, d), kv_norm.reshape(1, d), _bf16(w_kv), _bf16(b_w_in[li]),
                       cos[2], sin[2], bsz, seq, with_kv=(li == 0))
        n_g = len(B_GROUPS)
        if li == 0:
            k_sh, v_sh = outs[:n_g], outs[n_g:2 * n_g]
            outs = outs[2 * n_g:]
        q_groups, gate = outs[:n_g], outs[n_g]
        res = [_dilated(qg, kg, vg, window, dil)
               for qg, kg, vg, (window, dil) in zip(q_groups, k_sh, v_sh, B_GROUPS)]
        h = _merge([r[0] for r in res], [r[1] for r in res], gate, h, mod, _bf16(b_w_out[li]),
                   final_norm.reshape(1, d), seq, final=(li == n_b - 1))
    if n_b == 0:
        h = pl.pallas_call(
            _final_norm_kernel, grid=(n // 256,),
            in_specs=[pl.BlockSpec((256, d), lambda i: (i, 0)), pl.BlockSpec((1, d), lambda i: (0, 0))],
            out_specs=pl.BlockSpec((256, d), lambda i: (i, 0)),
            out_shape=jax.ShapeDtypeStruct((n, d), jnp.float32), name="final_norm",
        )(h, final_norm.reshape(1, d))
    return h.reshape(bsz, seq, d)
```

```python
import functools
import math

import jax
import jax.numpy as jnp
from jax import lax
from jax.experimental import pallas as pl
from jax.experimental.pallas import tpu as pltpu

ROPE_THETA = 10000.0
EPS = 1e-6
LOG2E = 1.4426950408889634

A_HEADS = 16
A_LATENT = 128
A_ROPE = 32
A_VDIM = 128
A_WIDTH = A_HEADS * A_VDIM
IDX_HEADS = 8
IDX_DIM = 64
TOPK_MAX = 256
Q_BLOCK = 128
KEY_CHUNK = 256
A_QPAD = 256
HEADS_PER_SLAB = 2
V_ROWS = A_LATENT + 16

B_HEADS = 8
B_HDIM = 128
B_GROUPS = ((128, 1), (512, 4), (2048, 16))
B_WIDTH = B_HEADS * B_HDIM

LANES = 128
NEG = -1e30
INT_MIN = -(2 ** 31)
HALF = 2 ** 15
VMEM_LIMIT = 56 * 1024 * 1024

AUX_W_OFF = A_ROPE


def _bf16(t):
    return t.astype(jnp.bfloat16)


def _dot(a, b):
    return jnp.dot(a, b, preferred_element_type=jnp.float32)


def _dot_nt(a, b):
    return lax.dot_general(a, b, (((1,), (1,)), ((), ())), preferred_element_type=jnp.float32)


def _silu(t):
    return t * (1.0 / (1.0 + jnp.exp(-t)))


def _lane_iota(shape):
    return lax.broadcasted_iota(jnp.int32, shape, len(shape) - 1)


def _ada_kernel(c_ref, w_ref, b_ref, out_ref):
    out_ref[...] = _dot(_bf16(_silu(c_ref[...])), _bf16(w_ref[...])) + b_ref[...]


def _ada(c, w, b):
    bsz, d = c.shape
    out = pl.pallas_call(
        _ada_kernel,
        grid=(3,),
        in_specs=[pl.BlockSpec((bsz, d), lambda j: (0, 0)),
                  pl.BlockSpec((d, d), lambda j: (0, j)),
                  pl.BlockSpec((1, d), lambda j: (0, j))],
        out_specs=pl.BlockSpec((bsz, d), lambda j: (0, j)),
        out_shape=jax.ShapeDtypeStruct((bsz, 3 * d), jnp.float32),
        name="ada_mod",
    )(c, w, b.reshape(1, 3 * d))
    return out.reshape(bsz, 3, d)


def _rope_table_kernel(pos_ref, inv_ref, sel_ref, sign_ref, cos_ref, sin_ref):
    ang = pos_ref[...] * inv_ref[...]
    cos, sin = jnp.cos(ang), jnp.sin(ang)
    for t in range(cos_ref.shape[0]):
        pick = lambda v: jnp.dot(v, sel_ref[t], precision=lax.Precision.HIGHEST, preferred_element_type=jnp.float32)
        cos_ref[t] = pick(cos)
        sin_ref[t] = pick(sin) * sign_ref[t]


def _rope_tables(positions, dims, tm=512):
    n = positions.size
    base = max(dims)
    assert base == LANES and all(base % d == 0 and d & (d - 1) == 0 for d in dims)
    posf = jnp.broadcast_to(positions.reshape(n, 1).astype(jnp.float32), (n, LANES))
    inv = ROPE_THETA ** (-jnp.arange(0, base, 2, dtype=jnp.float32) / base)
    lane = jnp.arange(LANES)
    inv = inv[lane % (base // 2)].reshape(1, LANES)
    src = [((lane % d) % (d // 2)) * (base // d) for d in dims]
    sel = jnp.stack([(lane[:, None] == s[None, :]).astype(jnp.float32) for s in src])
    sign = jnp.stack([jnp.where(lane % d < d // 2, -1.0, 1.0) for d in dims]).astype(jnp.float32)
    sign = sign.reshape(len(dims), 1, LANES)
    shape = jax.ShapeDtypeStruct((len(dims), n, LANES), jnp.float32)
    return pl.pallas_call(
        _rope_table_kernel,
        grid=(n // tm,),
        in_specs=[pl.BlockSpec((tm, LANES), lambda i: (i, 0)),
                  pl.BlockSpec((1, LANES), lambda i: (0, 0)),
                  pl.BlockSpec(sel.shape, lambda i: (0, 0, 0)),
                  pl.BlockSpec(sign.shape, lambda i: (0, 0, 0))],
        out_specs=[pl.BlockSpec((len(dims), tm, LANES), lambda i: (0, i, 0))] * 2,
        out_shape=[shape, shape],
        name="rope_tables",
    )(posf, inv, sel, sign)


def _rope_lanes(xv, cos, sin_signed, dim):
    half = dim // 2
    if dim == LANES:
        partner = pltpu.roll(xv, half, axis=1)
    else:
        first = (_lane_iota(xv.shape) % dim) < half
        partner = jnp.where(first, pltpu.roll(xv, LANES - half, axis=1), pltpu.roll(xv, half, axis=1))
    return xv * cos + partner * sin_signed


def _prenorm(h, g, mod_ref):
    y = h * lax.rsqrt(jnp.mean(h * h, axis=-1, keepdims=True) + EPS) * g
    return y * (1.0 + mod_ref[1:2, :]) + mod_ref[0:1, :]


_A_COLS = {}
_o = 0
for _name, _w in (("q_lat", A_HEADS * A_LATENT), ("q_rope", A_HEADS * A_ROPE), ("c_kv", A_LATENT),
                  ("aux", LANES), ("kdup", LANES), ("gate", A_WIDTH), ("q_idx", IDX_HEADS * IDX_DIM)):
    _A_COLS[_name] = (_o, _o + _w)
    _o += _w
A_COLS_TOTAL = _o


def _a_weight(w_in):
    sizes = (A_HEADS * A_LATENT, A_HEADS * A_ROPE, A_LATENT, A_ROPE, A_WIDTH, IDX_HEADS * IDX_DIM, IDX_DIM, IDX_HEADS)
    parts, o = [], 0
    for s in sizes:
        parts.append(w_in[:, o:o + s])
        o += s
    q_lat, q_rope, c_kv, k_rope, gate, q_idx, k_idx, w_idx = parts
    d = w_in.shape[0]
    aux = jnp.concatenate([k_rope, w_idx, jnp.zeros((d, LANES - A_ROPE - IDX_HEADS), w_in.dtype)], axis=1)
    kdup = jnp.concatenate([k_idx, k_idx], axis=1)
    return _bf16(jnp.concatenate([q_lat, q_rope, c_kv, aux, kdup, gate, q_idx], axis=1))


def _proj_a_kernel(x_ref, mod_ref, g_ref, kvg_ref, w_ref, cos_ref, sin_ref,
                   qcat_ref, keys_ref, vt_ref, kdup_ref, qidx_ref, aux_ref, gate_ref):
    hn = _bf16(_prenorm(x_ref[...], g_ref[...], mod_ref))
    cos32, sin32 = cos_ref[0], sin_ref[0]
    cos64, sin64 = cos_ref[1], sin_ref[1]
    lane = _lane_iota(cos32.shape)
    qscale = (A_LATENT + A_ROPE) ** -0.5 * LOG2E

    def proj(name):
        lo, hi = _A_COLS[name]
        return _dot(hn, w_ref[:, lo:hi])

    q_lat = proj("q_lat") * qscale
    q_rope = proj("q_rope")
    per_slab = LANES // A_ROPE
    for h in range(A_HEADS):
        qcat_ref[h, :, 0:A_LATENT] = _bf16(q_lat[:, h * A_LATENT:(h + 1) * A_LATENT])
    for j in range(A_HEADS // per_slab):
        roped = _rope_lanes(q_rope[:, j * LANES:(j + 1) * LANES], cos32, sin32, A_ROPE) * qscale
        for t in range(per_slab):
            piece = roped if t == 0 else pltpu.roll(roped, LANES - A_ROPE * t, axis=1)
            qcat_ref[j * per_slab + t, :, A_LATENT:A_QPAD] = _bf16(jnp.where(lane < A_ROPE, piece, 0.0))

    c_kv = proj("c_kv")
    c_kv = c_kv * lax.rsqrt(jnp.mean(c_kv * c_kv, axis=-1, keepdims=True) + EPS) * kvg_ref[...]
    keys_ref[:, 0:A_LATENT] = _bf16(c_kv)
    vt_ref[0:A_LATENT, :] = _bf16(c_kv.T)
    vt_ref[A_LATENT:V_ROWS, :] = jnp.ones((V_ROWS - A_LATENT, vt_ref.shape[1]), jnp.bfloat16)
    aux = proj("aux")
    k_rope = _rope_lanes(aux, cos32, sin32, A_ROPE)
    keys_ref[:, A_LATENT:A_QPAD] = _bf16(jnp.where(lane < A_ROPE, k_rope, 0.0))
    aux_ref[...] = aux * IDX_HEADS ** -0.5

    kdup_ref[...] = _bf16(_rope_lanes(proj("kdup"), cos64, sin64, IDX_DIM))
    q_idx = proj("q_idx")
    per_slab_i = LANES // IDX_DIM
    for j in range(IDX_HEADS // per_slab_i):
        roped = _rope_lanes(q_idx[:, j * LANES:(j + 1) * LANES], cos64, sin64, IDX_DIM)
        for t in range(per_slab_i):
            keep = (lane >= t * IDX_DIM) & (lane < (t + 1) * IDX_DIM)
            qidx_ref[j * per_slab_i + t] = _bf16(jnp.where(keep, roped, 0.0))

    gate_ref[...] = _bf16(_silu(proj("gate")))


def _proj_a(x2, mod, g, kvg, w, cos, sin, bsz, seq):
    n, d = x2.shape
    tm = KEY_CHUNK
    tps = seq // tm
    row = lambda i: (i, 0)
    bhs = lambda i: (i // tps, 0, i % tps, 0)
    return pl.pallas_call(
        _proj_a_kernel,
        grid=(n // tm,),
        in_specs=[pl.BlockSpec((tm, d), row),
                  pl.BlockSpec((None, 3, d), lambda i: (i // tps, 0, 0)),
                  pl.BlockSpec((1, d), lambda i: (0, 0)),
                  pl.BlockSpec((1, A_LATENT), lambda i: (0, 0)),
                  pl.BlockSpec((d, A_COLS_TOTAL), lambda i: (0, 0)),
                  pl.BlockSpec((2, tm, LANES), lambda i: (0, i, 0)),
                  pl.BlockSpec((2, tm, LANES), lambda i: (0, i, 0))],
        out_specs=[pl.BlockSpec((None, A_HEADS, tm, A_QPAD), bhs),
                   pl.BlockSpec((tm, A_QPAD), row),
                   pl.BlockSpec((None, None, V_ROWS, tm), lambda i: (i // tps, i % tps, 0, 0)),
                   pl.BlockSpec((tm, LANES), row),
                   pl.BlockSpec((None, IDX_HEADS, tm, LANES), bhs),
                   pl.BlockSpec((tm, LANES), row),
                   pl.BlockSpec((tm, A_WIDTH), row)],
        out_shape=[jax.ShapeDtypeStruct((bsz, A_HEADS, seq, A_QPAD), jnp.bfloat16),
                   jax.ShapeDtypeStruct((n, A_QPAD), jnp.bfloat16),
                   jax.ShapeDtypeStruct((bsz, seq // tm, V_ROWS, tm), jnp.bfloat16),
                   jax.ShapeDtypeStruct((n, LANES), jnp.bfloat16),
                   jax.ShapeDtypeStruct((bsz, IDX_HEADS, seq, LANES), jnp.bfloat16),
                   jax.ShapeDtypeStruct((n, LANES), jnp.float32),
                   jax.ShapeDtypeStruct((n, A_WIDTH), jnp.bfloat16)],
        compiler_params=pltpu.CompilerParams(vmem_limit_bytes=VMEM_LIMIT),
        name="proj_a",
    )(x2, mod, g, kvg, w, cos, sin)


def _attn_a_kernel(qcat_ref, qidx_ref, aux_ref, keys_ref, vt_ref, kdup_ref, gate_ref, x_ref, mod_ref, wuv_ref,
                   wout_ref, out_ref, sc_ref, hi_ref, lo_ref, lo2_ref, bias_ref, sa_ref, sb_ref, mxa_ref, mxb_ref, m_ref, acc_ref, og_ref, *, topk):
    i = pl.program_id(1)
    n_chunks = (i + 2) // 2
    seq = keys_ref.shape[0] * KEY_CHUNK

    w_t = aux_ref[...].T
    qi_all = qidx_ref[...].reshape(IDX_HEADS * Q_BLOCK, LANES)
    qpos = i * Q_BLOCK + lax.broadcasted_iota(jnp.int32, (KEY_CHUNK, Q_BLOCK), 1)
    krow = lax.broadcasted_iota(jnp.int32, (KEY_CHUNK, Q_BLOCK), 0)

    n_pairs = (n_chunks + 1) // 2

    def for_chunk_pairs(fn, init):
        def body(t, carry):
            return fn(2 * t + 1, fn(2 * t, carry))
        return lax.fori_loop(0, n_pairs, body, init)

    def score_chunk(c, carry):
        rel = _dot_nt(kdup_ref[c], qi_all)
        acc = jnp.zeros((KEY_CHUNK, Q_BLOCK), jnp.float32)
        for h in range(IDX_HEADS):
            acc = acc + w_t[AUX_W_OFF + h:AUX_W_OFF + h + 1, :] * jnp.maximum(rel[:, h * Q_BLOCK:(h + 1) * Q_BLOCK], 0.0)
        bits = lax.bitcast_convert_type(acc, jnp.int32)
        key = bits ^ ((bits >> 31) & 0x7FFFFFFF)
        key = jnp.where(c * KEY_CHUNK + krow <= qpos, key, INT_MIN)
        sc_ref[c] = key
        hi_ref[c] = (key >> 16).astype(jnp.int16)
        lo_ref[c] = ((key & 0xFFFF) - HALF).astype(jnp.int16)
        return carry

    for_chunk_pairs(score_chunk, 0)

    def count(pred):
        def body(c, acc):
            hit = jnp.where(pred(sc_ref[c], c), 1, 0)
            return acc + jnp.sum(hit.reshape(KEY_CHUNK // 32, 32, Q_BLOCK), axis=0)
        acc = for_chunk_pairs(body, jnp.zeros((32, Q_BLOCK), jnp.int32))
        return jnp.sum(acc, axis=0, keepdims=True)

    def count16(ref16, pred):
        def body(c, acc):
            hit = jnp.where(pred(ref16[c]), jnp.int16(1), jnp.int16(0))
            words = pltpu.bitcast(hit, jnp.int32)
            return acc + jnp.sum(words.reshape(KEY_CHUNK // 64, 32, Q_BLOCK), axis=0)
        acc = for_chunk_pairs(body, jnp.zeros((32, Q_BLOCK), jnp.int32))
        acc = jnp.sum(acc, axis=0, keepdims=True)
        return (acc & 0xFFFF) + (acc >> 16)

    def bisect16(ref16, target, n_bits):
        def bit(b, t):
            cand = t + lax.shift_left(jnp.int32(1), 15 - b)
            cand16 = cand.astype(jnp.int16)
            return jnp.where(count16(ref16, lambda v: v >= cand16) >= target, cand, t)
        return lax.fori_loop(0, n_bits, bit, jnp.full((1, Q_BLOCK), -HALF, jnp.int32))

    all_selected = (i + 1) * Q_BLOCK <= topk
    n_bits = jnp.where(all_selected, 0, 16)
    t_hi = bisect16(hi_ref, topk, n_bits)
    t_hi16 = t_hi.astype(jnp.int16)
    above = count16(hi_ref, lambda v: v > t_hi16)

    def low_chunk(c, carry):
        lo2_ref[c] = jnp.where(hi_ref[c] == t_hi16, lo_ref[c], jnp.int16(-HALF))
        return carry

    for_chunk_pairs(low_chunk, 0)
    t_lo = bisect16(lo2_ref, topk - above, n_bits)
    thr = lax.shift_left(t_hi, 16) + (t_lo + HALF)

    need = topk - count(lambda k, c: k > thr)
    n_eq = count(lambda k, c: k == thr)
    excess = jnp.max(jnp.where(n_eq > need, 1, 0)) > 0
    idx_bits = (seq - 1).bit_length()

    def tie_bound():
        def bit(b, bound):
            cand = bound + lax.shift_left(jnp.int32(1), idx_bits - 1 - b)
            below = count(lambda k, c: jnp.where(k == thr, c * KEY_CHUNK + krow, seq) < cand)
            return jnp.where(below < need, cand, bound)
        return lax.fori_loop(0, idx_bits, bit, jnp.zeros((1, Q_BLOCK), jnp.int32))

    tie_idx = lax.cond(jnp.logical_and(excess, jnp.logical_not(all_selected)), tie_bound,
                       lambda: jnp.where(all_selected, -1, seq) + jnp.zeros((1, Q_BLOCK), jnp.int32))

    def bias_chunk(c, carry):
        k = sc_ref[c]
        tie_ok = jnp.where(c * KEY_CHUNK + krow <= tie_idx, 0.0, NEG)
        bias_ref[c] = jnp.where(k > thr, 0.0, jnp.where(k == thr, tie_ok, NEG))
        return carry

    for_chunk_pairs(bias_chunk, 0)

    m_ref[...] = jnp.full(m_ref.shape, NEG, jnp.float32)
    acc_ref[...] = jnp.zeros(acc_ref.shape, jnp.float32)

    slabs = A_HEADS // HEADS_PER_SLAB
    last_chunk = 2 * n_pairs - 1

    def scores(c, s_buf, mx_buf):
        c = jnp.minimum(c, last_chunk)
        kc = keys_ref[c]
        bias2 = jnp.concatenate([bias_ref[c]] * HEADS_PER_SLAB, axis=1)
        for pr in range(slabs):
            q_slab = qcat_ref[pr * HEADS_PER_SLAB:(pr + 1) * HEADS_PER_SLAB].reshape(HEADS_PER_SLAB * Q_BLOCK, A_QPAD)
            s = _dot_nt(kc, q_slab) + bias2
            s_buf[pr] = s
            mx_buf[pr] = jnp.max(s, axis=0, keepdims=True)

    def accumulate(c, s_buf, mx_buf):
        vt = vt_ref[c]
        for pr in range(slabs):
            m_old = m_ref[pr]
            m_new = jnp.maximum(m_old, mx_buf[pr])
            alpha = jnp.exp2(m_old - m_new)
            p = jnp.exp2(s_buf[pr] - m_new)
            m_ref[pr] = m_new
            acc_ref[pr] = acc_ref[pr] * alpha + _dot(vt, _bf16(p))

    scores(0, sa_ref, mxa_ref)

    def attend_pair(t, carry):
        scores(2 * t + 1, sb_ref, mxb_ref)
        accumulate(2 * t, sa_ref, mxa_ref)
        scores(2 * t + 2, sa_ref, mxa_ref)
        accumulate(2 * t + 1, sb_ref, mxb_ref)
        return carry

    lax.fori_loop(0, n_pairs, attend_pair, 0)

    for pr in range(slabs):
        o_t = acc_ref[pr, 0:A_LATENT, :] / acc_ref[pr, A_LATENT:A_LATENT + 1, :]
        for t in range(HEADS_PER_SLAB):
            h = pr * HEADS_PER_SLAB + t
            o_lat = _bf16(o_t[:, t * Q_BLOCK:(t + 1) * Q_BLOCK].T)
            o = _dot(o_lat, wuv_ref[h]) * gate_ref[:, h * A_VDIM:(h + 1) * A_VDIM]
            og_ref[:, h * A_VDIM:(h + 1) * A_VDIM] = _bf16(o)
    out_ref[...] = x_ref[...] + mod_ref[2:3, :] * _dot(og_ref[...], wout_ref[...])


def _attn_a(qcat, qidx, aux, keys, vt, kdup, gate, x2, mod, w_uv, w_out, topk):
    bsz, _, seq, _ = qcat.shape
    d = x2.shape[1]
    nq = seq // Q_BLOCK
    nkc = seq // KEY_CHUNK
    row = lambda b, i: (b * nq + i, 0)
    slabs = A_HEADS // HEADS_PER_SLAB
    slab_w = HEADS_PER_SLAB * Q_BLOCK
    return pl.pallas_call(
        functools.partial(_attn_a_kernel, topk=topk),
        grid=(bsz, nq),
        in_specs=[pl.BlockSpec((None, A_HEADS, Q_BLOCK, A_QPAD), lambda b, i: (b, 0, i, 0)),
                  pl.BlockSpec((None, IDX_HEADS, Q_BLOCK, LANES), lambda b, i: (b, 0, i, 0)),
                  pl.BlockSpec((Q_BLOCK, LANES), row),
                  pl.BlockSpec((None, nkc, KEY_CHUNK, A_QPAD), lambda b, i: (b, 0, 0, 0)),
                  pl.BlockSpec((None, nkc, V_ROWS, KEY_CHUNK), lambda b, i: (b, 0, 0, 0)),
                  pl.BlockSpec((None, nkc, KEY_CHUNK, LANES), lambda b, i: (b, 0, 0, 0)),
                  pl.BlockSpec((Q_BLOCK, A_WIDTH), row),
                  pl.BlockSpec((Q_BLOCK, d), row),
                  pl.BlockSpec((None, 3, d), lambda b, i: (b, 0, 0)),
                  pl.BlockSpec((A_HEADS, A_LATENT, A_VDIM), lambda b, i: (0, 0, 0)),
                  pl.BlockSpec((A_WIDTH, d), lambda b, i: (0, 0))],
        out_specs=pl.BlockSpec((Q_BLOCK, d), row),
        out_shape=jax.ShapeDtypeStruct(x2.shape, jnp.float32),
        scratch_shapes=[pltpu.VMEM((nkc, KEY_CHUNK, Q_BLOCK), jnp.int32),
                        pltpu.VMEM((nkc, KEY_CHUNK, Q_BLOCK), jnp.int16),
                        pltpu.VMEM((nkc, KEY_CHUNK, Q_BLOCK), jnp.int16),
                        pltpu.VMEM((nkc, KEY_CHUNK, Q_BLOCK), jnp.int16),
                        pltpu.VMEM((nkc, KEY_CHUNK, Q_BLOCK), jnp.float32),
                        pltpu.VMEM((slabs, KEY_CHUNK, slab_w), jnp.float32),
                        pltpu.VMEM((slabs, KEY_CHUNK, slab_w), jnp.float32),
                        pltpu.VMEM((slabs, 1, slab_w), jnp.float32),
                        pltpu.VMEM((slabs, 1, slab_w), jnp.float32),
                        pltpu.VMEM((slabs, 1, slab_w), jnp.float32),
                        pltpu.VMEM((slabs, V_ROWS, slab_w), jnp.float32),
                        pltpu.VMEM((Q_BLOCK, A_WIDTH), jnp.bfloat16)],
        compiler_params=pltpu.CompilerParams(vmem_limit_bytes=VMEM_LIMIT),
        name="attn_a",
    )(qcat, qidx, aux, keys.reshape(bsz, nkc, KEY_CHUNK, A_QPAD), vt, kdup.reshape(bsz, nkc, KEY_CHUNK, LANES),
      gate, x2, mod, w_uv, w_out)


def _store_dilated(tmp_ref, out_ref, dil):
    heads, tm, hdim = tmp_ref.shape
    rows = tm // dil
    for r in range(dil):
        for hd in range(heads):
            src = tmp_ref[hd, pl.ds(r, rows, stride=dil), :] if dil > 1 else tmp_ref[hd]
            col = r * heads * hdim + hd * hdim
            out_ref[:, col:col + hdim] = _bf16(src)


def _proj_b_kernel(*refs, with_kv):
    n_g = len(B_GROUPS)
    if with_kv:
        h_ref, mod_ref, g_ref, kvg_ref, wkv_ref, win_ref, cos_ref, sin_ref = refs[:8]
        outs = refs[8:]
        k_refs, v_refs, outs = outs[:n_g], outs[n_g:2 * n_g], outs[2 * n_g:]
    else:
        h_ref, mod_ref, g_ref, win_ref, cos_ref, sin_ref = refs[:6]
        outs = refs[6:]
    q_refs, gate_ref, tmp_ref = outs[:n_g], outs[n_g], outs[n_g + 1]
    h = h_ref[...]
    cos, sin = cos_ref[...], sin_ref[...]
    width = B_WIDTH

    def rope_to_tmp(t, scale):
        for hd in range(B_HEADS):
            tmp_ref[hd] = _rope_lanes(t[:, hd * B_HDIM:(hd + 1) * B_HDIM], cos, sin, B_HDIM) * scale

    if with_kv:
        kvn = _bf16(h * lax.rsqrt(jnp.mean(h * h, axis=-1, keepdims=True) + EPS) * kvg_ref[...])
        rope_to_tmp(_dot(kvn, wkv_ref[:, 0:width]), 1.0)
        for k_ref, (_, dil) in zip(k_refs, B_GROUPS):
            _store_dilated(tmp_ref, k_ref, dil)
        v = _dot(kvn, wkv_ref[:, width:2 * width])
        for hd in range(B_HEADS):
            tmp_ref[hd] = v[:, hd * B_HDIM:(hd + 1) * B_HDIM]
        for v_ref, (_, dil) in zip(v_refs, B_GROUPS):
            _store_dilated(tmp_ref, v_ref, dil)
    hn = _bf16(_prenorm(h, g_ref[...], mod_ref))
    qscale = B_HDIM ** -0.5 * LOG2E
    for gi, (q_ref, (_, dil)) in enumerate(zip(q_refs, B_GROUPS)):
        rope_to_tmp(_dot(hn, win_ref[:, gi * width:(gi + 1) * width]), qscale)
        _store_dilated(tmp_ref, q_ref, dil)
    gate_ref[...] = _bf16(_silu(_dot(hn, win_ref[:, n_g * width:(n_g + 1) * width])))


def _proj_b(h2, mod, g, kvg, w_kv, w_in, cos, sin, bsz, seq, with_kv, tm=256):
    n, d = h2.shape
    tps = seq // tm
    row = lambda i: (i, 0)
    const = lambda i: (0, 0)
    in_specs = [pl.BlockSpec((tm, d), row), pl.BlockSpec((None, 3, d), lambda i: (i // tps, 0, 0)),
                pl.BlockSpec((1, d), const)]
    args = [h2, mod, g]
    if with_kv:
        in_specs += [pl.BlockSpec((1, d), const), pl.BlockSpec(w_kv.shape, const)]
        args += [kvg, w_kv]
    in_specs += [pl.BlockSpec(w_in.shape, const), pl.BlockSpec((tm, LANES), row), pl.BlockSpec((tm, LANES), row)]
    args += [w_in, cos, sin]
    dil_specs = [pl.BlockSpec((None, tm // dil, dil * B_WIDTH), lambda i: (i // tps, i % tps, 0)) for _, dil in B_GROUPS]
    dil_shapes = [jax.ShapeDtypeStruct((bsz, seq // dil, dil * B_WIDTH), jnp.bfloat16) for _, dil in B_GROUPS]
    n_sets = 3 if with_kv else 1
    return pl.pallas_call(
        functools.partial(_proj_b_kernel, with_kv=with_kv),
        grid=(n // tm,),
        in_specs=in_specs,
        out_specs=dil_specs * n_sets + [pl.BlockSpec((tm, B_WIDTH), row)],
        out_shape=dil_shapes * n_sets + [jax.ShapeDtypeStruct((n, B_WIDTH), jnp.bfloat16)],
        scratch_shapes=[pltpu.VMEM((B_HEADS, tm, B_HDIM), jnp.float32)],
        compiler_params=pltpu.CompilerParams(vmem_limit_bytes=VMEM_LIMIT),
        name="proj_b",
    )(*args)


def _dilated_kernel(q_ref, kp_ref, kc_ref, vp_ref, vc_ref, o_ref, lse_ref):
    j = pl.program_id(2)
    w = q_ref.shape[0]
    qi = lax.broadcasted_iota(jnp.int32, (w, w), 0)
    ki = lax.broadcasted_iota(jnp.int32, (w, w), 1)
    bias = jnp.concatenate([jnp.where(jnp.logical_and(ki >= qi, j > 0), 0.0, NEG),
                            jnp.where(ki <= qi, 0.0, NEG)], axis=1)
    ones = jnp.ones((2 * w, B_HDIM), jnp.bfloat16)
    lses = []
    for hd in range(B_HEADS):
        sl = slice(hd * B_HDIM, (hd + 1) * B_HDIM)
        keys = jnp.concatenate([kp_ref[:, sl], kc_ref[:, sl]], axis=0)
        vals = jnp.concatenate([vp_ref[:, sl], vc_ref[:, sl]], axis=0)
        s = _dot_nt(q_ref[:, sl], keys) + bias
        m = jnp.max(s, axis=-1, keepdims=True)
        p = _bf16(jnp.exp2(s - m))
        ol = _dot(p, jnp.concatenate([vals, ones], axis=1))
        l = ol[:, B_HDIM:]
        o_ref[:, sl] = _bf16(ol[:, :B_HDIM] / l)
        lses.append(m + jnp.log2(l[:, 0:1]))
    lse_ref[...] = jnp.concatenate(lses + [jnp.zeros((w, LANES - B_HEADS), jnp.float32)], axis=1)


def _dilated(q, k, v, window, dil):
    bsz, n, _ = q.shape
    w = window // dil
    nb = n // w
    cur = lambda b, r, j: (b, j, r)
    prev = lambda b, r, j: (b, jnp.maximum(j - 1, 0), r)
    blk = lambda im: pl.BlockSpec((None, w, B_WIDTH), im)
    return pl.pallas_call(
        _dilated_kernel,
        grid=(bsz, dil, nb),
        in_specs=[blk(cur), blk(prev), blk(cur), blk(prev), blk(cur)],
        out_specs=[blk(cur), pl.BlockSpec((None, None, w, LANES), lambda b, r, j: (b, r, j, 0))],
        out_shape=[jax.ShapeDtypeStruct(q.shape, jnp.bfloat16),
                   jax.ShapeDtypeStruct((bsz, dil, n, LANES), jnp.float32)],
        name=f"dilated_{dil}",
    )(q, k, k, v, v)


def _merge_kernel(*refs, final):
    n_g = len(B_GROUPS)
    o_refs, lse_refs = refs[:n_g], refs[n_g:2 * n_g]
    gate_ref, h_ref, mod_ref, wout_ref, fin_ref, out_ref, os_ref, ls_ref, og_ref = refs[2 * n_g:]
    tm = h_ref.shape[0]
    for gi, (_, dil) in enumerate(B_GROUPS):
        rows = tm // dil
        for r in range(dil):
            dst = pl.ds(r, rows, stride=dil) if dil > 1 else slice(None)
            ls_ref[gi, dst, :] = lse_refs[gi][r]
            for hd in range(B_HEADS):
                col = r * B_WIDTH + hd * B_HDIM
                os_ref[gi, hd, dst, :] = o_refs[gi][:, col:col + B_HDIM].astype(jnp.float32)
    lses = [ls_ref[gi] for gi in range(n_g)]
    m = functools.reduce(jnp.maximum, lses)
    e = [jnp.exp2(t - m) for t in lses]
    inv = 1.0 / functools.reduce(lambda a, b: a + b, e)
    for hd in range(B_HEADS):
        sl = slice(hd * B_HDIM, (hd + 1) * B_HDIM)
        o = functools.reduce(lambda a, b: a + b,
                             [(e[gi] * inv)[:, hd:hd + 1] * os_ref[gi, hd] for gi in range(n_g)])
        og_ref[:, sl] = _bf16(o * gate_ref[:, sl])
    h = h_ref[...] + mod_ref[2:3, :] * _dot(og_ref[...], wout_ref[...])
    if final:
        h = h * lax.rsqrt(jnp.mean(h * h, axis=-1, keepdims=True) + EPS) * fin_ref[...]
    out_ref[...] = h


def _merge(os_, lses, gate, h2, mod, w_out, fin, seq, final, tm=256):
    n, d = h2.shape
    tps = seq // tm
    row = lambda i: (i, 0)
    rows = pl.BlockSpec((tm, d), row)
    n_g = len(B_GROUPS)
    o_specs = [pl.BlockSpec((None, tm // dil, dil * B_WIDTH), lambda i: (i // tps, i % tps, 0)) for _, dil in B_GROUPS]
    lse_specs = [pl.BlockSpec((None, dil, tm // dil, LANES), lambda i: (i // tps, 0, i % tps, 0)) for _, dil in B_GROUPS]
    return pl.pallas_call(
        functools.partial(_merge_kernel, final=final),
        grid=(n // tm,),
        in_specs=o_specs + lse_specs + [rows, rows, pl.BlockSpec((None, 3, d), lambda i: (i // tps, 0, 0)),
                                        pl.BlockSpec(w_out.shape, lambda i: (0, 0)),
                                        pl.BlockSpec((1, d), lambda i: (0, 0))],
        out_specs=rows,
        out_shape=jax.ShapeDtypeStruct((n, d), jnp.float32),
        scratch_shapes=[pltpu.VMEM((n_g, B_HEADS, tm, B_HDIM), jnp.float32),
                        pltpu.VMEM((n_g, tm, LANES), jnp.float32),
                        pltpu.VMEM((tm, B_WIDTH), jnp.bfloat16)],
        compiler_params=pltpu.CompilerParams(vmem_limit_bytes=VMEM_LIMIT),
        name="merge_out",
    )(*os_, *lses, gate, h2, mod, w_out, fin)


def _final_norm_kernel(h_ref, g_ref, out_ref):
    h = h_ref[...]
    out_ref[...] = h * lax.rsqrt(jnp.mean(h * h, axis=-1, keepdims=True) + EPS) * g_ref[...]


def kernel(x, c, positions, a_norm, a_ada_w, a_ada_b, a_w_in, a_kv_norm, a_w_uv, a_w_out, kv_norm, w_kv, b_norm, b_ada_w, b_ada_b, b_w_in, b_w_out, final_norm):
    bsz, seq, d = x.shape
    n = bsz * seq
    n_a, n_b = a_norm.shape[0], b_norm.shape[0]
    topk = min(TOPK_MAX, seq // 4)
    assert topk % Q_BLOCK == 0 and seq % KEY_CHUNK == 0 and d == B_WIDTH
    assert all(seq % window == 0 for window, _ in B_GROUPS)

    cos, sin = _rope_tables(positions, (A_ROPE, IDX_DIM, B_HDIM))
    h = x.reshape(n, d)
    for li in range(n_a):
        mod = _ada(c, a_ada_w[li], a_ada_b[li])
        qcat, keys, vt, kdup, qidx, aux, gate = _proj_a(
            h, mod, a_norm[li].reshape(1, d), a_kv_norm[li].reshape(1, A_LATENT), _a_weight(a_w_in[li]),
            cos[0:2], sin[0:2], bsz, seq)
        h = _attn_a(qcat, qidx, aux, keys, vt, kdup, gate, h, mod, _bf16(a_w_uv[li]), _bf16(a_w_out[li]), topk)
    k_sh = v_sh = None
    for li in range(n_b):
        mod = _ada(c, b_ada_w[li], b_ada_b[li])
        outs = _proj_b(h, mod, b_norm[li].reshape(1, d), kv_norm.reshape(1, d), _bf16(w_kv), _bf16(b_w_in[li]),
                       cos[2], sin[2], bsz, seq, with_kv=(li == 0))
        n_g = len(B_GROUPS)
        if li == 0:
            k_sh, v_sh = outs[:n_g], outs[n_g:2 * n_g]
            outs = outs[2 * n_g:]
        q_groups, gate = outs[:n_g], outs[n_g]
        res = [_dilated(qg, kg, vg, window, dil)
               for qg, kg, vg, (window, dil) in zip(q_groups, k_sh, v_sh, B_GROUPS)]
        h = _merge([r[0] for r in res], [r[1] for r in res], gate, h, mod, _bf16(b_w_out[li]),
                   final_norm.reshape(1, d), seq, final=(li == n_b - 1))
    if n_b == 0:
        h = pl.pallas_call(
            _final_norm_kernel, grid=(n // 256,),
            in_specs=[pl.BlockSpec((256, d), lambda i: (i, 0)), pl.BlockSpec((1, d), lambda i: (0, 0))],
            out_specs=pl.BlockSpec((256, d), lambda i: (i, 0)),
            out_shape=jax.ShapeDtypeStruct((n, d), jnp.float32), name="final_norm",
        )(h, final_norm.reshape(1, d))
    return h.reshape(bsz, seq, d)
```

```python
import functools
import math

import jax
import jax.numpy as jnp
from jax import lax
from jax.experimental import pallas as pl
from jax.experimental.pallas import tpu as pltpu

ROPE_THETA = 10000.0
EPS = 1e-6
LOG2E = 1.4426950408889634

A_HEADS = 16
A_LATENT = 128
A_ROPE = 32
A_VDIM = 128
A_WIDTH = A_HEADS * A_VDIM
IDX_HEADS = 8
IDX_DIM = 64
TOPK_MAX = 256
Q_BLOCK = 128
KEY_CHUNK = 256
A_QPAD = 256
HEADS_PER_SLAB = 2
V_ROWS = A_LATENT + 16

B_HEADS = 8
B_HDIM = 128
B_GROUPS = ((128, 1), (512, 4), (2048, 16))
B_WIDTH = B_HEADS * B_HDIM

LANES = 128
NEG = -1e30
INT_MIN = -(2 ** 31)
HALF = 2 ** 15
VMEM_LIMIT = 56 * 1024 * 1024

AUX_W_OFF = A_ROPE


def _bf16(t):
    return t.astype(jnp.bfloat16)


def _dot(a, b):
    return jnp.dot(a, b, preferred_element_type=jnp.float32)


def _dot_nt(a, b):
    return lax.dot_general(a, b, (((1,), (1,)), ((), ())), preferred_element_type=jnp.float32)


def _silu(t):
    return t * (1.0 / (1.0 + jnp.exp(-t)))


def _lane_iota(shape):
    return lax.broadcasted_iota(jnp.int32, shape, len(shape) - 1)


def _ada_kernel(c_ref, w_ref, b_ref, out_ref):
    out_ref[...] = _dot(_bf16(_silu(c_ref[...])), _bf16(w_ref[...])) + b_ref[...]


def _ada(c, w, b):
    bsz, d = c.shape
    out = pl.pallas_call(
        _ada_kernel,
        grid=(3,),
        in_specs=[pl.BlockSpec((bsz, d), lambda j: (0, 0)),
                  pl.BlockSpec((d, d), lambda j: (0, j)),
                  pl.BlockSpec((1, d), lambda j: (0, j))],
        out_specs=pl.BlockSpec((bsz, d), lambda j: (0, j)),
        out_shape=jax.ShapeDtypeStruct((bsz, 3 * d), jnp.float32),
        name="ada_mod",
    )(c, w, b.reshape(1, 3 * d))
    return out.reshape(bsz, 3, d)


def _rope_table_kernel(pos_ref, inv_ref, sel_ref, sign_ref, cos_ref, sin_ref):
    ang = pos_ref[...] * inv_ref[...]
    cos, sin = jnp.cos(ang), jnp.sin(ang)
    for t in range(cos_ref.shape[0]):
        pick = lambda v: jnp.dot(v, sel_ref[t], precision=lax.Precision.HIGHEST, preferred_element_type=jnp.float32)
        cos_ref[t] = pick(cos)
        sin_ref[t] = pick(sin) * sign_ref[t]


def _rope_tables(positions, dims, tm=512):
    n = positions.size
    base = max(dims)
    assert base == LANES and all(base % d == 0 and d & (d - 1) == 0 for d in dims)
    posf = jnp.broadcast_to(positions.reshape(n, 1).astype(jnp.float32), (n, LANES))
    inv = ROPE_THETA ** (-jnp.arange(0, base, 2, dtype=jnp.float32) / base)
    lane = jnp.arange(LANES)
    inv = inv[lane % (base // 2)].reshape(1, LANES)
    src = [((lane % d) % (d // 2)) * (base // d) for d in dims]
    sel = jnp.stack([(lane[:, None] == s[None, :]).astype(jnp.float32) for s in src])
    sign = jnp.stack([jnp.where(lane % d < d // 2, -1.0, 1.0) for d in dims]).astype(jnp.float32)
    sign = sign.reshape(len(dims), 1, LANES)
    shape = jax.ShapeDtypeStruct((len(dims), n, LANES), jnp.float32)
    return pl.pallas_call(
        _rope_table_kernel,
        grid=(n // tm,),
        in_specs=[pl.BlockSpec((tm, LANES), lambda i: (i, 0)),
                  pl.BlockSpec((1, LANES), lambda i: (0, 0)),
                  pl.BlockSpec(sel.shape, lambda i: (0, 0, 0)),
                  pl.BlockSpec(sign.shape, lambda i: (0, 0, 0))],
        out_specs=[pl.BlockSpec((len(dims), tm, LANES), lambda i: (0, i, 0))] * 2,
        out_shape=[shape, shape],
        name="rope_tables",
    )(posf, inv, sel, sign)


def _rope_lanes(xv, cos, sin_signed, dim):
    half = dim // 2
    if dim == LANES:
        partner = pltpu.roll(xv, half, axis=1)
    else:
        first = (_lane_iota(xv.shape) % dim) < half
        partner = jnp.where(first, pltpu.roll(xv, LANES - half, axis=1), pltpu.roll(xv, half, axis=1))
    return xv * cos + partner * sin_signed


def _prenorm(h, g, mod_ref):
    y = h * lax.rsqrt(jnp.mean(h * h, axis=-1, keepdims=True) + EPS) * g
    return y * (1.0 + mod_ref[1:2, :]) + mod_ref[0:1, :]


_A_COLS = {}
_o = 0
for _name, _w in (("q_lat", A_HEADS * A_LATENT), ("q_rope", A_HEADS * A_ROPE), ("c_kv", A_LATENT),
                  ("aux", LANES), ("kdup", LANES), ("gate", A_WIDTH), ("q_idx", IDX_HEADS * IDX_DIM)):
    _A_COLS[_name] = (_o, _o + _w)
    _o += _w
A_COLS_TOTAL = _o


def _a_weight(w_in):
    sizes = (A_HEADS * A_LATENT, A_HEADS * A_ROPE, A_LATENT, A_ROPE, A_WIDTH, IDX_HEADS * IDX_DIM, IDX_DIM, IDX_HEADS)
    parts, o = [], 0
    for s in sizes:
        parts.append(w_in[:, o:o + s])
        o += s
    q_lat, q_rope, c_kv, k_rope, gate, q_idx, k_idx, w_idx = parts
    d = w_in.shape[0]
    aux = jnp.concatenate([k_rope, w_idx, jnp.zeros((d, LANES - A_ROPE - IDX_HEADS), w_in.dtype)], axis=1)
    kdup = jnp.concatenate([k_idx, k_idx], axis=1)
    return _bf16(jnp.concatenate([q_lat, q_rope, c_kv, aux, kdup, gate, q_idx], axis=1))


def _proj_a_kernel(x_ref, mod_ref, g_ref, kvg_ref, w_ref, cos_ref, sin_ref,
                   qcat_ref, keys_ref, vt_ref, kdup_ref, qidx_ref, aux_ref, gate_ref):
    hn = _bf16(_prenorm(x_ref[...], g_ref[...], mod_ref))
    cos32, sin32 = cos_ref[0], sin_ref[0]
    cos64, sin64 = cos_ref[1], sin_ref[1]
    lane = _lane_iota(cos32.shape)
    qscale = (A_LATENT + A_ROPE) ** -0.5 * LOG2E

    def proj(name):
        lo, hi = _A_COLS[name]
        return _dot(hn, w_ref[:, lo:hi])

    q_lat = proj("q_lat") * qscale
    q_rope = proj("q_rope")
    per_slab = LANES // A_ROPE
    for h in range(A_HEADS):
        qcat_ref[h, :, 0:A_LATENT] = _bf16(q_lat[:, h * A_LATENT:(h + 1) * A_LATENT])
    for j in range(A_HEADS // per_slab):
        roped = _rope_lanes(q_rope[:, j * LANES:(j + 1) * LANES], cos32, sin32, A_ROPE) * qscale
        for t in range(per_slab):
            piece = roped if t == 0 else pltpu.roll(roped, LANES - A_ROPE * t, axis=1)
            qcat_ref[j * per_slab + t, :, A_LATENT:A_QPAD] = _bf16(jnp.where(lane < A_ROPE, piece, 0.0))

    c_kv = proj("c_kv")
    c_kv = c_kv * lax.rsqrt(jnp.mean(c_kv * c_kv, axis=-1, keepdims=True) + EPS) * kvg_ref[...]
    keys_ref[:, 0:A_LATENT] = _bf16(c_kv)
    vt_ref[0:A_LATENT, :] = _bf16(c_kv.T)
    vt_ref[A_LATENT:V_ROWS, :] = jnp.ones((V_ROWS - A_LATENT, vt_ref.shape[1]), jnp.bfloat16)
    aux = proj("aux")
    k_rope = _rope_lanes(aux, cos32, sin32, A_ROPE)
    keys_ref[:, A_LATENT:A_QPAD] = _bf16(jnp.where(lane < A_ROPE, k_rope, 0.0))
    aux_ref[...] = aux * IDX_HEADS ** -0.5

    kdup_ref[...] = _bf16(_rope_lanes(proj("kdup"), cos64, sin64, IDX_DIM))
    q_idx = proj("q_idx")
    per_slab_i = LANES // IDX_DIM
    for j in range(IDX_HEADS // per_slab_i):
        roped = _rope_lanes(q_idx[:, j * LANES:(j + 1) * LANES], cos64, sin64, IDX_DIM)
        for t in range(per_slab_i):
            keep = (lane >= t * IDX_DIM) & (lane < (t + 1) * IDX_DIM)
            qidx_ref[j * per_slab_i + t] = _bf16(jnp.where(keep, roped, 0.0))

    gate_ref[...] = _bf16(_silu(proj("gate")))


def _proj_a(x2, mod, g, kvg, w, cos, sin, bsz, seq):
    n, d = x2.shape
    tm = KEY_CHUNK
    tps = seq // tm
    row = lambda i: (i, 0)
    bhs = lambda i: (i // tps, 0, i % tps, 0)
    return pl.pallas_call(
        _proj_a_kernel,
        grid=(n // tm,),
        in_specs=[pl.BlockSpec((tm, d), row),
                  pl.BlockSpec((None, 3, d), lambda i: (i // tps, 0, 0)),
                  pl.BlockSpec((1, d), lambda i: (0, 0)),
                  pl.BlockSpec((1, A_LATENT), lambda i: (0, 0)),
                  pl.BlockSpec((d, A_COLS_TOTAL), lambda i: (0, 0)),
                  pl.BlockSpec((2, tm, LANES), lambda i: (0, i, 0)),
                  pl.BlockSpec((2, tm, LANES), lambda i: (0, i, 0))],
        out_specs=[pl.BlockSpec((None, A_HEADS, tm, A_QPAD), bhs),
                   pl.BlockSpec((tm, A_QPAD), row),
                   pl.BlockSpec((None, None, V_ROWS, tm), lambda i: (i // tps, i % tps, 0, 0)),
                   pl.BlockSpec((tm, LANES), row),
                   pl.BlockSpec((None, IDX_HEADS, tm, LANES), bhs),
                   pl.BlockSpec((tm, LANES), row),
                   pl.BlockSpec((tm, A_WIDTH), row)],
        out_shape=[jax.ShapeDtypeStruct((bsz, A_HEADS, seq, A_QPAD), jnp.bfloat16),
                   jax.ShapeDtypeStruct((n, A_QPAD), jnp.bfloat16),
                   jax.ShapeDtypeStruct((bsz, seq // tm, V_ROWS, tm), jnp.bfloat16),
                   jax.ShapeDtypeStruct((n, LANES), jnp.bfloat16),
                   jax.ShapeDtypeStruct((bsz, IDX_HEADS, seq, LANES), jnp.bfloat16),
                   jax.ShapeDtypeStruct((n, LANES), jnp.float32),
                   jax.ShapeDtypeStruct((n, A_WIDTH), jnp.bfloat16)],
        compiler_params=pltpu.CompilerParams(vmem_limit_bytes=VMEM_LIMIT),
        name="proj_a",
    )(x2, mod, g, kvg, w, cos, sin)


def _attn_a_kernel(qcat_ref, qidx_ref, aux_ref, keys_ref, vt_ref, kdup_ref, gate_ref, x_ref, mod_ref, wuv_ref,
                   wout_ref, out_ref, sc_ref, hi_ref, lo_ref, lo2_ref, bias_ref, sa_ref, sb_ref, mxa_ref, mxb_ref, m_ref, acc_ref, og_ref, *, topk):
    i = pl.program_id(1)
    n_chunks = (i + 2) // 2
    seq = keys_ref.shape[0] * KEY_CHUNK

    w_t = aux_ref[...].T
    qi_all = qidx_ref[...].reshape(IDX_HEADS * Q_BLOCK, LANES)
    qpos = i * Q_BLOCK + lax.broadcasted_iota(jnp.int32, (KEY_CHUNK, Q_BLOCK), 1)
    krow = lax.broadcasted_iota(jnp.int32, (KEY_CHUNK, Q_BLOCK), 0)

    n_pairs = (n_chunks + 1) // 2

    def for_chunk_pairs(fn, init):
        def body(t, carry):
            return fn(2 * t + 1, fn(2 * t, carry))
        return lax.fori_loop(0, n_pairs, body, init)

    def score_chunk(c, carry):
        rel = _dot_nt(kdup_ref[c], qi_all)
        acc = jnp.zeros((KEY_CHUNK, Q_BLOCK), jnp.float32)
        for h in range(IDX_HEADS):
            acc = acc + w_t[AUX_W_OFF + h:AUX_W_OFF + h + 1, :] * jnp.maximum(rel[:, h * Q_BLOCK:(h + 1) * Q_BLOCK], 0.0)
        bits = lax.bitcast_convert_type(acc, jnp.int32)
        key = bits ^ ((bits >> 31) & 0x7FFFFFFF)
        key = jnp.where(c * KEY_CHUNK + krow <= qpos, key, INT_MIN)
        sc_ref[c] = key
        hi_ref[c] = (key >> 16).astype(jnp.int16)
        lo_ref[c] = ((key & 0xFFFF) - HALF).astype(jnp.int16)
        return carry

    for_chunk_pairs(score_chunk, 0)

    def count(pred):
        def body(c, acc):
            hit = jnp.where(pred(sc_ref[c], c), 1, 0)
            return acc + jnp.sum(hit.reshape(KEY_CHUNK // 32, 32, Q_BLOCK), axis=0)
        acc = for_chunk_pairs(body, jnp.zeros((32, Q_BLOCK), jnp.int32))
        return jnp.sum(acc, axis=0, keepdims=True)

    def count16(ref16, pred):
        def body(c, acc):
            hit = jnp.where(pred(ref16[c]), jnp.int16(1), jnp.int16(0))
            words = pltpu.bitcast(hit, jnp.int32)
            return acc + jnp.sum(words.reshape(KEY_CHUNK // 64, 32, Q_BLOCK), axis=0)
        acc = for_chunk_pairs(body, jnp.zeros((32, Q_BLOCK), jnp.int32))
        acc = jnp.sum(acc, axis=0, keepdims=True)
        return (acc & 0xFFFF) + (acc >> 16)

    def bisect16(ref16, target, n_bits):
        def bit(b, t):
            cand = t + lax.shift_left(jnp.int32(1), 15 - b)
            cand16 = cand.astype(jnp.int16)
            return jnp.where(count16(ref16, lambda v: v >= cand16) >= target, cand, t)
        return lax.fori_loop(0, n_bits, bit, jnp.full((1, Q_BLOCK), -HALF, jnp.int32))

    all_selected = (i + 1) * Q_BLOCK <= topk
    n_bits = jnp.where(all_selected, 0, 16)
    t_hi = bisect16(hi_ref, topk, n_bits)
    t_hi16 = t_hi.astype(jnp.int16)
    above = count16(hi_ref, lambda v: v > t_hi16)

    def low_chunk(c, carry):
        lo2_ref[c] = jnp.where(hi_ref[c] == t_hi16, lo_ref[c], jnp.int16(-HALF))
        return carry

    for_chunk_pairs(low_chunk, 0)
    t_lo = bisect16(lo2_ref, topk - above, n_bits)
    thr = lax.shift_left(t_hi, 16) + (t_lo + HALF)

    need = topk - count(lambda k, c: k > thr)
    n_eq = count(lambda k, c: k == thr)
    excess = jnp.max(jnp.where(n_eq > need, 1, 0)) > 0
    idx_bits = (seq - 1).bit_length()

    def tie_bound():
        def bit(b, bound):
            cand = bound + lax.shift_left(jnp.int32(1), idx_bits - 1 - b)
            below = count(lambda k, c: jnp.where(k == thr, c * KEY_CHUNK + krow, seq) < cand)
            return jnp.where(below < need, cand, bound)
        return lax.fori_loop(0, idx_bits, bit, jnp.zeros((1, Q_BLOCK), jnp.int32))

    tie_idx = lax.cond(jnp.logical_and(excess, jnp.logical_not(all_selected)), tie_bound,
                       lambda: jnp.where(all_selected, -1, seq) + jnp.zeros((1, Q_BLOCK), jnp.int32))

    def bias_chunk(c, carry):
        k = sc_ref[c]
        tie_ok = jnp.where(c * KEY_CHUNK + krow <= tie_idx, 0.0, NEG)
        bias_ref[c] = jnp.where(k > thr, 0.0, jnp.where(k == thr, tie_ok, NEG))
        return carry

    for_chunk_pairs(bias_chunk, 0)

    m_ref[...] = jnp.full(m_ref.shape, NEG, jnp.float32)
    acc_ref[...] = jnp.zeros(acc_ref.shape, jnp.float32)

    slabs = A_HEADS // HEADS_PER_SLAB
    last_chunk = 2 * n_pairs - 1

    def scores(c, s_buf, mx_buf):
        c = jnp.minimum(c, last_chunk)
        kc = keys_ref[c]
        bias2 = jnp.concatenate([bias_ref[c]] * HEADS_PER_SLAB, axis=1)
        for pr in range(slabs):
            q_slab = qcat_ref[pr * HEADS_PER_SLAB:(pr + 1) * HEADS_PER_SLAB].reshape(HEADS_PER_SLAB * Q_BLOCK, A_QPAD)
            s = _dot_nt(kc, q_slab) + bias2
            s_buf[pr] = s
            mx_buf[pr] = jnp.max(s, axis=0, keepdims=True)

    def accumulate(c, s_buf, mx_buf):
        vt = vt_ref[c]
        for pr in range(slabs):
            m_old = m_ref[pr]
            m_new = jnp.maximum(m_old, mx_buf[pr])
            alpha = jnp.exp2(m_old - m_new)
            p = jnp.exp2(s_buf[pr] - m_new)
            m_ref[pr] = m_new
            acc_ref[pr] = acc_ref[pr] * alpha + _dot(vt, _bf16(p))

    scores(0, sa_ref, mxa_ref)

    def attend_pair(t, carry):
        scores(2 * t + 1, sb_ref, mxb_ref)
        accumulate(2 * t, sa_ref, mxa_ref)
        scores(2 * t + 2, sa_ref, mxa_ref)
        accumulate(2 * t + 1, sb_ref, mxb_ref)
        return carry

    lax.fori_loop(0, n_pairs, attend_pair, 0)

    for pr in range(slabs):
        o_t = acc_ref[pr, 0:A_LATENT, :] / acc_ref[pr, A_LATENT:A_LATENT + 1, :]
        for t in range(HEADS_PER_SLAB):
            h = pr * HEADS_PER_SLAB + t
            o_lat = _bf16(o_t[:, t * Q_BLOCK:(t + 1) * Q_BLOCK].T)
            o = _dot(o_lat, wuv_ref[h]) * gate_ref[:, h * A_VDIM:(h + 1) * A_VDIM]
            og_ref[:, h * A_VDIM:(h + 1) * A_VDIM] = _bf16(o)
    out_ref[...] = x_ref[...] + mod_ref[2:3, :] * _dot(og_ref[...], wout_ref[...])


def _attn_a(qcat, qidx, aux, keys, vt, kdup, gate, x2, mod, w_uv, w_out, topk):
    bsz, _, seq, _ = qcat.shape
    d = x2.shape[1]
    nq = seq // Q_BLOCK
    nkc = seq // KEY_CHUNK
    row = lambda b, i: (b * nq + i, 0)
    slabs = A_HEADS // HEADS_PER_SLAB
    slab_w = HEADS_PER_SLAB * Q_BLOCK
    return pl.pallas_call(
        functools.partial(_attn_a_kernel, topk=topk),
        grid=(bsz, nq),
        in_specs=[pl.BlockSpec((None, A_HEADS, Q_BLOCK, A_QPAD), lambda b, i: (b, 0, i, 0)),
                  pl.BlockSpec((None, IDX_HEADS, Q_BLOCK, LANES), lambda b, i: (b, 0, i, 0)),
                  pl.BlockSpec((Q_BLOCK, LANES), row),
                  pl.BlockSpec((None, nkc, KEY_CHUNK, A_QPAD), lambda b, i: (b, 0, 0, 0)),
                  pl.BlockSpec((None, nkc, V_ROWS, KEY_CHUNK), lambda b, i: (b, 0, 0, 0)),
                  pl.BlockSpec((None, nkc, KEY_CHUNK, LANES), lambda b, i: (b, 0, 0, 0)),
                  pl.BlockSpec((Q_BLOCK, A_WIDTH), row),
                  pl.BlockSpec((Q_BLOCK, d), row),
                  pl.BlockSpec((None, 3, d), lambda b, i: (b, 0, 0)),
                  pl.BlockSpec((A_HEADS, A_LATENT, A_VDIM), lambda b, i: (0, 0, 0)),
                  pl.BlockSpec((A_WIDTH, d), lambda b, i: (0, 0))],
        out_specs=pl.BlockSpec((Q_BLOCK, d), row),
        out_shape=jax.ShapeDtypeStruct(x2.shape, jnp.float32),
        scratch_shapes=[pltpu.VMEM((nkc, KEY_CHUNK, Q_BLOCK), jnp.int32),
                        pltpu.VMEM((nkc, KEY_CHUNK, Q_BLOCK), jnp.int16),
                        pltpu.VMEM((nkc, KEY_CHUNK, Q_BLOCK), jnp.int16),
                        pltpu.VMEM((nkc, KEY_CHUNK, Q_BLOCK), jnp.int16),
                        pltpu.VMEM((nkc, KEY_CHUNK, Q_BLOCK), jnp.float32),
                        pltpu.VMEM((slabs, KEY_CHUNK, slab_w), jnp.float32),
                        pltpu.VMEM((slabs, KEY_CHUNK, slab_w), jnp.float32),
                        pltpu.VMEM((slabs, 1, slab_w), jnp.float32),
                        pltpu.VMEM((slabs, 1, slab_w), jnp.float32),
                        pltpu.VMEM((slabs, 1, slab_w), jnp.float32),
                        pltpu.VMEM((slabs, V_ROWS, slab_w), jnp.float32),
                        pltpu.VMEM((Q_BLOCK, A_WIDTH), jnp.bfloat16)],
        compiler_params=pltpu.CompilerParams(vmem_limit_bytes=VMEM_LIMIT),
        name="attn_a",
    )(qcat, qidx, aux, keys.reshape(bsz, nkc, KEY_CHUNK, A_QPAD), vt, kdup.reshape(bsz, nkc, KEY_CHUNK, LANES),
      gate, x2, mod, w_uv, w_out)


_PERM_DILS = tuple(dil for _, dil in B_GROUPS if dil > 1)


def _dilation_perms(tm):
    dst = jnp.arange(tm)
    perms = []
    for dil in _PERM_DILS:
        src = (dst % (tm // dil)) * dil + dst // (tm // dil)
        perms.append((src[:, None] == dst[None, :]).astype(jnp.bfloat16))
    return jnp.stack(perms)


def _store_dilated(val, out_refs, perm_ref):
    vb = _bf16(val)
    tm, width = vb.shape
    for out_ref in out_refs:
        dil = out_ref.shape[1] // width
        if dil == 1:
            out_ref[...] = vb
            continue
        res = _bf16(_dot(perm_ref[_PERM_DILS.index(dil)], vb))
        rows = tm // dil
        for r in range(dil):
            out_ref[:, r * width:(r + 1) * width] = res[r * rows:(r + 1) * rows, :]


def _proj_b_kernel(*refs, with_kv):
    n_g = len(B_GROUPS)
    if with_kv:
        h_ref, mod_ref, g_ref, kvg_ref, wkv_ref, win_ref, cos_ref, sin_ref, perm_ref = refs[:9]
        outs = refs[9:]
        k_refs, v_refs, outs = outs[:n_g], outs[n_g:2 * n_g], outs[2 * n_g:]
    else:
        h_ref, mod_ref, g_ref, win_ref, cos_ref, sin_ref, perm_ref = refs[:7]
        outs = refs[7:]
    q_refs, gate_ref = outs[:n_g], outs[n_g]
    h = h_ref[...]
    cos, sin = cos_ref[...], sin_ref[...]
    width = B_WIDTH

    def rope(t, scale):
        return jnp.concatenate([_rope_lanes(t[:, hd * B_HDIM:(hd + 1) * B_HDIM], cos, sin, B_HDIM) * scale
                                for hd in range(B_HEADS)], axis=1)

    if with_kv:
        kvn = _bf16(h * lax.rsqrt(jnp.mean(h * h, axis=-1, keepdims=True) + EPS) * kvg_ref[...])
        _store_dilated(rope(_dot(kvn, wkv_ref[:, 0:width]), 1.0), k_refs, perm_ref)
        _store_dilated(_dot(kvn, wkv_ref[:, width:2 * width]), v_refs, perm_ref)
    hn = _bf16(_prenorm(h, g_ref[...], mod_ref))
    qscale = B_HDIM ** -0.5 * LOG2E
    for gi, q_ref in enumerate(q_refs):
        _store_dilated(rope(_dot(hn, win_ref[:, gi * width:(gi + 1) * width]), qscale), [q_ref], perm_ref)
    gate_ref[...] = _bf16(_silu(_dot(hn, win_ref[:, n_g * width:(n_g + 1) * width])))


def _proj_b(h2, mod, g, kvg, w_kv, w_in, cos, sin, bsz, seq, with_kv, tm=256):
    n, d = h2.shape
    tps = seq // tm
    row = lambda i: (i, 0)
    const = lambda i: (0, 0)
    in_specs = [pl.BlockSpec((tm, d), row), pl.BlockSpec((None, 3, d), lambda i: (i // tps, 0, 0)),
                pl.BlockSpec((1, d), const)]
    args = [h2, mod, g]
    if with_kv:
        in_specs += [pl.BlockSpec((1, d), const), pl.BlockSpec(w_kv.shape, const)]
        args += [kvg, w_kv]
    perms = _dilation_perms(tm)
    in_specs += [pl.BlockSpec(w_in.shape, const), pl.BlockSpec((tm, LANES), row), pl.BlockSpec((tm, LANES), row),
                 pl.BlockSpec(perms.shape, lambda i: (0, 0, 0))]
    args += [w_in, cos, sin, perms]
    dil_specs = [pl.BlockSpec((None, tm // dil, dil * B_WIDTH), lambda i: (i // tps, i % tps, 0)) for _, dil in B_GROUPS]
    dil_shapes = [jax.ShapeDtypeStruct((bsz, seq // dil, dil * B_WIDTH), jnp.bfloat16) for _, dil in B_GROUPS]
    n_sets = 3 if with_kv else 1
    return pl.pallas_call(
        functools.partial(_proj_b_kernel, with_kv=with_kv),
        grid=(n // tm,),
        in_specs=in_specs,
        out_specs=dil_specs * n_sets + [pl.BlockSpec((tm, B_WIDTH), row)],
        out_shape=dil_shapes * n_sets + [jax.ShapeDtypeStruct((n, B_WIDTH), jnp.bfloat16)],
        compiler_params=pltpu.CompilerParams(vmem_limit_bytes=VMEM_LIMIT),
        name="proj_b",
    )(*args)


def _dilated_kernel(q_ref, kc_ref, vc_ref, o_ref, lse_ref, kp_ref, vp_ref):
    j = pl.program_id(2)
    w = q_ref.shape[0]

    @pl.when(j == 0)
    def _():
        kp_ref[...] = jnp.zeros(kp_ref.shape, kp_ref.dtype)
        vp_ref[...] = jnp.zeros(vp_ref.shape, vp_ref.dtype)

    qi = lax.broadcasted_iota(jnp.int32, (w, w), 0)
    ki = lax.broadcasted_iota(jnp.int32, (w, w), 1)
    bias = jnp.concatenate([jnp.where(jnp.logical_and(ki >= qi, j > 0), 0.0, NEG),
                            jnp.where(ki <= qi, 0.0, NEG)], axis=1)
    ones = jnp.ones((2 * w, B_HDIM), jnp.bfloat16)
    lses = []
    for hd in range(B_HEADS):
        sl = slice(hd * B_HDIM, (hd + 1) * B_HDIM)
        keys = jnp.concatenate([kp_ref[:, sl], kc_ref[:, sl]], axis=0)
        vals = jnp.concatenate([vp_ref[:, sl], vc_ref[:, sl]], axis=0)
        s = _dot_nt(q_ref[:, sl], keys) + bias
        m = jnp.max(s, axis=-1, keepdims=True)
        p = _bf16(jnp.exp2(s - m))
        ol = _dot(p, jnp.concatenate([vals, ones], axis=1))
        l = ol[:, B_HDIM:]
        o_ref[:, sl] = _bf16(ol[:, :B_HDIM] / l)
        lses.append(m + jnp.log2(l[:, 0:1]))
    lse_ref[...] = jnp.concatenate(lses + [jnp.zeros((w, LANES - B_HEADS), jnp.float32)], axis=1)
    kp_ref[...] = kc_ref[...]
    vp_ref[...] = vc_ref[...]


def _dilated(q, k, v, window, dil):
    bsz, n, _ = q.shape
    w = window // dil
    nb = n // w
    cur = lambda b, r, j: (b, j, r)
    blk = pl.BlockSpec((None, w, B_WIDTH), cur)
    return pl.pallas_call(
        _dilated_kernel,
        grid=(bsz, dil, nb),
        in_specs=[blk, blk, blk],
        out_specs=[blk, pl.BlockSpec((None, None, w, LANES), lambda b, r, j: (b, r, j, 0))],
        out_shape=[jax.ShapeDtypeStruct(q.shape, jnp.bfloat16),
                   jax.ShapeDtypeStruct((bsz, dil, n, LANES), jnp.float32)],
        scratch_shapes=[pltpu.VMEM((w, B_WIDTH), jnp.bfloat16), pltpu.VMEM((w, B_WIDTH), jnp.bfloat16)],
        compiler_params=pltpu.CompilerParams(dimension_semantics=("arbitrary", "arbitrary", "arbitrary")),
        name=f"dilated_{dil}",
    )(q, k, v)


def _merge_kernel(*refs, final):
    n_g = len(B_GROUPS)
    o_refs, lse_refs = refs[:n_g], refs[n_g:2 * n_g]
    gate_ref, h_ref, mod_ref, wout_ref, fin_ref, unperm_ref, out_ref, ls_ref, og_ref = refs[2 * n_g:]
    tm = h_ref.shape[0]
    o_seq = []
    for gi, (_, dil) in enumerate(B_GROUPS):
        rows = tm // dil
        for r in range(dil):
            dst = pl.ds(r, rows, stride=dil) if dil > 1 else slice(None)
            ls_ref[gi, dst, :] = lse_refs[gi][r]
        if dil == 1:
            o_seq.append(o_refs[gi][...].astype(jnp.float32))
        else:
            stacked = jnp.concatenate([o_refs[gi][:, r * B_WIDTH:(r + 1) * B_WIDTH] for r in range(dil)], axis=0)
            o_seq.append(_dot(unperm_ref[_PERM_DILS.index(dil)], stacked))
    lses = [ls_ref[gi] for gi in range(n_g)]
    m = functools.reduce(jnp.maximum, lses)
    e = [jnp.exp2(t - m) for t in lses]
    inv = 1.0 / functools.reduce(lambda a, b: a + b, e)
    for hd in range(B_HEADS):
        sl = slice(hd * B_HDIM, (hd + 1) * B_HDIM)
        o = functools.reduce(lambda a, b: a + b,
                             [(e[gi] * inv)[:, hd:hd + 1] * o_seq[gi][:, sl] for gi in range(n_g)])
        og_ref[:, sl] = _bf16(o * gate_ref[:, sl])
    h = h_ref[...] + mod_ref[2:3, :] * _dot(og_ref[...], wout_ref[...])
    if final:
        h = h * lax.rsqrt(jnp.mean(h * h, axis=-1, keepdims=True) + EPS) * fin_ref[...]
    out_ref[...] = h


def _merge(os_, lses, gate, h2, mod, w_out, fin, seq, final, tm=256):
    n, d = h2.shape
    tps = seq // tm
    row = lambda i: (i, 0)
    rows = pl.BlockSpec((tm, d), row)
    n_g = len(B_GROUPS)
    o_specs = [pl.BlockSpec((None, tm // dil, dil * B_WIDTH), lambda i: (i // tps, i % tps, 0)) for _, dil in B_GROUPS]
    lse_specs = [pl.BlockSpec((None, dil, tm // dil, LANES), lambda i: (i // tps, 0, i % tps, 0)) for _, dil in B_GROUPS]
    unperms = jnp.swapaxes(_dilation_perms(tm), 1, 2)
    return pl.pallas_call(
        functools.partial(_merge_kernel, final=final),
        grid=(n // tm,),
        in_specs=o_specs + lse_specs + [rows, rows, pl.BlockSpec((None, 3, d), lambda i: (i // tps, 0, 0)),
                                        pl.BlockSpec(w_out.shape, lambda i: (0, 0)),
                                        pl.BlockSpec((1, d), lambda i: (0, 0)),
                                        pl.BlockSpec(unperms.shape, lambda i: (0, 0, 0))],
        out_specs=rows,
        out_shape=jax.ShapeDtypeStruct((n, d), jnp.float32),
        scratch_shapes=[pltpu.VMEM((n_g, tm, LANES), jnp.float32),
                        pltpu.VMEM((tm, B_WIDTH), jnp.bfloat16)],
        compiler_params=pltpu.CompilerParams(vmem_limit_bytes=VMEM_LIMIT),
        name="merge_out",
    )(*os_, *lses, gate, h2, mod, w_out, fin, unperms)


def _final_norm_kernel(h_ref, g_ref, out_ref):
    h = h_ref[...]
    out_ref[...] = h * lax.rsqrt(jnp.mean(h * h, axis=-1, keepdims=True) + EPS) * g_ref[...]


def kernel(x, c, positions, a_norm, a_ada_w, a_ada_b, a_w_in, a_kv_norm, a_w_uv, a_w_out, kv_norm, w_kv, b_norm, b_ada_w, b_ada_b, b_w_in, b_w_out, final_norm):
    bsz, seq, d = x.shape
    n = bsz * seq
    n_a, n_b = a_norm.shape[0], b_norm.shape[0]
    topk = min(TOPK_MAX, seq // 4)
    assert topk % Q_BLOCK == 0 and seq % KEY_CHUNK == 0 and d == B_WIDTH
    assert all(seq % window == 0 for window, _ in B_GROUPS)

    cos, sin = _rope_tables(positions, (A_ROPE, IDX_DIM, B_HDIM))
    h = x.reshape(n, d)
    for li in range(n_a):
        mod = _ada(c, a_ada_w[li], a_ada_b[li])
        qcat, keys, vt, kdup, qidx, aux, gate = _proj_a(
            h, mod, a_norm[li].reshape(1, d), a_kv_norm[li].reshape(1, A_LATENT), _a_weight(a_w_in[li]),
            cos[0:2], sin[0:2], bsz, seq)
        h = _attn_a(qcat, qidx, aux, keys, vt, kdup, gate, h, mod, _bf16(a_w_uv[li]), _bf16(a_w_out[li]), topk)
    k_sh = v_sh = None
    for li in range(n_b):
        mod = _ada(c, b_ada_w[li], b_ada_b[li])
        outs = _proj_b(h, mod, b_norm[li].reshape(1, d), kv_norm.reshape(1, d), _bf16(w_kv), _bf16(b_w_in[li]),
                       cos[2], sin[2], bsz, seq, with_kv=(li == 0))
        n_g = len(B_GROUPS)
        if li == 0:
            k_sh, v_sh = outs[:n_g], outs[n_g:2 * n_g]
            outs = outs[2 * n_g:]
        q_groups, gate = outs[:n_g], outs[n_g]
        res = [_dilated(qg, kg, vg, window, dil)
               for qg, kg, vg, (window, dil) in zip(q_groups, k_sh, v_sh, B_GROUPS)]
        h = _merge([r[0] for r in res], [r[1] for r in res], gate, h, mod, _bf16(b_w_out[li]),
                   final_norm.reshape(1, d), seq, final=(li == n_b - 1))
    if n_b == 0:
        h = pl.pallas_call(
            _final_norm_kernel, grid=(n // 256,),
            in_specs=[pl.BlockSpec((256, d), lambda i: (i, 0)), pl.BlockSpec((1, d), lambda i: (0, 0))],
            out_specs=pl.BlockSpec((256, d), lambda i: (i, 0)),
            out_shape=jax.ShapeDtypeStruct((n, d), jnp.float32), name="final_norm",
        )(h, final_norm.reshape(1, d))
    return h.reshape(bsz, seq, d)
```

```python
import functools
import math

import jax
import jax.numpy as jnp
from jax import lax
from jax.experimental import pallas as pl
from jax.experimental.pallas import tpu as pltpu

ROPE_THETA = 10000.0
EPS = 1e-6
LOG2E = 1.4426950408889634

A_HEADS = 16
A_LATENT = 128
A_ROPE = 32
A_VDIM = 128
A_WIDTH = A_HEADS * A_VDIM
IDX_HEADS = 8
IDX_DIM = 64
TOPK_MAX = 256
Q_BLOCK = 128
KEY_CHUNK = 256
A_QPAD = 256
HEADS_PER_SLAB = 2
V_ROWS = A_LATENT + 16

B_HEADS = 8
B_HDIM = 128
B_GROUPS = ((128, 1), (512, 4), (2048, 16))
B_WIDTH = B_HEADS * B_HDIM

LANES = 128
NEG = -1e30
INT_MIN = -(2 ** 31)
HALF = 2 ** 15
VMEM_LIMIT = 56 * 1024 * 1024

AUX_W_OFF = A_ROPE


def _bf16(t):
    return t.astype(jnp.bfloat16)


def _dot(a, b):
    return jnp.dot(a, b, preferred_element_type=jnp.float32)


def _dot_nt(a, b):
    return lax.dot_general(a, b, (((1,), (1,)), ((), ())), preferred_element_type=jnp.float32)


def _silu(t):
    return t * (1.0 / (1.0 + jnp.exp(-t)))


def _lane_iota(shape):
    return lax.broadcasted_iota(jnp.int32, shape, len(shape) - 1)


def _ada_kernel(c_ref, w_ref, b_ref, out_ref):
    out_ref[...] = _dot(_bf16(_silu(c_ref[...])), _bf16(w_ref[...])) + b_ref[...]


def _ada(c, w, b):
    bsz, d = c.shape
    out = pl.pallas_call(
        _ada_kernel,
        grid=(3,),
        in_specs=[pl.BlockSpec((bsz, d), lambda j: (0, 0)),
                  pl.BlockSpec((d, d), lambda j: (0, j)),
                  pl.BlockSpec((1, d), lambda j: (0, j))],
        out_specs=pl.BlockSpec((bsz, d), lambda j: (0, j)),
        out_shape=jax.ShapeDtypeStruct((bsz, 3 * d), jnp.float32),
        name="ada_mod",
    )(c, w, b.reshape(1, 3 * d))
    return out.reshape(bsz, 3, d)


def _rope_table_kernel(pos_ref, inv_ref, sel_ref, sign_ref, cos_ref, sin_ref):
    ang = pos_ref[...] * inv_ref[...]
    cos, sin = jnp.cos(ang), jnp.sin(ang)
    for t in range(cos_ref.shape[0]):
        pick = lambda v: jnp.dot(v, sel_ref[t], precision=lax.Precision.HIGHEST, preferred_element_type=jnp.float32)
        cos_ref[t] = pick(cos)
        sin_ref[t] = pick(sin) * sign_ref[t]


def _rope_tables(positions, dims, tm=512):
    n = positions.size
    base = max(dims)
    assert base == LANES and all(base % d == 0 and d & (d - 1) == 0 for d in dims)
    posf = jnp.broadcast_to(positions.reshape(n, 1).astype(jnp.float32), (n, LANES))
    inv = ROPE_THETA ** (-jnp.arange(0, base, 2, dtype=jnp.float32) / base)
    lane = jnp.arange(LANES)
    inv = inv[lane % (base // 2)].reshape(1, LANES)
    src = [((lane % d) % (d // 2)) * (base // d) for d in dims]
    sel = jnp.stack([(lane[:, None] == s[None, :]).astype(jnp.float32) for s in src])
    sign = jnp.stack([jnp.where(lane % d < d // 2, -1.0, 1.0) for d in dims]).astype(jnp.float32)
    sign = sign.reshape(len(dims), 1, LANES)
    shape = jax.ShapeDtypeStruct((len(dims), n, LANES), jnp.float32)
    return pl.pallas_call(
        _rope_table_kernel,
        grid=(n // tm,),
        in_specs=[pl.BlockSpec((tm, LANES), lambda i: (i, 0)),
                  pl.BlockSpec((1, LANES), lambda i: (0, 0)),
                  pl.BlockSpec(sel.shape, lambda i: (0, 0, 0)),
                  pl.BlockSpec(sign.shape, lambda i: (0, 0, 0))],
        out_specs=[pl.BlockSpec((len(dims), tm, LANES), lambda i: (0, i, 0))] * 2,
        out_shape=[shape, shape],
        name="rope_tables",
    )(posf, inv, sel, sign)


def _rope_lanes(xv, cos, sin_signed, dim):
    half = dim // 2
    if dim == LANES:
        partner = pltpu.roll(xv, half, axis=1)
    else:
        first = (_lane_iota(xv.shape) % dim) < half
        partner = jnp.where(first, pltpu.roll(xv, LANES - half, axis=1), pltpu.roll(xv, half, axis=1))
    return xv * cos + partner * sin_signed


def _prenorm(h, g, mod_ref):
    y = h * lax.rsqrt(jnp.mean(h * h, axis=-1, keepdims=True) + EPS) * g
    return y * (1.0 + mod_ref[1:2, :]) + mod_ref[0:1, :]


_A_COLS = {}
_o = 0
for _name, _w in (("q_lat", A_HEADS * A_LATENT), ("q_rope", A_HEADS * A_ROPE), ("c_kv", A_LATENT),
                  ("aux", LANES), ("kdup", LANES), ("gate", A_WIDTH), ("q_idx", IDX_HEADS * IDX_DIM)):
    _A_COLS[_name] = (_o, _o + _w)
    _o += _w
A_COLS_TOTAL = _o


def _a_weight(w_in):
    sizes = (A_HEADS * A_LATENT, A_HEADS * A_ROPE, A_LATENT, A_ROPE, A_WIDTH, IDX_HEADS * IDX_DIM, IDX_DIM, IDX_HEADS)
    parts, o = [], 0
    for s in sizes:
        parts.append(w_in[:, o:o + s])
        o += s
    q_lat, q_rope, c_kv, k_rope, gate, q_idx, k_idx, w_idx = parts
    d = w_in.shape[0]
    aux = jnp.concatenate([k_rope, w_idx, jnp.zeros((d, LANES - A_ROPE - IDX_HEADS), w_in.dtype)], axis=1)
    kdup = jnp.concatenate([k_idx, k_idx], axis=1)
    return _bf16(jnp.concatenate([q_lat, q_rope, c_kv, aux, kdup, gate, q_idx], axis=1))


def _proj_a_kernel(x_ref, mod_ref, g_ref, kvg_ref, w_ref, cos_ref, sin_ref,
                   qcat_ref, keys_ref, vt_ref, kdup_ref, qidx_ref, aux_ref, gate_ref):
    hn = _bf16(_prenorm(x_ref[...], g_ref[...], mod_ref))
    cos32, sin32 = cos_ref[0], sin_ref[0]
    cos64, sin64 = cos_ref[1], sin_ref[1]
    lane = _lane_iota(cos32.shape)
    qscale = (A_LATENT + A_ROPE) ** -0.5 * LOG2E

    def proj(name):
        lo, hi = _A_COLS[name]
        return _dot(hn, w_ref[:, lo:hi])

    q_blocks = x_ref.shape[0] // Q_BLOCK

    def put_q(h, row0, piece_t):
        slab, col0 = h // HEADS_PER_SLAB, (h % HEADS_PER_SLAB) * Q_BLOCK
        for qb in range(q_blocks):
            qcat_ref[qb, slab, row0:row0 + piece_t.shape[0], col0:col0 + Q_BLOCK] = _bf16(
                piece_t[:, qb * Q_BLOCK:(qb + 1) * Q_BLOCK])

    q_lat = proj("q_lat") * qscale
    q_rope = proj("q_rope")
    per_slab = LANES // A_ROPE
    pad = jnp.zeros((A_QPAD - A_LATENT - A_ROPE, x_ref.shape[0]), jnp.float32)
    for h in range(A_HEADS):
        put_q(h, 0, q_lat[:, h * A_LATENT:(h + 1) * A_LATENT].T)
    for j in range(A_HEADS // per_slab):
        roped_t = (_rope_lanes(q_rope[:, j * LANES:(j + 1) * LANES], cos32, sin32, A_ROPE) * qscale).T
        for t in range(per_slab):
            put_q(j * per_slab + t, A_LATENT, roped_t[t * A_ROPE:(t + 1) * A_ROPE, :])
            put_q(j * per_slab + t, A_LATENT + A_ROPE, pad)

    c_kv = proj("c_kv")
    c_kv = c_kv * lax.rsqrt(jnp.mean(c_kv * c_kv, axis=-1, keepdims=True) + EPS) * kvg_ref[...]
    keys_ref[:, 0:A_LATENT] = _bf16(c_kv)
    vt_ref[0:A_LATENT, :] = _bf16(c_kv.T)
    vt_ref[A_LATENT:V_ROWS, :] = jnp.ones((V_ROWS - A_LATENT, vt_ref.shape[1]), jnp.bfloat16)
    aux = proj("aux")
    k_rope = _rope_lanes(aux, cos32, sin32, A_ROPE)
    keys_ref[:, A_LATENT:A_QPAD] = _bf16(jnp.where(lane < A_ROPE, k_rope, 0.0))
    aux_ref[...] = aux * IDX_HEADS ** -0.5

    kdup_ref[...] = _bf16(_rope_lanes(proj("kdup"), cos64, sin64, IDX_DIM))
    q_idx = proj("q_idx")
    per_slab_i = LANES // IDX_DIM
    feat = lax.broadcasted_iota(jnp.int32, (LANES, x_ref.shape[0]), 0)
    for j in range(IDX_HEADS // per_slab_i):
        roped_t = _rope_lanes(q_idx[:, j * LANES:(j + 1) * LANES], cos64, sin64, IDX_DIM).T
        for t in range(per_slab_i):
            keep = (feat >= t * IDX_DIM) & (feat < (t + 1) * IDX_DIM)
            head_t = _bf16(jnp.where(keep, roped_t, 0.0))
            h = j * per_slab_i + t
            for qb in range(q_blocks):
                qidx_ref[qb, :, h * Q_BLOCK:(h + 1) * Q_BLOCK] = head_t[:, qb * Q_BLOCK:(qb + 1) * Q_BLOCK]

    gate_ref[...] = _bf16(_silu(proj("gate")))


def _proj_a(x2, mod, g, kvg, w, cos, sin, bsz, seq):
    n, d = x2.shape
    tm = KEY_CHUNK
    tps = seq // tm
    row = lambda i: (i, 0)
    qbt = tm // Q_BLOCK
    nq = seq // Q_BLOCK
    slabs, slab_w = A_HEADS // HEADS_PER_SLAB, HEADS_PER_SLAB * Q_BLOCK
    return pl.pallas_call(
        _proj_a_kernel,
        grid=(n // tm,),
        in_specs=[pl.BlockSpec((tm, d), row),
                  pl.BlockSpec((None, 3, d), lambda i: (i // tps, 0, 0)),
                  pl.BlockSpec((1, d), lambda i: (0, 0)),
                  pl.BlockSpec((1, A_LATENT), lambda i: (0, 0)),
                  pl.BlockSpec((d, A_COLS_TOTAL), lambda i: (0, 0)),
                  pl.BlockSpec((2, tm, LANES), lambda i: (0, i, 0)),
                  pl.BlockSpec((2, tm, LANES), lambda i: (0, i, 0))],
        out_specs=[pl.BlockSpec((None, qbt, slabs, A_QPAD, slab_w), lambda i: (i // tps, i % tps, 0, 0, 0)),
                   pl.BlockSpec((tm, A_QPAD), row),
                   pl.BlockSpec((None, None, V_ROWS, tm), lambda i: (i // tps, i % tps, 0, 0)),
                   pl.BlockSpec((tm, LANES), row),
                   pl.BlockSpec((None, qbt, LANES, IDX_HEADS * Q_BLOCK), lambda i: (i // tps, i % tps, 0, 0)),
                   pl.BlockSpec((tm, LANES), row),
                   pl.BlockSpec((tm, A_WIDTH), row)],
        out_shape=[jax.ShapeDtypeStruct((bsz, nq, slabs, A_QPAD, slab_w), jnp.bfloat16),
                   jax.ShapeDtypeStruct((n, A_QPAD), jnp.bfloat16),
                   jax.ShapeDtypeStruct((bsz, seq // tm, V_ROWS, tm), jnp.bfloat16),
                   jax.ShapeDtypeStruct((n, LANES), jnp.bfloat16),
                   jax.ShapeDtypeStruct((bsz, nq, LANES, IDX_HEADS * Q_BLOCK), jnp.bfloat16),
                   jax.ShapeDtypeStruct((n, LANES), jnp.float32),
                   jax.ShapeDtypeStruct((n, A_WIDTH), jnp.bfloat16)],
        compiler_params=pltpu.CompilerParams(vmem_limit_bytes=VMEM_LIMIT),
        name="proj_a",
    )(x2, mod, g, kvg, w, cos, sin)


def _attn_a_kernel(qcat_ref, qidx_ref, aux_ref, keys_ref, vt_ref, kdup_ref, gate_ref, x_ref, mod_ref, wuv_ref,
                   wout_ref, out_ref, sc_ref, hi_ref, lo_ref, lo2_ref, bias_ref, sa_ref, sb_ref, mxa_ref, mxb_ref, m_ref, acc_ref, og_ref, *, topk):
    i = pl.program_id(1)
    n_chunks = (i + 2) // 2
    seq = keys_ref.shape[0] * KEY_CHUNK

    w_t = aux_ref[...].T
    qi_all = qidx_ref[...]
    qpos = i * Q_BLOCK + lax.broadcasted_iota(jnp.int32, (KEY_CHUNK, Q_BLOCK), 1)
    krow = lax.broadcasted_iota(jnp.int32, (KEY_CHUNK, Q_BLOCK), 0)

    n_pairs = (n_chunks + 1) // 2

    def for_chunk_pairs(fn, init):
        def body(t, carry):
            return fn(2 * t + 1, fn(2 * t, carry))
        return lax.fori_loop(0, n_pairs, body, init)

    def score_chunk(c, carry):
        rel = _dot(kdup_ref[c], qi_all)
        acc = jnp.zeros((KEY_CHUNK, Q_BLOCK), jnp.float32)
        for h in range(IDX_HEADS):
            acc = acc + w_t[AUX_W_OFF + h:AUX_W_OFF + h + 1, :] * jnp.maximum(rel[:, h * Q_BLOCK:(h + 1) * Q_BLOCK], 0.0)
        bits = lax.bitcast_convert_type(acc, jnp.int32)
        key = bits ^ ((bits >> 31) & 0x7FFFFFFF)
        key = jnp.where(c * KEY_CHUNK + krow <= qpos, key, INT_MIN)
        sc_ref[c] = key
        hi_ref[c] = (key >> 16).astype(jnp.int16)
        lo_ref[c] = ((key & 0xFFFF) - HALF).astype(jnp.int16)
        return carry

    for_chunk_pairs(score_chunk, 0)

    def count(pred):
        def body(c, acc):
            hit = jnp.where(pred(sc_ref[c], c), 1, 0)
            return acc + jnp.sum(hit.reshape(KEY_CHUNK // 32, 32, Q_BLOCK), axis=0)
        acc = for_chunk_pairs(body, jnp.zeros((32, Q_BLOCK), jnp.int32))
        return jnp.sum(acc, axis=0, keepdims=True)

    def count16(ref16, pred):
        def body(c, acc):
            hit = jnp.where(pred(ref16[c]), jnp.int16(1), jnp.int16(0))
            words = pltpu.bitcast(hit, jnp.int32)
            return acc + jnp.sum(words.reshape(KEY_CHUNK // 64, 32, Q_BLOCK), axis=0)
        acc = for_chunk_pairs(body, jnp.zeros((32, Q_BLOCK), jnp.int32))
        acc = jnp.sum(acc, axis=0, keepdims=True)
        return (acc & 0xFFFF) + (acc >> 16)

    def bisect16(ref16, target, n_bits):
        def bit(b, t):
            cand = t + lax.shift_left(jnp.int32(1), 15 - b)
            cand16 = cand.astype(jnp.int16)
            return jnp.where(count16(ref16, lambda v: v >= cand16) >= target, cand, t)
        return lax.fori_loop(0, n_bits, bit, jnp.full((1, Q_BLOCK), -HALF, jnp.int32))

    all_selected = (i + 1) * Q_BLOCK <= topk
    n_bits = jnp.where(all_selected, 0, 16)
    t_hi = bisect16(hi_ref, topk, n_bits)
    t_hi16 = t_hi.astype(jnp.int16)
    above = count16(hi_ref, lambda v: v > t_hi16)

    def low_chunk(c, carry):
        lo2_ref[c] = jnp.where(hi_ref[c] == t_hi16, lo_ref[c], jnp.int16(-HALF))
        return carry

    for_chunk_pairs(low_chunk, 0)
    t_lo = bisect16(lo2_ref, topk - above, n_bits)
    thr = lax.shift_left(t_hi, 16) + (t_lo + HALF)

    need = topk - count(lambda k, c: k > thr)
    n_eq = count(lambda k, c: k == thr)
    excess = jnp.max(jnp.where(n_eq > need, 1, 0)) > 0
    idx_bits = (seq - 1).bit_length()

    def tie_bound():
        def bit(b, bound):
            cand = bound + lax.shift_left(jnp.int32(1), idx_bits - 1 - b)
            below = count(lambda k, c: jnp.where(k == thr, c * KEY_CHUNK + krow, seq) < cand)
            return jnp.where(below < need, cand, bound)
        return lax.fori_loop(0, idx_bits, bit, jnp.zeros((1, Q_BLOCK), jnp.int32))

    tie_idx = lax.cond(jnp.logical_and(excess, jnp.logical_not(all_selected)), tie_bound,
                       lambda: jnp.where(all_selected, -1, seq) + jnp.zeros((1, Q_BLOCK), jnp.int32))

    def bias_chunk(c, carry):
        k = sc_ref[c]
        tie_ok = jnp.where(c * KEY_CHUNK + krow <= tie_idx, 0.0, NEG)
        bias_ref[c] = jnp.where(k > thr, 0.0, jnp.where(k == thr, tie_ok, NEG))
        return carry

    for_chunk_pairs(bias_chunk, 0)

    m_ref[...] = jnp.full(m_ref.shape, NEG, jnp.float32)
    acc_ref[...] = jnp.zeros(acc_ref.shape, jnp.float32)

    slabs = A_HEADS // HEADS_PER_SLAB
    last_chunk = 2 * n_pairs - 1

    def scores(c, s_buf, mx_buf):
        c = jnp.minimum(c, last_chunk)
        kc = keys_ref[c]
        bias2 = jnp.concatenate([bias_ref[c]] * HEADS_PER_SLAB, axis=1)
        for pr in range(slabs):
            s = _dot(kc, qcat_ref[pr]) + bias2
            s_buf[pr] = s
            mx_buf[pr] = jnp.max(s, axis=0, keepdims=True)

    def accumulate(c, s_buf, mx_buf):
        vt = vt_ref[c]
        for pr in range(slabs):
            m_old = m_ref[pr]
            m_new = jnp.maximum(m_old, mx_buf[pr])
            alpha = jnp.exp2(m_old - m_new)
            p = jnp.exp2(s_buf[pr] - m_new)
            m_ref[pr] = m_new
            acc_ref[pr] = acc_ref[pr] * alpha + _dot(vt, _bf16(p))

    scores(0, sa_ref, mxa_ref)

    def attend_pair(t, carry):
        scores(2 * t + 1, sb_ref, mxb_ref)
        accumulate(2 * t, sa_ref, mxa_ref)
        scores(2 * t + 2, sa_ref, mxa_ref)
        accumulate(2 * t + 1, sb_ref, mxb_ref)
        return carry

    lax.fori_loop(0, n_pairs, attend_pair, 0)

    for pr in range(slabs):
        o_t = acc_ref[pr, 0:A_LATENT, :] / acc_ref[pr, A_LATENT:A_LATENT + 1, :]
        for t in range(HEADS_PER_SLAB):
            h = pr * HEADS_PER_SLAB + t
            o_lat = _bf16(o_t[:, t * Q_BLOCK:(t + 1) * Q_BLOCK].T)
            o = _dot(o_lat, wuv_ref[h]) * gate_ref[:, h * A_VDIM:(h + 1) * A_VDIM]
            og_ref[:, h * A_VDIM:(h + 1) * A_VDIM] = _bf16(o)
    out_ref[...] = x_ref[...] + mod_ref[2:3, :] * _dot(og_ref[...], wout_ref[...])


def _attn_a(qcat, qidx, aux, keys, vt, kdup, gate, x2, mod, w_uv, w_out, topk):
    bsz, nq = qcat.shape[:2]
    seq = nq * Q_BLOCK
    d = x2.shape[1]
    nkc = seq // KEY_CHUNK
    row = lambda b, i: (b * nq + i, 0)
    slabs = A_HEADS // HEADS_PER_SLAB
    slab_w = HEADS_PER_SLAB * Q_BLOCK
    return pl.pallas_call(
        functools.partial(_attn_a_kernel, topk=topk),
        grid=(bsz, nq),
        in_specs=[pl.BlockSpec((None, None, slabs, A_QPAD, slab_w), lambda b, i: (b, i, 0, 0, 0)),
                  pl.BlockSpec((None, None, LANES, IDX_HEADS * Q_BLOCK), lambda b, i: (b, i, 0, 0)),
                  pl.BlockSpec((Q_BLOCK, LANES), row),
                  pl.BlockSpec((None, nkc, KEY_CHUNK, A_QPAD), lambda b, i: (b, 0, 0, 0)),
                  pl.BlockSpec((None, nkc, V_ROWS, KEY_CHUNK), lambda b, i: (b, 0, 0, 0)),
                  pl.BlockSpec((None, nkc, KEY_CHUNK, LANES), lambda b, i: (b, 0, 0, 0)),
                  pl.BlockSpec((Q_BLOCK, A_WIDTH), row),
                  pl.BlockSpec((Q_BLOCK, d), row),
                  pl.BlockSpec((None, 3, d), lambda b, i: (b, 0, 0)),
                  pl.BlockSpec((A_HEADS, A_LATENT, A_VDIM), lambda b, i: (0, 0, 0)),
                  pl.BlockSpec((A_WIDTH, d), lambda b, i: (0, 0))],
        out_specs=pl.BlockSpec((Q_BLOCK, d), row),
        out_shape=jax.ShapeDtypeStruct(x2.shape, jnp.float32),
        scratch_shapes=[pltpu.VMEM((nkc, KEY_CHUNK, Q_BLOCK), jnp.int32),
                        pltpu.VMEM((nkc, KEY_CHUNK, Q_BLOCK), jnp.int16),
                        pltpu.VMEM((nkc, KEY_CHUNK, Q_BLOCK), jnp.int16),
                        pltpu.VMEM((nkc, KEY_CHUNK, Q_BLOCK), jnp.int16),
                        pltpu.VMEM((nkc, KEY_CHUNK, Q_BLOCK), jnp.float32),
                        pltpu.VMEM((slabs, KEY_CHUNK, slab_w), jnp.float32),
                        pltpu.VMEM((slabs, KEY_CHUNK, slab_w), jnp.float32),
                        pltpu.VMEM((slabs, 1, slab_w), jnp.float32),
                        pltpu.VMEM((slabs, 1, slab_w), jnp.float32),
                        pltpu.VMEM((slabs, 1, slab_w), jnp.float32),
                        pltpu.VMEM((slabs, V_ROWS, slab_w), jnp.float32),
                        pltpu.VMEM((Q_BLOCK, A_WIDTH), jnp.bfloat16)],
        compiler_params=pltpu.CompilerParams(vmem_limit_bytes=VMEM_LIMIT),
        name="attn_a",
    )(qcat, qidx, aux, keys.reshape(bsz, nkc, KEY_CHUNK, A_QPAD), vt, kdup.reshape(bsz, nkc, KEY_CHUNK, LANES),
      gate, x2, mod, w_uv, w_out)


_PERM_DILS = tuple(dil for _, dil in B_GROUPS if dil > 1)


def _dilation_perms(tm):
    dst = jnp.arange(tm)
    perms = []
    for dil in _PERM_DILS:
        src = (dst % (tm // dil)) * dil + dst // (tm // dil)
        perms.append((src[:, None] == dst[None, :]).astype(jnp.bfloat16))
    return jnp.stack(perms)


def _store_dilated(val, out_refs, perm_ref):
    vb = _bf16(val)
    tm, width = vb.shape
    for out_ref in out_refs:
        dil = out_ref.shape[1] // width
        if dil == 1:
            out_ref[...] = vb
            continue
        res = _bf16(_dot(perm_ref[_PERM_DILS.index(dil)], vb))
        rows = tm // dil
        for r in range(dil):
            out_ref[:, r * width:(r + 1) * width] = res[r * rows:(r + 1) * rows, :]


def _proj_b_kernel(*refs, with_kv):
    n_g = len(B_GROUPS)
    if with_kv:
        h_ref, mod_ref, g_ref, kvg_ref, wkv_ref, win_ref, cos_ref, sin_ref, perm_ref = refs[:9]
        outs = refs[9:]
        k_refs, v_refs, outs = outs[:n_g], outs[n_g:2 * n_g], outs[2 * n_g:]
    else:
        h_ref, mod_ref, g_ref, win_ref, cos_ref, sin_ref, perm_ref = refs[:7]
        outs = refs[7:]
    q_refs, gate_ref = outs[:n_g], outs[n_g]
    h = h_ref[...]
    cos, sin = cos_ref[...], sin_ref[...]
    width = B_WIDTH

    def rope(t, scale):
        return jnp.concatenate([_rope_lanes(t[:, hd * B_HDIM:(hd + 1) * B_HDIM], cos, sin, B_HDIM) * scale
                                for hd in range(B_HEADS)], axis=1)

    if with_kv:
        kvn = _bf16(h * lax.rsqrt(jnp.mean(h * h, axis=-1, keepdims=True) + EPS) * kvg_ref[...])
        _store_dilated(rope(_dot(kvn, wkv_ref[:, 0:width]), 1.0), k_refs, perm_ref)
        _store_dilated(_dot(kvn, wkv_ref[:, width:2 * width]), v_refs, perm_ref)
    hn = _bf16(_prenorm(h, g_ref[...], mod_ref))
    qscale = B_HDIM ** -0.5 * LOG2E
    for gi, q_ref in enumerate(q_refs):
        _store_dilated(rope(_dot(hn, win_ref[:, gi * width:(gi + 1) * width]), qscale), [q_ref], perm_ref)
    gate_ref[...] = _bf16(_silu(_dot(hn, win_ref[:, n_g * width:(n_g + 1) * width])))


def _proj_b(h2, mod, g, kvg, w_kv, w_in, cos, sin, bsz, seq, with_kv, tm=256):
    n, d = h2.shape
    tps = seq // tm
    row = lambda i: (i, 0)
    const = lambda i: (0, 0)
    in_specs = [pl.BlockSpec((tm, d), row), pl.BlockSpec((None, 3, d), lambda i: (i // tps, 0, 0)),
                pl.BlockSpec((1, d), const)]
    args = [h2, mod, g]
    if with_kv:
        in_specs += [pl.BlockSpec((1, d), const), pl.BlockSpec(w_kv.shape, const)]
        args += [kvg, w_kv]
    perms = _dilation_perms(tm)
    in_specs += [pl.BlockSpec(w_in.shape, const), pl.BlockSpec((tm, LANES), row), pl.BlockSpec((tm, LANES), row),
                 pl.BlockSpec(perms.shape, lambda i: (0, 0, 0))]
    args += [w_in, cos, sin, perms]
    dil_specs = [pl.BlockSpec((None, tm // dil, dil * B_WIDTH), lambda i: (i // tps, i % tps, 0)) for _, dil in B_GROUPS]
    dil_shapes = [jax.ShapeDtypeStruct((bsz, seq // dil, dil * B_WIDTH), jnp.bfloat16) for _, dil in B_GROUPS]
    n_sets = 3 if with_kv else 1
    return pl.pallas_call(
        functools.partial(_proj_b_kernel, with_kv=with_kv),
        grid=(n // tm,),
        in_specs=in_specs,
        out_specs=dil_specs * n_sets + [pl.BlockSpec((tm, B_WIDTH), row)],
        out_shape=dil_shapes * n_sets + [jax.ShapeDtypeStruct((n, B_WIDTH), jnp.bfloat16)],
        compiler_params=pltpu.CompilerParams(vmem_limit_bytes=VMEM_LIMIT),
        name="proj_b",
    )(*args)


def _dilated_kernel(q_ref, kc_ref, vc_ref, o_ref, lse_ref, kp_ref, vp_ref):
    j = pl.program_id(2)
    w = q_ref.shape[0]

    @pl.when(j == 0)
    def _():
        kp_ref[...] = jnp.zeros(kp_ref.shape, kp_ref.dtype)
        vp_ref[...] = jnp.zeros(vp_ref.shape, vp_ref.dtype)

    qi = lax.broadcasted_iota(jnp.int32, (w, w), 0)
    ki = lax.broadcasted_iota(jnp.int32, (w, w), 1)
    bias = jnp.concatenate([jnp.where(jnp.logical_and(ki >= qi, j > 0), 0.0, NEG),
                            jnp.where(ki <= qi, 0.0, NEG)], axis=1)
    ones = jnp.ones((2 * w, B_HDIM), jnp.bfloat16)
    lses = []
    for hd in range(B_HEADS):
        sl = slice(hd * B_HDIM, (hd + 1) * B_HDIM)
        keys = jnp.concatenate([kp_ref[:, sl], kc_ref[:, sl]], axis=0)
        vals = jnp.concatenate([vp_ref[:, sl], vc_ref[:, sl]], axis=0)
        s = _dot_nt(q_ref[:, sl], keys) + bias
        m = jnp.max(s, axis=-1, keepdims=True)
        p = _bf16(jnp.exp2(s - m))
        ol = _dot(p, jnp.concatenate([vals, ones], axis=1))
        l = ol[:, B_HDIM:]
        o_ref[:, sl] = _bf16(ol[:, :B_HDIM] / l)
        lses.append(m + jnp.log2(l[:, 0:1]))
    lse_ref[...] = jnp.concatenate(lses + [jnp.zeros((w, LANES - B_HEADS), jnp.float32)], axis=1)
    kp_ref[...] = kc_ref[...]
    vp_ref[...] = vc_ref[...]


def _dilated(q, k, v, window, dil):
    bsz, n, _ = q.shape
    w = window // dil
    nb = n // w
    cur = lambda b, r, j: (b, j, r)
    blk = pl.BlockSpec((None, w, B_WIDTH), cur)
    return pl.pallas_call(
        _dilated_kernel,
        grid=(bsz, dil, nb),
        in_specs=[blk, blk, blk],
        out_specs=[blk, pl.BlockSpec((None, None, w, LANES), lambda b, r, j: (b, r, j, 0))],
        out_shape=[jax.ShapeDtypeStruct(q.shape, jnp.bfloat16),
                   jax.ShapeDtypeStruct((bsz, dil, n, LANES), jnp.float32)],
        scratch_shapes=[pltpu.VMEM((w, B_WIDTH), jnp.bfloat16), pltpu.VMEM((w, B_WIDTH), jnp.bfloat16)],
        compiler_params=pltpu.CompilerParams(dimension_semantics=("arbitrary", "arbitrary", "arbitrary")),
        name=f"dilated_{dil}",
    )(q, k, v)


def _merge_kernel(*refs, final):
    n_g = len(B_GROUPS)
    o_refs, lse_refs = refs[:n_g], refs[n_g:2 * n_g]
    gate_ref, h_ref, mod_ref, wout_ref, fin_ref, unperm_ref, out_ref, ls_ref, og_ref = refs[2 * n_g:]
    tm = h_ref.shape[0]
    o_seq = []
    for gi, (_, dil) in enumerate(B_GROUPS):
        rows = tm // dil
        for r in range(dil):
            dst = pl.ds(r, rows, stride=dil) if dil > 1 else slice(None)
            ls_ref[gi, dst, :] = lse_refs[gi][r]
        if dil == 1:
            o_seq.append(o_refs[gi][...].astype(jnp.float32))
        else:
            stacked = jnp.concatenate([o_refs[gi][:, r * B_WIDTH:(r + 1) * B_WIDTH] for r in range(dil)], axis=0)
            o_seq.append(_dot(unperm_ref[_PERM_DILS.index(dil)], stacked))
    lses = [ls_ref[gi] for gi in range(n_g)]
    m = functools.reduce(jnp.maximum, lses)
    e = [jnp.exp2(t - m) for t in lses]
    inv = 1.0 / functools.reduce(lambda a, b: a + b, e)
    for hd in range(B_HEADS):
        sl = slice(hd * B_HDIM, (hd + 1) * B_HDIM)
        o = functools.reduce(lambda a, b: a + b,
                             [(e[gi] * inv)[:, hd:hd + 1] * o_seq[gi][:, sl] for gi in range(n_g)])
        og_ref[:, sl] = _bf16(o * gate_ref[:, sl])
    h = h_ref[...] + mod_ref[2:3, :] * _dot(og_ref[...], wout_ref[...])
    if final:
        h = h * lax.rsqrt(jnp.mean(h * h, axis=-1, keepdims=True) + EPS) * fin_ref[...]
    out_ref[...] = h


def _merge(os_, lses, gate, h2, mod, w_out, fin, seq, final, tm=256):
    n, d = h2.shape
    tps = seq // tm
    row = lambda i: (i, 0)
    rows = pl.BlockSpec((tm, d), row)
    n_g = len(B_GROUPS)
    o_specs = [pl.BlockSpec((None, tm // dil, dil * B_WIDTH), lambda i: (i // tps, i % tps, 0)) for _, dil in B_GROUPS]
    lse_specs = [pl.BlockSpec((None, dil, tm // dil, LANES), lambda i: (i // tps, 0, i % tps, 0)) for _, dil in B_GROUPS]
    unperms = jnp.swapaxes(_dilation_perms(tm), 1, 2)
    return pl.pallas_call(
        functools.partial(_merge_kernel, final=final),
        grid=(n // tm,),
        in_specs=o_specs + lse_specs + [rows, rows, pl.BlockSpec((None, 3, d), lambda i: (i // tps, 0, 0)),
                                        pl.BlockSpec(w_out.shape, lambda i: (0, 0)),
                                        pl.BlockSpec((1, d), lambda i: (0, 0)),
                                        pl.BlockSpec(unperms.shape, lambda i: (0, 0, 0))],
        out_specs=rows,
        out_shape=jax.ShapeDtypeStruct((n, d), jnp.float32),
        scratch_shapes=[pltpu.VMEM((n_g, tm, LANES), jnp.float32),
                        pltpu.VMEM((tm, B_WIDTH), jnp.bfloat16)],
        compiler_params=pltpu.CompilerParams(vmem_limit_bytes=VMEM_LIMIT),
        name="merge_out",
    )(*os_, *lses, gate, h2, mod, w_out, fin, unperms)


def _final_norm_kernel(h_ref, g_ref, out_ref):
    h = h_ref[...]
    out_ref[...] = h * lax.rsqrt(jnp.mean(h * h, axis=-1, keepdims=True) + EPS) * g_ref[...]


def kernel(x, c, positions, a_norm, a_ada_w, a_ada_b, a_w_in, a_kv_norm, a_w_uv, a_w_out, kv_norm, w_kv, b_norm, b_ada_w, b_ada_b, b_w_in, b_w_out, final_norm):
    bsz, seq, d = x.shape
    n = bsz * seq
    n_a, n_b = a_norm.shape[0], b_norm.shape[0]
    topk = min(TOPK_MAX, seq // 4)
    assert topk % Q_BLOCK == 0 and seq % KEY_CHUNK == 0 and d == B_WIDTH
    assert all(seq % window == 0 for window, _ in B_GROUPS)

    cos, sin = _rope_tables(positions, (A_ROPE, IDX_DIM, B_HDIM))
    h = x.reshape(n, d)
    for li in range(n_a):
        mod = _ada(c, a_ada_w[li], a_ada_b[li])
        qcat, keys, vt, kdup, qidx, aux, gate = _proj_a(
            h, mod, a_norm[li].reshape(1, d), a_kv_norm[li].reshape(1, A_LATENT), _a_weight(a_w_in[li]),
            cos[0:2], sin[0:2], bsz, seq)
        h = _attn_a(qcat, qidx, aux, keys, vt, kdup, gate, h, mod, _bf16(a_w_uv[li]), _bf16(a_w_out[li]), topk)
    k_sh = v_sh = None
    for li in range(n_b):
        mod = _ada(c, b_ada_w[li], b_ada_b[li])
        outs = _proj_b(h, mod, b_norm[li].reshape(1, d), kv_norm.reshape(1, d), _bf16(w_kv), _bf16(b_w_in[li]),
                       cos[2], sin[2], bsz, seq, with_kv=(li == 0))
        n_g = len(B_GROUPS)
        if li == 0:
            k_sh, v_sh = outs[:n_g], outs[n_g:2 * n_g]
            outs = outs[2 * n_g:]
        q_groups, gate = outs[:n_g], outs[n_g]
        res = [_dilated(qg, kg, vg, window, dil)
               for qg, kg, vg, (window, dil) in zip(q_groups, k_sh, v_sh, B_GROUPS)]
        h = _merge([r[0] for r in res], [r[1] for r in res], gate, h, mod, _bf16(b_w_out[li]),
                   final_norm.reshape(1, d), seq, final=(li == n_b - 1))
    if n_b == 0:
        h = pl.pallas_call(
            _final_norm_kernel, grid=(n // 256,),
            in_specs=[pl.BlockSpec((256, d), lambda i: (i, 0)), pl.BlockSpec((1, d), lambda i: (0, 0))],
            out_specs=pl.BlockSpec((256, d), lambda i: (i, 0)),
            out_shape=jax.ShapeDtypeStruct((n, d), jnp.float32), name="final_norm",
        )(h, final_norm.reshape(1, d))
    return h.reshape(bsz, seq, d)
```

```python
import functools
import math

import jax
import jax.numpy as jnp
from jax import lax
from jax.experimental import pallas as pl
from jax.experimental.pallas import tpu as pltpu

ROPE_THETA = 10000.0
EPS = 1e-6
LOG2E = 1.4426950408889634

A_HEADS = 16
A_LATENT = 128
A_ROPE = 32
A_VDIM = 128
A_WIDTH = A_HEADS * A_VDIM
IDX_HEADS = 8
IDX_DIM = 64
TOPK_MAX = 256
Q_BLOCK = 128
KEY_CHUNK = 256
A_QPAD = 256
HEADS_PER_SLAB = 2
V_ROWS = A_LATENT + 16

B_HEADS = 8
B_HDIM = 128
B_GROUPS = ((128, 1), (512, 4), (2048, 16))
B_WIDTH = B_HEADS * B_HDIM
ROPE_DIMS = (A_ROPE, IDX_DIM, B_HDIM)

LANES = 128
NEG = -1e30
INT_MIN = -(2 ** 31)
HALF = 2 ** 15
VMEM_LIMIT = 56 * 1024 * 1024

AUX_W_OFF = A_ROPE


def _bf16(t):
    return t.astype(jnp.bfloat16)


def _dot(a, b):
    return jnp.dot(a, b, preferred_element_type=jnp.float32)


def _dot_nt(a, b):
    return lax.dot_general(a, b, (((1,), (1,)), ((), ())), preferred_element_type=jnp.float32)


def _silu(t):
    return t * (1.0 / (1.0 + jnp.exp(-t)))


def _lane_iota(shape):
    return lax.broadcasted_iota(jnp.int32, shape, len(shape) - 1)


def _ada_kernel(c_ref, w_ref, b_ref, out_ref):
    out_ref[...] = _dot(_bf16(_silu(c_ref[...])), _bf16(w_ref[...])) + b_ref[...]


def _ada(c, w, b):
    bsz, d = c.shape
    out = pl.pallas_call(
        _ada_kernel,
        grid=(3,),
        in_specs=[pl.BlockSpec((bsz, d), lambda j: (0, 0)),
                  pl.BlockSpec((d, d), lambda j: (0, j)),
                  pl.BlockSpec((1, d), lambda j: (0, j))],
        out_specs=pl.BlockSpec((bsz, d), lambda j: (0, j)),
        out_shape=jax.ShapeDtypeStruct((bsz, 3 * d), jnp.float32),
        name="ada_mod",
    )(c, w, b.reshape(1, 3 * d))
    return out.reshape(bsz, 3, d)


def _rope_table_kernel(pos_ref, inv_ref, sel_ref, sign_ref, cos_ref, sin_ref):
    ang = pos_ref[...] * inv_ref[...]
    cos, sin = jnp.cos(ang), jnp.sin(ang)
    for t in range(cos_ref.shape[0]):
        pick = lambda v: jnp.dot(v, sel_ref[t], precision=lax.Precision.HIGHEST, preferred_element_type=jnp.float32)
        cos_ref[t] = pick(cos)
        sin_ref[t] = pick(sin) * sign_ref[t]


def _rope_tables(positions, dims, tm=512):
    n = positions.size
    base = max(dims)
    assert base == LANES and all(base % d == 0 and d & (d - 1) == 0 for d in dims)
    posf = jnp.broadcast_to(positions.reshape(n, 1).astype(jnp.float32), (n, LANES))
    inv = ROPE_THETA ** (-jnp.arange(0, base, 2, dtype=jnp.float32) / base)
    lane = jnp.arange(LANES)
    inv = inv[lane % (base // 2)].reshape(1, LANES)
    src = [((lane % d) % (d // 2)) * (base // d) for d in dims]
    sel = jnp.stack([(lane[:, None] == s[None, :]).astype(jnp.float32) for s in src])
    sign = jnp.stack([jnp.where(lane % d < d // 2, -1.0, 1.0) for d in dims]).astype(jnp.float32)
    sign = sign.reshape(len(dims), 1, LANES)
    shape = jax.ShapeDtypeStruct((len(dims), n, LANES), jnp.float32)
    return pl.pallas_call(
        _rope_table_kernel,
        grid=(n // tm,),
        in_specs=[pl.BlockSpec((tm, LANES), lambda i: (i, 0)),
                  pl.BlockSpec((1, LANES), lambda i: (0, 0)),
                  pl.BlockSpec(sel.shape, lambda i: (0, 0, 0)),
                  pl.BlockSpec(sign.shape, lambda i: (0, 0, 0))],
        out_specs=[pl.BlockSpec((len(dims), tm, LANES), lambda i: (0, i, 0))] * 2,
        out_shape=[shape, shape],
        name="rope_tables",
    )(posf, inv, sel, sign)


def _rope_lanes(xv, cos, sin_signed, dim):
    half = dim // 2
    if dim == LANES:
        partner = pltpu.roll(xv, half, axis=1)
    else:
        first = (_lane_iota(xv.shape) % dim) < half
        partner = jnp.where(first, pltpu.roll(xv, LANES - half, axis=1), pltpu.roll(xv, half, axis=1))
    return xv * cos + partner * sin_signed


def _prenorm(h, g, mod_ref):
    y = h * lax.rsqrt(jnp.mean(h * h, axis=-1, keepdims=True) + EPS) * g
    return y * (1.0 + mod_ref[1:2, :]) + mod_ref[0:1, :]


_A_COLS = {}
_o = 0
for _name, _w in (("q_lat", A_HEADS * A_LATENT), ("q_rope", A_HEADS * A_ROPE), ("c_kv", A_LATENT),
                  ("aux", LANES), ("kdup", LANES), ("gate", A_WIDTH), ("q_idx", IDX_HEADS * IDX_DIM)):
    _A_COLS[_name] = (_o, _o + _w)
    _o += _w
A_COLS_TOTAL = _o


def _a_weight(w_in):
    sizes = (A_HEADS * A_LATENT, A_HEADS * A_ROPE, A_LATENT, A_ROPE, A_WIDTH, IDX_HEADS * IDX_DIM, IDX_DIM, IDX_HEADS)
    parts, o = [], 0
    for s in sizes:
        parts.append(w_in[:, o:o + s])
        o += s
    q_lat, q_rope, c_kv, k_rope, gate, q_idx, k_idx, w_idx = parts
    d = w_in.shape[0]
    aux = jnp.concatenate([k_rope, w_idx, jnp.zeros((d, LANES - A_ROPE - IDX_HEADS), w_in.dtype)], axis=1)
    kdup = jnp.concatenate([k_idx, k_idx], axis=1)
    return _bf16(jnp.concatenate([q_lat, q_rope, c_kv, aux, kdup, gate, q_idx], axis=1))


def _proj_a_kernel(x_ref, mod_ref, g_ref, kvg_ref, w_ref, cos_ref, sin_ref,
                   qcat_ref, keys_ref, vt_ref, kdup_ref, qidx_ref, aux_ref, gate_ref):
    hn = _bf16(_prenorm(x_ref[...], g_ref[...], mod_ref))
    cos32, sin32 = cos_ref[0], sin_ref[0]
    cos64, sin64 = cos_ref[1], sin_ref[1]
    lane = _lane_iota(cos32.shape)
    qscale = (A_LATENT + A_ROPE) ** -0.5 * LOG2E

    def proj(name):
        lo, hi = _A_COLS[name]
        return _dot(hn, w_ref[:, lo:hi])

    q_blocks = x_ref.shape[0] // Q_BLOCK

    def put_q(h, row0, piece_t):
        slab, col0 = h // HEADS_PER_SLAB, (h % HEADS_PER_SLAB) * Q_BLOCK
        for qb in range(q_blocks):
            qcat_ref[qb, slab, row0:row0 + piece_t.shape[0], col0:col0 + Q_BLOCK] = _bf16(
                piece_t[:, qb * Q_BLOCK:(qb + 1) * Q_BLOCK])

    q_lat = proj("q_lat") * qscale
    q_rope = proj("q_rope")
    per_slab = LANES // A_ROPE
    pad = jnp.zeros((A_QPAD - A_LATENT - A_ROPE, x_ref.shape[0]), jnp.float32)
    for h in range(A_HEADS):
        put_q(h, 0, q_lat[:, h * A_LATENT:(h + 1) * A_LATENT].T)
    for j in range(A_HEADS // per_slab):
        roped_t = (_rope_lanes(q_rope[:, j * LANES:(j + 1) * LANES], cos32, sin32, A_ROPE) * qscale).T
        for t in range(per_slab):
            put_q(j * per_slab + t, A_LATENT, roped_t[t * A_ROPE:(t + 1) * A_ROPE, :])
            put_q(j * per_slab + t, A_LATENT + A_ROPE, pad)

    c_kv = proj("c_kv")
    c_kv = c_kv * lax.rsqrt(jnp.mean(c_kv * c_kv, axis=-1, keepdims=True) + EPS) * kvg_ref[...]
    keys_ref[:, 0:A_LATENT] = _bf16(c_kv)
    vt_ref[0:A_LATENT, :] = _bf16(c_kv.T)
    vt_ref[A_LATENT:V_ROWS, :] = jnp.ones((V_ROWS - A_LATENT, vt_ref.shape[1]), jnp.bfloat16)
    aux = proj("aux")
    k_rope = _rope_lanes(aux, cos32, sin32, A_ROPE)
    keys_ref[:, A_LATENT:A_QPAD] = _bf16(jnp.where(lane < A_ROPE, k_rope, 0.0))
    aux_ref[...] = aux * IDX_HEADS ** -0.5

    kdup_ref[...] = _bf16(_rope_lanes(proj("kdup"), cos64, sin64, IDX_DIM))
    q_idx = proj("q_idx")
    per_slab_i = LANES // IDX_DIM
    feat = lax.broadcasted_iota(jnp.int32, (LANES, x_ref.shape[0]), 0)
    for j in range(IDX_HEADS // per_slab_i):
        roped_t = _rope_lanes(q_idx[:, j * LANES:(j + 1) * LANES], cos64, sin64, IDX_DIM).T
        for t in range(per_slab_i):
            keep = (feat >= t * IDX_DIM) & (feat < (t + 1) * IDX_DIM)
            head_t = _bf16(jnp.where(keep, roped_t, 0.0))
            h = j * per_slab_i + t
            for qb in range(q_blocks):
                qidx_ref[qb, :, h * Q_BLOCK:(h + 1) * Q_BLOCK] = head_t[:, qb * Q_BLOCK:(qb + 1) * Q_BLOCK]

    gate_ref[...] = _bf16(_silu(proj("gate")))


def _proj_a(x2, mod, g, kvg, w, cos, sin, bsz, seq):
    n, d = x2.shape
    tm = KEY_CHUNK
    tps = seq // tm
    row = lambda i: (i, 0)
    qbt = tm // Q_BLOCK
    nq = seq // Q_BLOCK
    slabs, slab_w = A_HEADS // HEADS_PER_SLAB, HEADS_PER_SLAB * Q_BLOCK
    return pl.pallas_call(
        _proj_a_kernel,
        grid=(n // tm,),
        in_specs=[pl.BlockSpec((tm, d), row),
                  pl.BlockSpec((None, 3, d), lambda i: (i // tps, 0, 0)),
                  pl.BlockSpec((1, d), lambda i: (0, 0)),
                  pl.BlockSpec((1, A_LATENT), lambda i: (0, 0)),
                  pl.BlockSpec((d, A_COLS_TOTAL), lambda i: (0, 0)),
                  pl.BlockSpec((2, tm, LANES), lambda i: (0, i, 0)),
                  pl.BlockSpec((2, tm, LANES), lambda i: (0, i, 0))],
        out_specs=[pl.BlockSpec((None, qbt, slabs, A_QPAD, slab_w), lambda i: (i // tps, i % tps, 0, 0, 0)),
                   pl.BlockSpec((tm, A_QPAD), row),
                   pl.BlockSpec((None, None, V_ROWS, tm), lambda i: (i // tps, i % tps, 0, 0)),
                   pl.BlockSpec((tm, LANES), row),
                   pl.BlockSpec((None, qbt, LANES, IDX_HEADS * Q_BLOCK), lambda i: (i // tps, i % tps, 0, 0)),
                   pl.BlockSpec((tm, LANES), row),
                   pl.BlockSpec((tm, A_WIDTH), row)],
        out_shape=[jax.ShapeDtypeStruct((bsz, nq, slabs, A_QPAD, slab_w), jnp.bfloat16),
                   jax.ShapeDtypeStruct((n, A_QPAD), jnp.bfloat16),
                   jax.ShapeDtypeStruct((bsz, seq // tm, V_ROWS, tm), jnp.bfloat16),
                   jax.ShapeDtypeStruct((n, LANES), jnp.bfloat16),
                   jax.ShapeDtypeStruct((bsz, nq, LANES, IDX_HEADS * Q_BLOCK), jnp.bfloat16),
                   jax.ShapeDtypeStruct((n, LANES), jnp.float32),
                   jax.ShapeDtypeStruct((n, A_WIDTH), jnp.bfloat16)],
        compiler_params=pltpu.CompilerParams(vmem_limit_bytes=VMEM_LIMIT),
        name="proj_a",
    )(x2, mod, g, kvg, w, cos, sin)


def _attn_a_kernel(qcat_ref, qidx_ref, aux_ref, keys_ref, vt_ref, kdup_ref, gate_ref, x_ref, mod_ref, wuv_ref,
                   wout_ref, out_ref, sc_ref, hi_ref, lo_ref, lo2_ref, bias_ref, sa_ref, sb_ref, mxa_ref, mxb_ref, m_ref, acc_ref, og_ref, *, topk):
    i = pl.program_id(1)
    n_chunks = (i + 2) // 2
    seq = keys_ref.shape[0] * KEY_CHUNK

    w_t = aux_ref[...].T
    qi_all = qidx_ref[...]
    qpos = i * Q_BLOCK + lax.broadcasted_iota(jnp.int32, (KEY_CHUNK, Q_BLOCK), 1)
    krow = lax.broadcasted_iota(jnp.int32, (KEY_CHUNK, Q_BLOCK), 0)

    n_pairs = n_chunks // 2
    odd = n_chunks % 2 == 1

    def for_chunk_pairs(fn, init):
        def body(t, carry):
            return fn(2 * t + 1, fn(2 * t, carry))
        carry = lax.fori_loop(0, n_pairs, body, init)
        return lax.cond(odd, lambda cr: fn(n_chunks - 1, cr), lambda cr: cr, carry)

    def score_chunk(c, carry):
        rel = _dot(kdup_ref[c], qi_all)
        acc = jnp.zeros((KEY_CHUNK, Q_BLOCK), jnp.float32)
        for h in range(IDX_HEADS):
            acc = acc + w_t[AUX_W_OFF + h:AUX_W_OFF + h + 1, :] * jnp.maximum(rel[:, h * Q_BLOCK:(h + 1) * Q_BLOCK], 0.0)
        bits = lax.bitcast_convert_type(acc, jnp.int32)
        key = bits ^ ((bits >> 31) & 0x7FFFFFFF)
        key = jnp.where(c * KEY_CHUNK + krow <= qpos, key, INT_MIN)
        sc_ref[c] = key
        hi_ref[c] = (key >> 16).astype(jnp.int16)
        lo_ref[c] = ((key & 0xFFFF) - HALF).astype(jnp.int16)
        return carry

    for_chunk_pairs(score_chunk, 0)

    def count(pred):
        def body(c, acc):
            hit = jnp.where(pred(sc_ref[c], c), 1, 0)
            return acc + jnp.sum(hit.reshape(KEY_CHUNK // 32, 32, Q_BLOCK), axis=0)
        acc = for_chunk_pairs(body, jnp.zeros((32, Q_BLOCK), jnp.int32))
        return jnp.sum(acc, axis=0, keepdims=True)

    def count16(ref16, pred):
        def body(c, acc):
            hit = jnp.where(pred(ref16[c]), jnp.int16(1), jnp.int16(0))
            words = pltpu.bitcast(hit, jnp.int32)
            return acc + jnp.sum(words.reshape(KEY_CHUNK // 64, 32, Q_BLOCK), axis=0)
        acc = for_chunk_pairs(body, jnp.zeros((32, Q_BLOCK), jnp.int32))
        acc = jnp.sum(acc, axis=0, keepdims=True)
        return (acc & 0xFFFF) + (acc >> 16)

    def bisect16(ref16, target, n_bits):
        def bit(b, t):
            cand = t + lax.shift_left(jnp.int32(1), 15 - b)
            cand16 = cand.astype(jnp.int16)
            return jnp.where(count16(ref16, lambda v: v >= cand16) >= target, cand, t)
        return lax.fori_loop(0, n_bits, bit, jnp.full((1, Q_BLOCK), -HALF, jnp.int32))

    all_selected = (i + 1) * Q_BLOCK <= topk
    n_bits = jnp.where(all_selected, 0, 16)
    t_hi = bisect16(hi_ref, topk, n_bits)
    t_hi16 = t_hi.astype(jnp.int16)
    above = count16(hi_ref, lambda v: v > t_hi16)

    def low_chunk(c, carry):
        lo2_ref[c] = jnp.where(hi_ref[c] == t_hi16, lo_ref[c], jnp.int16(-HALF))
        return carry

    for_chunk_pairs(low_chunk, 0)
    t_lo = bisect16(lo2_ref, topk - above, n_bits)
    thr = lax.shift_left(t_hi, 16) + (t_lo + HALF)

    need = topk - count(lambda k, c: k > thr)
    n_eq = count(lambda k, c: k == thr)
    excess = jnp.max(jnp.where(n_eq > need, 1, 0)) > 0
    idx_bits = (seq - 1).bit_length()

    def tie_bound():
        def bit(b, bound):
            cand = bound + lax.shift_left(jnp.int32(1), idx_bits - 1 - b)
            below = count(lambda k, c: jnp.where(k == thr, c * KEY_CHUNK + krow, seq) < cand)
            return jnp.where(below < need, cand, bound)
        return lax.fori_loop(0, idx_bits, bit, jnp.zeros((1, Q_BLOCK), jnp.int32))

    tie_idx = lax.cond(jnp.logical_and(excess, jnp.logical_not(all_selected)), tie_bound,
                       lambda: jnp.where(all_selected, -1, seq) + jnp.zeros((1, Q_BLOCK), jnp.int32))

    def bias_chunk(c, carry):
        k = sc_ref[c]
        tie_ok = jnp.where(c * KEY_CHUNK + krow <= tie_idx, 0.0, NEG)
        bias_ref[c] = jnp.where(k > thr, 0.0, jnp.where(k == thr, tie_ok, NEG))
        return carry

    for_chunk_pairs(bias_chunk, 0)

    m_ref[...] = jnp.full(m_ref.shape, NEG, jnp.float32)
    acc_ref[...] = jnp.zeros(acc_ref.shape, jnp.float32)

    slabs = A_HEADS // HEADS_PER_SLAB
    last_chunk = n_chunks - 1

    def scores(c, s_buf, mx_buf):
        c = jnp.minimum(c, last_chunk)
        kc = keys_ref[c]
        bias2 = jnp.concatenate([bias_ref[c]] * HEADS_PER_SLAB, axis=1)
        for pr in range(slabs):
            s = _dot(kc, qcat_ref[pr]) + bias2
            s_buf[pr] = s
            mx_buf[pr] = jnp.max(s, axis=0, keepdims=True)

    def accumulate(c, s_buf, mx_buf):
        vt = vt_ref[c]
        for pr in range(slabs):
            m_old = m_ref[pr]
            m_new = jnp.maximum(m_old, mx_buf[pr])
            alpha = jnp.exp2(m_old - m_new)
            p = jnp.exp2(s_buf[pr] - m_new)
            m_ref[pr] = m_new
            acc_ref[pr] = acc_ref[pr] * alpha + _dot(vt, _bf16(p))

    scores(0, sa_ref, mxa_ref)

    def attend_pair(t, carry):
        scores(2 * t + 1, sb_ref, mxb_ref)
        accumulate(2 * t, sa_ref, mxa_ref)
        scores(2 * t + 2, sa_ref, mxa_ref)
        accumulate(2 * t + 1, sb_ref, mxb_ref)
        return carry

    lax.fori_loop(0, n_pairs, attend_pair, 0)

    @pl.when(odd)
    def _():
        accumulate(n_chunks - 1, sa_ref, mxa_ref)

    for pr in range(slabs):
        o_t = acc_ref[pr, 0:A_LATENT, :] / acc_ref[pr, A_LATENT:A_LATENT + 1, :]
        for t in range(HEADS_PER_SLAB):
            h = pr * HEADS_PER_SLAB + t
            o_lat = _bf16(o_t[:, t * Q_BLOCK:(t + 1) * Q_BLOCK].T)
            o = _dot(o_lat, wuv_ref[h]) * gate_ref[:, h * A_VDIM:(h + 1) * A_VDIM]
            og_ref[:, h * A_VDIM:(h + 1) * A_VDIM] = _bf16(o)
    out_ref[...] = x_ref[...] + mod_ref[2:3, :] * _dot(og_ref[...], wout_ref[...])


def _attn_a(qcat, qidx, aux, keys, vt, kdup, gate, x2, mod, w_uv, w_out, topk):
    bsz, nq = qcat.shape[:2]
    seq = nq * Q_BLOCK
    d = x2.shape[1]
    nkc = seq // KEY_CHUNK
    row = lambda b, i: (b * nq + i, 0)
    slabs = A_HEADS // HEADS_PER_SLAB
    slab_w = HEADS_PER_SLAB * Q_BLOCK
    return pl.pallas_call(
        functools.partial(_attn_a_kernel, topk=topk),
        grid=(bsz, nq),
        in_specs=[pl.BlockSpec((None, None, slabs, A_QPAD, slab_w), lambda b, i: (b, i, 0, 0, 0)),
                  pl.BlockSpec((None, None, LANES, IDX_HEADS * Q_BLOCK), lambda b, i: (b, i, 0, 0)),
                  pl.BlockSpec((Q_BLOCK, LANES), row),
                  pl.BlockSpec((None, nkc, KEY_CHUNK, A_QPAD), lambda b, i: (b, 0, 0, 0)),
                  pl.BlockSpec((None, nkc, V_ROWS, KEY_CHUNK), lambda b, i: (b, 0, 0, 0)),
                  pl.BlockSpec((None, nkc, KEY_CHUNK, LANES), lambda b, i: (b, 0, 0, 0)),
                  pl.BlockSpec((Q_BLOCK, A_WIDTH), row),
                  pl.BlockSpec((Q_BLOCK, d), row),
                  pl.BlockSpec((None, 3, d), lambda b, i: (b, 0, 0)),
                  pl.BlockSpec((A_HEADS, A_LATENT, A_VDIM), lambda b, i: (0, 0, 0)),
                  pl.BlockSpec((A_WIDTH, d), lambda b, i: (0, 0))],
        out_specs=pl.BlockSpec((Q_BLOCK, d), row),
        out_shape=jax.ShapeDtypeStruct(x2.shape, jnp.float32),
        scratch_shapes=[pltpu.VMEM((nkc, KEY_CHUNK, Q_BLOCK), jnp.int32),
                        pltpu.VMEM((nkc, KEY_CHUNK, Q_BLOCK), jnp.int16),
                        pltpu.VMEM((nkc, KEY_CHUNK, Q_BLOCK), jnp.int16),
                        pltpu.VMEM((nkc, KEY_CHUNK, Q_BLOCK), jnp.int16),
                        pltpu.VMEM((nkc, KEY_CHUNK, Q_BLOCK), jnp.float32),
                        pltpu.VMEM((slabs, KEY_CHUNK, slab_w), jnp.float32),
                        pltpu.VMEM((slabs, KEY_CHUNK, slab_w), jnp.float32),
                        pltpu.VMEM((slabs, 1, slab_w), jnp.float32),
                        pltpu.VMEM((slabs, 1, slab_w), jnp.float32),
                        pltpu.VMEM((slabs, 1, slab_w), jnp.float32),
                        pltpu.VMEM((slabs, V_ROWS, slab_w), jnp.float32),
                        pltpu.VMEM((Q_BLOCK, A_WIDTH), jnp.bfloat16)],
        compiler_params=pltpu.CompilerParams(vmem_limit_bytes=VMEM_LIMIT),
        name="attn_a",
    )(qcat, qidx, aux, keys.reshape(bsz, nkc, KEY_CHUNK, A_QPAD), vt, kdup.reshape(bsz, nkc, KEY_CHUNK, LANES),
      gate, x2, mod, w_uv, w_out)


_PERM_DILS = tuple(dil for _, dil in B_GROUPS if dil > 1)


def _dilation_perms(tm):
    dst = jnp.arange(tm)
    perms = []
    for dil in _PERM_DILS:
        src = (dst % (tm // dil)) * dil + dst // (tm // dil)
        perms.append((src[:, None] == dst[None, :]).astype(jnp.bfloat16))
    return jnp.stack(perms)


def _store_dilated(val, out_refs, perm_ref):
    vb = _bf16(val)
    tm, width = vb.shape
    for out_ref in out_refs:
        dil = out_ref.shape[1] // width
        if dil == 1:
            out_ref[...] = vb
            continue
        res = _bf16(_dot(perm_ref[_PERM_DILS.index(dil)], vb))
        rows = tm // dil
        for r in range(dil):
            out_ref[:, r * width:(r + 1) * width] = res[r * rows:(r + 1) * rows, :]


def _proj_b_kernel(*refs, with_kv):
    n_g = len(B_GROUPS)
    if with_kv:
        h_ref, mod_ref, g_ref, kvg_ref, wkv_ref, win_ref, cos_ref, sin_ref, perm_ref = refs[:9]
        outs = refs[9:]
        k_refs, v_refs, outs = outs[:n_g], outs[n_g:2 * n_g], outs[2 * n_g:]
    else:
        h_ref, mod_ref, g_ref, win_ref, cos_ref, sin_ref, perm_ref = refs[:7]
        outs = refs[7:]
    q_refs, gate_ref = outs[:n_g], outs[n_g]
    h = h_ref[...]
    cos, sin = cos_ref[...], sin_ref[...]
    width = B_WIDTH

    def rope(t, scale):
        return jnp.concatenate([_rope_lanes(t[:, hd * B_HDIM:(hd + 1) * B_HDIM], cos, sin, B_HDIM) * scale
                                for hd in range(B_HEADS)], axis=1)

    if with_kv:
        kvn = _bf16(h * lax.rsqrt(jnp.mean(h * h, axis=-1, keepdims=True) + EPS) * kvg_ref[...])
        _store_dilated(rope(_dot(kvn, wkv_ref[:, 0:width]), 1.0), k_refs, perm_ref)
        _store_dilated(_dot(kvn, wkv_ref[:, width:2 * width]), v_refs, perm_ref)
    hn = _bf16(_prenorm(h, g_ref[...], mod_ref))
    qscale = B_HDIM ** -0.5 * LOG2E
    for gi, q_ref in enumerate(q_refs):
        _store_dilated(rope(_dot(hn, win_ref[:, gi * width:(gi + 1) * width]), qscale), [q_ref], perm_ref)
    gate_ref[...] = _bf16(_silu(_dot(hn, win_ref[:, n_g * width:(n_g + 1) * width])))


def _proj_b(h2, mod, g, kvg, w_kv, w_in, cos, sin, bsz, seq, with_kv, tm=256):
    n, d = h2.shape
    tps = seq // tm
    row = lambda i: (i, 0)
    const = lambda i: (0, 0)
    in_specs = [pl.BlockSpec((tm, d), row), pl.BlockSpec((None, 3, d), lambda i: (i // tps, 0, 0)),
                pl.BlockSpec((1, d), const)]
    args = [h2, mod, g]
    if with_kv:
        in_specs += [pl.BlockSpec((1, d), const), pl.BlockSpec(w_kv.shape, const)]
        args += [kvg, w_kv]
    perms = _dilation_perms(tm)
    table = pl.BlockSpec((None, tm, LANES), lambda i: (ROPE_DIMS.index(B_HDIM), i, 0))
    in_specs += [pl.BlockSpec(w_in.shape, const), table, table,
                 pl.BlockSpec(perms.shape, lambda i: (0, 0, 0))]
    args += [w_in, cos, sin, perms]
    dil_specs = [pl.BlockSpec((None, tm // dil, dil * B_WIDTH), lambda i: (i // tps, i % tps, 0)) for _, dil in B_GROUPS]
    dil_shapes = [jax.ShapeDtypeStruct((bsz, seq // dil, dil * B_WIDTH), jnp.bfloat16) for _, dil in B_GROUPS]
    n_sets = 3 if with_kv else 1
    return pl.pallas_call(
        functools.partial(_proj_b_kernel, with_kv=with_kv),
        grid=(n // tm,),
        in_specs=in_specs,
        out_specs=dil_specs * n_sets + [pl.BlockSpec((tm, B_WIDTH), row)],
        out_shape=dil_shapes * n_sets + [jax.ShapeDtypeStruct((n, B_WIDTH), jnp.bfloat16)],
        compiler_params=pltpu.CompilerParams(vmem_limit_bytes=VMEM_LIMIT),
        name="proj_b",
    )(*args)


def _dilated_kernel(q_ref, kc_ref, vc_ref, o_ref, lse_ref, kp_ref, vp_ref):
    j = pl.program_id(2)
    w = q_ref.shape[0]

    @pl.when(j == 0)
    def _():
        kp_ref[...] = jnp.zeros(kp_ref.shape, kp_ref.dtype)
        vp_ref[...] = jnp.zeros(vp_ref.shape, vp_ref.dtype)

    qi = lax.broadcasted_iota(jnp.int32, (w, w), 0)
    ki = lax.broadcasted_iota(jnp.int32, (w, w), 1)
    bias = jnp.concatenate([jnp.where(jnp.logical_and(ki >= qi, j > 0), 0.0, NEG),
                            jnp.where(ki <= qi, 0.0, NEG)], axis=1)
    ones = jnp.ones((2 * w, B_HDIM), jnp.bfloat16)
    lses = []
    for hd in range(B_HEADS):
        sl = slice(hd * B_HDIM, (hd + 1) * B_HDIM)
        keys = jnp.concatenate([kp_ref[:, sl], kc_ref[:, sl]], axis=0)
        vals = jnp.concatenate([vp_ref[:, sl], vc_ref[:, sl]], axis=0)
        s = _dot_nt(q_ref[:, sl], keys) + bias
        m = jnp.max(s, axis=-1, keepdims=True)
        p = _bf16(jnp.exp2(s - m))
        ol = _dot(p, jnp.concatenate([vals, ones], axis=1))
        l = ol[:, B_HDIM:]
        o_ref[:, sl] = _bf16(ol[:, :B_HDIM] / l)
        lses.append(m + jnp.log2(l[:, 0:1]))
    lse_ref[...] = jnp.concatenate(lses + [jnp.zeros((w, LANES - B_HEADS), jnp.float32)], axis=1)
    kp_ref[...] = kc_ref[...]
    vp_ref[...] = vc_ref[...]


def _dilated(q, k, v, window, dil):
    bsz, n, _ = q.shape
    w = window // dil
    nb = n // w
    cur = lambda b, r, j: (b, j, r)
    blk = pl.BlockSpec((None, w, B_WIDTH), cur)
    return pl.pallas_call(
        _dilated_kernel,
        grid=(bsz, dil, nb),
        in_specs=[blk, blk, blk],
        out_specs=[blk, pl.BlockSpec((None, None, w, LANES), lambda b, r, j: (b, r, j, 0))],
        out_shape=[jax.ShapeDtypeStruct(q.shape, jnp.bfloat16),
                   jax.ShapeDtypeStruct((bsz, dil, n, LANES), jnp.float32)],
        scratch_shapes=[pltpu.VMEM((w, B_WIDTH), jnp.bfloat16), pltpu.VMEM((w, B_WIDTH), jnp.bfloat16)],
        compiler_params=pltpu.CompilerParams(dimension_semantics=("arbitrary", "arbitrary", "arbitrary")),
        name=f"dilated_{dil}",
    )(q, k, v)


def _merge_kernel(*refs, final):
    n_g = len(B_GROUPS)
    o_refs, lse_refs = refs[:n_g], refs[n_g:2 * n_g]
    gate_ref, h_ref, mod_ref, wout_ref, fin_ref, unperm_ref, out_ref, ls_ref, og_ref = refs[2 * n_g:]
    tm = h_ref.shape[0]
    o_seq = []
    for gi, (_, dil) in enumerate(B_GROUPS):
        rows = tm // dil
        for r in range(dil):
            dst = pl.ds(r, rows, stride=dil) if dil > 1 else slice(None)
            ls_ref[gi, dst, :] = lse_refs[gi][r]
        if dil == 1:
            o_seq.append(o_refs[gi][...].astype(jnp.float32))
        else:
            stacked = jnp.concatenate([o_refs[gi][:, r * B_WIDTH:(r + 1) * B_WIDTH] for r in range(dil)], axis=0)
            o_seq.append(_dot(unperm_ref[_PERM_DILS.index(dil)], stacked))
    lses = [ls_ref[gi] for gi in range(n_g)]
    m = functools.reduce(jnp.maximum, lses)
    e = [jnp.exp2(t - m) for t in lses]
    inv = 1.0 / functools.reduce(lambda a, b: a + b, e)
    for hd in range(B_HEADS):
        sl = slice(hd * B_HDIM, (hd + 1) * B_HDIM)
        o = functools.reduce(lambda a, b: a + b,
                             [(e[gi] * inv)[:, hd:hd + 1] * o_seq[gi][:, sl] for gi in range(n_g)])
        og_ref[:, sl] = _bf16(o * gate_ref[:, sl])
    h = h_ref[...] + mod_ref[2:3, :] * _dot(og_ref[...], wout_ref[...])
    if final:
        h = h * lax.rsqrt(jnp.mean(h * h, axis=-1, keepdims=True) + EPS) * fin_ref[...]
    out_ref[...] = h


def _merge(os_, lses, gate, h2, mod, w_out, fin, seq, final, tm=256):
    n, d = h2.shape
    tps = seq // tm
    row = lambda i: (i, 0)
    rows = pl.BlockSpec((tm, d), row)
    n_g = len(B_GROUPS)
    o_specs = [pl.BlockSpec((None, tm // dil, dil * B_WIDTH), lambda i: (i // tps, i % tps, 0)) for _, dil in B_GROUPS]
    lse_specs = [pl.BlockSpec((None, dil, tm // dil, LANES), lambda i: (i // tps, 0, i % tps, 0)) for _, dil in B_GROUPS]
    unperms = jnp.swapaxes(_dilation_perms(tm), 1, 2)
    return pl.pallas_call(
        functools.partial(_merge_kernel, final=final),
        grid=(n // tm,),
        in_specs=o_specs + lse_specs + [rows, rows, pl.BlockSpec((None, 3, d), lambda i: (i // tps, 0, 0)),
                                        pl.BlockSpec(w_out.shape, lambda i: (0, 0)),
                                        pl.BlockSpec((1, d), lambda i: (0, 0)),
                                        pl.BlockSpec(unperms.shape, lambda i: (0, 0, 0))],
        out_specs=rows,
        out_shape=jax.ShapeDtypeStruct((n, d), jnp.float32),
        scratch_shapes=[pltpu.VMEM((n_g, tm, LANES), jnp.float32),
                        pltpu.VMEM((tm, B_WIDTH), jnp.bfloat16)],
        compiler_params=pltpu.CompilerParams(vmem_limit_bytes=VMEM_LIMIT),
        name="merge_out",
    )(*os_, *lses, gate, h2, mod, w_out, fin, unperms)


def _final_norm_kernel(h_ref, g_ref, out_ref):
    h = h_ref[...]
    out_ref[...] = h * lax.rsqrt(jnp.mean(h * h, axis=-1, keepdims=True) + EPS) * g_ref[...]


def kernel(x, c, positions, a_norm, a_ada_w, a_ada_b, a_w_in, a_kv_norm, a_w_uv, a_w_out, kv_norm, w_kv, b_norm, b_ada_w, b_ada_b, b_w_in, b_w_out, final_norm):
    bsz, seq, d = x.shape
    n = bsz * seq
    n_a, n_b = a_norm.shape[0], b_norm.shape[0]
    topk = min(TOPK_MAX, seq // 4)
    assert topk % Q_BLOCK == 0 and seq % KEY_CHUNK == 0 and d == B_WIDTH
    assert all(seq % window == 0 for window, _ in B_GROUPS)

    cos, sin = _rope_tables(positions, ROPE_DIMS)
    h = x.reshape(n, d)
    for li in range(n_a):
        mod = _ada(c, a_ada_w[li], a_ada_b[li])
        qcat, keys, vt, kdup, qidx, aux, gate = _proj_a(
            h, mod, a_norm[li].reshape(1, d), a_kv_norm[li].reshape(1, A_LATENT), _a_weight(a_w_in[li]),
            cos, sin, bsz, seq)
        h = _attn_a(qcat, qidx, aux, keys, vt, kdup, gate, h, mod, _bf16(a_w_uv[li]), _bf16(a_w_out[li]), topk)
    k_sh = v_sh = None
    for li in range(n_b):
        mod = _ada(c, b_ada_w[li], b_ada_b[li])
        outs = _proj_b(h, mod, b_norm[li].reshape(1, d), kv_norm.reshape(1, d), _bf16(w_kv), _bf16(b_w_in[li]),
                       cos, sin, bsz, seq, with_kv=(li == 0))
        n_g = len(B_GROUPS)
        if li == 0:
            k_sh, v_sh = outs[:n_g], outs[n_g:2 * n_g]
            outs = outs[2 * n_g:]
        q_groups, gate = outs[:n_g], outs[n_g]
        res = [_dilated(qg, kg, vg, window, dil)
               for qg, kg, vg, (window, dil) in zip(q_groups, k_sh, v_sh, B_GROUPS)]
        h = _merge([r[0] for r in res], [r[1] for r in res], gate, h, mod, _bf16(b_w_out[li]),
                   final_norm.reshape(1, d), seq, final=(li == n_b - 1))
    if n_b == 0:
        h = pl.pallas_call(
            _final_norm_kernel, grid=(n // 256,),
            in_specs=[pl.BlockSpec((256, d), lambda i: (i, 0)), pl.BlockSpec((1, d), lambda i: (0, 0))],
            out_specs=pl.BlockSpec((256, d), lambda i: (i, 0)),
            out_shape=jax.ShapeDtypeStruct((n, d), jnp.float32), name="final_norm",
        )(h, final_norm.reshape(1, d))
    return h.reshape(bsz, seq, d)
```

```python
import functools
import math

import jax
import jax.numpy as jnp
from jax import lax
from jax.experimental import pallas as pl
from jax.experimental.pallas import tpu as pltpu

ROPE_THETA = 10000.0
EPS = 1e-6
LOG2E = 1.4426950408889634

A_HEADS = 16
A_LATENT = 128
A_ROPE = 32
A_VDIM = 128
A_WIDTH = A_HEADS * A_VDIM
IDX_HEADS = 8
IDX_DIM = 64
TOPK_MAX = 256
Q_BLOCK = 128
KEY_CHUNK = 256
A_QPAD = 256
HEADS_PER_SLAB = 2
V_ROWS = A_LATENT + 16

B_HEADS = 8
B_HDIM = 128
B_GROUPS = ((128, 1), (512, 4), (2048, 16))
B_WIDTH = B_HEADS * B_HDIM
ROPE_DIMS = (A_ROPE, IDX_DIM, B_HDIM)

LANES = 128
NEG = -1e30
INT_MIN = -(2 ** 31)
HALF = 2 ** 15
VMEM_LIMIT = 56 * 1024 * 1024

AUX_W_OFF = A_ROPE


def _bf16(t):
    return t.astype(jnp.bfloat16)


def _dot(a, b):
    return jnp.dot(a, b, preferred_element_type=jnp.float32)


def _dot_nt(a, b):
    return lax.dot_general(a, b, (((1,), (1,)), ((), ())), preferred_element_type=jnp.float32)


def _silu(t):
    return t * (1.0 / (1.0 + jnp.exp(-t)))


def _lane_iota(shape):
    return lax.broadcasted_iota(jnp.int32, shape, len(shape) - 1)


def _ada_kernel(c_ref, w_ref, b_ref, out_ref):
    out_ref[...] = _dot(_bf16(_silu(c_ref[...])), _bf16(w_ref[...])) + b_ref[...]


def _ada(c, w, b):
    bsz, d = c.shape
    out = pl.pallas_call(
        _ada_kernel,
        grid=(3,),
        in_specs=[pl.BlockSpec((bsz, d), lambda j: (0, 0)),
                  pl.BlockSpec((d, d), lambda j: (0, j)),
                  pl.BlockSpec((1, d), lambda j: (0, j))],
        out_specs=pl.BlockSpec((bsz, d), lambda j: (0, j)),
        out_shape=jax.ShapeDtypeStruct((bsz, 3 * d), jnp.float32),
        name="ada_mod",
    )(c, w, b.reshape(1, 3 * d))
    return out.reshape(bsz, 3, d)


def _rope_table_kernel(pos_ref, inv_ref, sel_ref, sign_ref, cos_ref, sin_ref):
    half_rows = pos_ref.shape[0] // 2
    upper = _lane_iota((half_rows, LANES)) >= LANES // 2
    ang = jnp.where(upper, pos_ref[half_rows:, :], pos_ref[:half_rows, :]) * inv_ref[...]
    cos, sin = jnp.cos(ang), jnp.sin(ang)
    for t in range(cos_ref.shape[0]):
        for part in range(2):
            pick = lambda v: jnp.dot(v, sel_ref[part, t], precision=lax.Precision.HIGHEST,
                                     preferred_element_type=jnp.float32)
            rows = slice(part * half_rows, (part + 1) * half_rows)
            cos_ref[t, rows, :] = pick(cos)
            sin_ref[t, rows, :] = pick(sin) * sign_ref[t]


def _rope_tables(positions, dims, tm=512):
    n = positions.size
    base = max(dims)
    assert base == LANES and all(base % d == 0 and d & (d - 1) == 0 for d in dims)
    posf = jnp.broadcast_to(positions.reshape(n, 1).astype(jnp.float32), (n, LANES))
    inv = ROPE_THETA ** (-jnp.arange(0, base, 2, dtype=jnp.float32) / base)
    lane = jnp.arange(LANES)
    inv = inv[lane % (base // 2)].reshape(1, LANES)
    src = [((lane % d) % (d // 2)) * (base // d) for d in dims]
    sel = jnp.stack([jnp.stack([(lane[:, None] == (s + part * (base // 2))[None, :]).astype(jnp.float32) for s in src])
                     for part in range(2)])
    sign = jnp.stack([jnp.where(lane % d < d // 2, -1.0, 1.0) for d in dims]).astype(jnp.float32)
    sign = sign.reshape(len(dims), 1, LANES)
    shape = jax.ShapeDtypeStruct((len(dims), n, LANES), jnp.float32)
    return pl.pallas_call(
        _rope_table_kernel,
        grid=(n // tm,),
        in_specs=[pl.BlockSpec((tm, LANES), lambda i: (i, 0)),
                  pl.BlockSpec((1, LANES), lambda i: (0, 0)),
                  pl.BlockSpec(sel.shape, lambda i: (0, 0, 0, 0)),
                  pl.BlockSpec(sign.shape, lambda i: (0, 0, 0))],
        out_specs=[pl.BlockSpec((len(dims), tm, LANES), lambda i: (0, i, 0))] * 2,
        out_shape=[shape, shape],
        name="rope_tables",
    )(posf, inv, sel, sign)


def _rope_lanes(xv, cos, sin_signed, dim):
    half = dim // 2
    if dim == LANES:
        partner = pltpu.roll(xv, half, axis=1)
    else:
        first = (_lane_iota(xv.shape) % dim) < half
        partner = jnp.where(first, pltpu.roll(xv, LANES - half, axis=1), pltpu.roll(xv, half, axis=1))
    return xv * cos + partner * sin_signed


def _prenorm(h, g, mod_ref):
    y = h * lax.rsqrt(jnp.mean(h * h, axis=-1, keepdims=True) + EPS) * g
    return y * (1.0 + mod_ref[1:2, :]) + mod_ref[0:1, :]


_A_COLS = {}
_o = 0
for _name, _w in (("q_lat", A_HEADS * A_LATENT), ("q_rope", A_HEADS * A_ROPE), ("c_kv", A_LATENT),
                  ("aux", LANES), ("kdup", LANES), ("gate", A_WIDTH), ("q_idx", IDX_HEADS * IDX_DIM)):
    _A_COLS[_name] = (_o, _o + _w)
    _o += _w
A_COLS_TOTAL = _o


def _a_weight(w_in):
    sizes = (A_HEADS * A_LATENT, A_HEADS * A_ROPE, A_LATENT, A_ROPE, A_WIDTH, IDX_HEADS * IDX_DIM, IDX_DIM, IDX_HEADS)
    parts, o = [], 0
    for s in sizes:
        parts.append(w_in[:, o:o + s])
        o += s
    q_lat, q_rope, c_kv, k_rope, gate, q_idx, k_idx, w_idx = parts
    d = w_in.shape[0]
    aux = jnp.concatenate([k_rope, w_idx, jnp.zeros((d, LANES - A_ROPE - IDX_HEADS), w_in.dtype)], axis=1)
    kdup = jnp.concatenate([k_idx, k_idx], axis=1)
    return _bf16(jnp.concatenate([q_lat, q_rope, c_kv, aux, kdup, gate, q_idx], axis=1))


def _proj_a_kernel(x_ref, mod_ref, g_ref, kvg_ref, w_ref, cos_ref, sin_ref,
                   qcat_ref, keys_ref, vt_ref, kdup_ref, qidx_ref, aux_ref, gate_ref):
    hn = _bf16(_prenorm(x_ref[...], g_ref[...], mod_ref))
    cos32, sin32 = cos_ref[0], sin_ref[0]
    cos64, sin64 = cos_ref[1], sin_ref[1]
    lane = _lane_iota(cos32.shape)
    qscale = (A_LATENT + A_ROPE) ** -0.5 * LOG2E

    def proj(name):
        lo, hi = _A_COLS[name]
        return _dot(hn, w_ref[:, lo:hi])

    q_blocks = x_ref.shape[0] // Q_BLOCK

    def put_q(h, row0, piece_t):
        slab, col0 = h // HEADS_PER_SLAB, (h % HEADS_PER_SLAB) * Q_BLOCK
        for qb in range(q_blocks):
            qcat_ref[qb, slab, row0:row0 + piece_t.shape[0], col0:col0 + Q_BLOCK] = _bf16(
                piece_t[:, qb * Q_BLOCK:(qb + 1) * Q_BLOCK])

    q_lat = proj("q_lat") * qscale
    q_rope = proj("q_rope")
    per_slab = LANES // A_ROPE
    pad = jnp.zeros((A_QPAD - A_LATENT - A_ROPE, x_ref.shape[0]), jnp.float32)
    for h in range(A_HEADS):
        put_q(h, 0, q_lat[:, h * A_LATENT:(h + 1) * A_LATENT].T)
    for j in range(A_HEADS // per_slab):
        roped_t = (_rope_lanes(q_rope[:, j * LANES:(j + 1) * LANES], cos32, sin32, A_ROPE) * qscale).T
        for t in range(per_slab):
            put_q(j * per_slab + t, A_LATENT, roped_t[t * A_ROPE:(t + 1) * A_ROPE, :])
            put_q(j * per_slab + t, A_LATENT + A_ROPE, pad)

    c_kv = proj("c_kv")
    c_kv = c_kv * lax.rsqrt(jnp.mean(c_kv * c_kv, axis=-1, keepdims=True) + EPS) * kvg_ref[...]
    keys_ref[:, 0:A_LATENT] = _bf16(c_kv)
    vt_ref[0:A_LATENT, :] = _bf16(c_kv.T)
    vt_ref[A_LATENT:V_ROWS, :] = jnp.ones((V_ROWS - A_LATENT, vt_ref.shape[1]), jnp.bfloat16)
    aux = proj("aux")
    k_rope = _rope_lanes(aux, cos32, sin32, A_ROPE)
    keys_ref[:, A_LATENT:A_QPAD] = _bf16(jnp.where(lane < A_ROPE, k_rope, 0.0))
    aux_ref[...] = aux * IDX_HEADS ** -0.5

    kdup_ref[...] = _bf16(_rope_lanes(proj("kdup"), cos64, sin64, IDX_DIM))
    q_idx = proj("q_idx")
    per_slab_i = LANES // IDX_DIM
    feat = lax.broadcasted_iota(jnp.int32, (LANES, x_ref.shape[0]), 0)
    for j in range(IDX_HEADS // per_slab_i):
        roped_t = _rope_lanes(q_idx[:, j * LANES:(j + 1) * LANES], cos64, sin64, IDX_DIM).T
        for t in range(per_slab_i):
            keep = (feat >= t * IDX_DIM) & (feat < (t + 1) * IDX_DIM)
            head_t = _bf16(jnp.where(keep, roped_t, 0.0))
            h = j * per_slab_i + t
            for qb in range(q_blocks):
                qidx_ref[qb, :, h * Q_BLOCK:(h + 1) * Q_BLOCK] = head_t[:, qb * Q_BLOCK:(qb + 1) * Q_BLOCK]

    gate_ref[...] = _bf16(_silu(proj("gate")))


def _proj_a(x2, mod, g, kvg, w, cos, sin, bsz, seq):
    n, d = x2.shape
    tm = KEY_CHUNK
    tps = seq // tm
    row = lambda i: (i, 0)
    qbt = tm // Q_BLOCK
    nq = seq // Q_BLOCK
    slabs, slab_w = A_HEADS // HEADS_PER_SLAB, HEADS_PER_SLAB * Q_BLOCK
    return pl.pallas_call(
        _proj_a_kernel,
        grid=(n // tm,),
        in_specs=[pl.BlockSpec((tm, d), row),
                  pl.BlockSpec((None, 3, d), lambda i: (i // tps, 0, 0)),
                  pl.BlockSpec((1, d), lambda i: (0, 0)),
                  pl.BlockSpec((1, A_LATENT), lambda i: (0, 0)),
                  pl.BlockSpec((d, A_COLS_TOTAL), lambda i: (0, 0)),
                  pl.BlockSpec((2, tm, LANES), lambda i: (0, i, 0)),
                  pl.BlockSpec((2, tm, LANES), lambda i: (0, i, 0))],
        out_specs=[pl.BlockSpec((None, qbt, slabs, A_QPAD, slab_w), lambda i: (i // tps, i % tps, 0, 0, 0)),
                   pl.BlockSpec((tm, A_QPAD), row),
                   pl.BlockSpec((None, None, V_ROWS, tm), lambda i: (i // tps, i % tps, 0, 0)),
                   pl.BlockSpec((tm, LANES), row),
                   pl.BlockSpec((None, qbt, LANES, IDX_HEADS * Q_BLOCK), lambda i: (i // tps, i % tps, 0, 0)),
                   pl.BlockSpec((tm, LANES), row),
                   pl.BlockSpec((tm, A_WIDTH), row)],
        out_shape=[jax.ShapeDtypeStruct((bsz, nq, slabs, A_QPAD, slab_w), jnp.bfloat16),
                   jax.ShapeDtypeStruct((n, A_QPAD), jnp.bfloat16),
                   jax.ShapeDtypeStruct((bsz, seq // tm, V_ROWS, tm), jnp.bfloat16),
                   jax.ShapeDtypeStruct((n, LANES), jnp.bfloat16),
                   jax.ShapeDtypeStruct((bsz, nq, LANES, IDX_HEADS * Q_BLOCK), jnp.bfloat16),
                   jax.ShapeDtypeStruct((n, LANES), jnp.float32),
                   jax.ShapeDtypeStruct((n, A_WIDTH), jnp.bfloat16)],
        compiler_params=pltpu.CompilerParams(vmem_limit_bytes=VMEM_LIMIT),
        name="proj_a",
    )(x2, mod, g, kvg, w, cos, sin)


def _attn_a_kernel(qcat_ref, qidx_ref, aux_ref, keys_ref, vt_ref, kdup_ref, gate_ref, x_ref, mod_ref, wuv_ref,
                   wout_ref, out_ref, sc_ref, hi_ref, lo_ref, lo2_ref, bias_ref, sa_ref, sb_ref, mxa_ref, mxb_ref, m_ref, acc_ref, og_ref, *, topk):
    i = pl.program_id(1)
    n_chunks = (i + 2) // 2
    seq = keys_ref.shape[0] * KEY_CHUNK

    w_t = aux_ref[...].T
    qi_all = qidx_ref[...]
    qpos = i * Q_BLOCK + lax.broadcasted_iota(jnp.int32, (KEY_CHUNK, Q_BLOCK), 1)
    krow = lax.broadcasted_iota(jnp.int32, (KEY_CHUNK, Q_BLOCK), 0)

    n_pairs = n_chunks // 2
    odd = n_chunks % 2 == 1

    def for_chunk_pairs(fn, init):
        def body(t, carry):
            return fn(2 * t + 1, fn(2 * t, carry))
        carry = lax.fori_loop(0, n_pairs, body, init)
        return lax.cond(odd, lambda cr: fn(n_chunks - 1, cr), lambda cr: cr, carry)

    def score_chunk(c, carry):
        rel = _dot(kdup_ref[c], qi_all)
        acc = jnp.zeros((KEY_CHUNK, Q_BLOCK), jnp.float32)
        for h in range(IDX_HEADS):
            acc = acc + w_t[AUX_W_OFF + h:AUX_W_OFF + h + 1, :] * jnp.maximum(rel[:, h * Q_BLOCK:(h + 1) * Q_BLOCK], 0.0)
        bits = lax.bitcast_convert_type(acc, jnp.int32)
        key = bits ^ ((bits >> 31) & 0x7FFFFFFF)
        key = jnp.where(c * KEY_CHUNK + krow <= qpos, key, INT_MIN)
        sc_ref[c] = key
        hi_ref[c] = (key >> 16).astype(jnp.int16)
        lo_ref[c] = ((key & 0xFFFF) - HALF).astype(jnp.int16)
        return carry

    for_chunk_pairs(score_chunk, 0)

    def count(pred):
        def body(c, acc):
            hit = jnp.where(pred(sc_ref[c], c), 1, 0)
            return acc + jnp.sum(hit.reshape(KEY_CHUNK // 32, 32, Q_BLOCK), axis=0)
        acc = for_chunk_pairs(body, jnp.zeros((32, Q_BLOCK), jnp.int32))
        return jnp.sum(acc, axis=0, keepdims=True)

    def count16(ref16, pred):
        def body(c, acc):
            hit = jnp.where(pred(ref16[c]), jnp.int16(1), jnp.int16(0))
            words = pltpu.bitcast(hit, jnp.int32)
            return acc + jnp.sum(words.reshape(KEY_CHUNK // 64, 32, Q_BLOCK), axis=0)
        acc = for_chunk_pairs(body, jnp.zeros((32, Q_BLOCK), jnp.int32))
        acc = jnp.sum(acc, axis=0, keepdims=True)
        return (acc & 0xFFFF) + (acc >> 16)

    def bisect16(ref16, target, n_bits):
        def bit(b, t):
            cand = t + lax.shift_left(jnp.int32(1), 15 - b)
            cand16 = cand.astype(jnp.int16)
            return jnp.where(count16(ref16, lambda v: v >= cand16) >= target, cand, t)
        return lax.fori_loop(0, n_bits, bit, jnp.full((1, Q_BLOCK), -HALF, jnp.int32))

    all_selected = (i + 1) * Q_BLOCK <= topk
    n_bits = jnp.where(all_selected, 0, 16)
    t_hi = bisect16(hi_ref, topk, n_bits)
    t_hi16 = t_hi.astype(jnp.int16)
    above = count16(hi_ref, lambda v: v > t_hi16)

    def low_chunk(c, carry):
        lo2_ref[c] = jnp.where(hi_ref[c] == t_hi16, lo_ref[c], jnp.int16(-HALF))
        return carry

    for_chunk_pairs(low_chunk, 0)
    t_lo = bisect16(lo2_ref, topk - above, n_bits)
    thr = lax.shift_left(t_hi, 16) + (t_lo + HALF)

    need = topk - count(lambda k, c: k > thr)
    n_eq = count(lambda k, c: k == thr)
    excess = jnp.max(jnp.where(n_eq > need, 1, 0)) > 0
    idx_bits = (seq - 1).bit_length()

    def tie_bound():
        def bit(b, bound):
            cand = bound + lax.shift_left(jnp.int32(1), idx_bits - 1 - b)
            below = count(lambda k, c: jnp.where(k == thr, c * KEY_CHUNK + krow, seq) < cand)
            return jnp.where(below < need, cand, bound)
        return lax.fori_loop(0, idx_bits, bit, jnp.zeros((1, Q_BLOCK), jnp.int32))

    tie_idx = lax.cond(jnp.logical_and(excess, jnp.logical_not(all_selected)), tie_bound,
                       lambda: jnp.where(all_selected, -1, seq) + jnp.zeros((1, Q_BLOCK), jnp.int32))

    def bias_chunk(c, carry):
        k = sc_ref[c]
        tie_ok = jnp.where(c * KEY_CHUNK + krow <= tie_idx, 0.0, NEG)
        bias_ref[c] = jnp.where(k > thr, 0.0, jnp.where(k == thr, tie_ok, NEG))
        return carry

    for_chunk_pairs(bias_chunk, 0)

    m_ref[...] = jnp.full(m_ref.shape, NEG, jnp.float32)
    acc_ref[...] = jnp.zeros(acc_ref.shape, jnp.float32)

    slabs = A_HEADS // HEADS_PER_SLAB
    last_chunk = n_chunks - 1

    def scores(c, s_buf, mx_buf):
        c = jnp.minimum(c, last_chunk)
        kc = keys_ref[c]
        bias2 = jnp.concatenate([bias_ref[c]] * HEADS_PER_SLAB, axis=1)
        for pr in range(slabs):
            s = _dot(kc, qcat_ref[pr]) + bias2
            s_buf[pr] = s
            mx_buf[pr] = jnp.max(s, axis=0, keepdims=True)

    def accumulate(c, s_buf, mx_buf):
        vt = vt_ref[c]
        for pr in range(slabs):
            m_old = m_ref[pr]
            m_new = jnp.maximum(m_old, mx_buf[pr])
            alpha = jnp.exp2(m_old - m_new)
            p = jnp.exp2(s_buf[pr] - m_new)
            m_ref[pr] = m_new
            acc_ref[pr] = acc_ref[pr] * alpha + _dot(vt, _bf16(p))

    scores(0, sa_ref, mxa_ref)

    def attend_pair(t, carry):
        scores(2 * t + 1, sb_ref, mxb_ref)
        accumulate(2 * t, sa_ref, mxa_ref)
        scores(2 * t + 2, sa_ref, mxa_ref)
        accumulate(2 * t + 1, sb_ref, mxb_ref)
        return carry

    lax.fori_loop(0, n_pairs, attend_pair, 0)

    @pl.when(odd)
    def _():
        accumulate(n_chunks - 1, sa_ref, mxa_ref)

    for pr in range(slabs):
        o_t = acc_ref[pr, 0:A_LATENT, :] / acc_ref[pr, A_LATENT:A_LATENT + 1, :]
        for t in range(HEADS_PER_SLAB):
            h = pr * HEADS_PER_SLAB + t
            o_lat = _bf16(o_t[:, t * Q_BLOCK:(t + 1) * Q_BLOCK].T)
            o = _dot(o_lat, wuv_ref[h]) * gate_ref[:, h * A_VDIM:(h + 1) * A_VDIM]
            og_ref[:, h * A_VDIM:(h + 1) * A_VDIM] = _bf16(o)
    out_ref[...] = x_ref[...] + mod_ref[2:3, :] * _dot(og_ref[...], wout_ref[...])


def _attn_a(qcat, qidx, aux, keys, vt, kdup, gate, x2, mod, w_uv, w_out, topk):
    bsz, nq = qcat.shape[:2]
    seq = nq * Q_BLOCK
    d = x2.shape[1]
    nkc = seq // KEY_CHUNK
    row = lambda b, i: (b * nq + i, 0)
    slabs = A_HEADS // HEADS_PER_SLAB
    slab_w = HEADS_PER_SLAB * Q_BLOCK
    return pl.pallas_call(
        functools.partial(_attn_a_kernel, topk=topk),
        grid=(bsz, nq),
        in_specs=[pl.BlockSpec((None, None, slabs, A_QPAD, slab_w), lambda b, i: (b, i, 0, 0, 0)),
                  pl.BlockSpec((None, None, LANES, IDX_HEADS * Q_BLOCK), lambda b, i: (b, i, 0, 0)),
                  pl.BlockSpec((Q_BLOCK, LANES), row),
                  pl.BlockSpec((None, nkc, KEY_CHUNK, A_QPAD), lambda b, i: (b, 0, 0, 0)),
                  pl.BlockSpec((None, nkc, V_ROWS, KEY_CHUNK), lambda b, i: (b, 0, 0, 0)),
                  pl.BlockSpec((None, nkc, KEY_CHUNK, LANES), lambda b, i: (b, 0, 0, 0)),
                  pl.BlockSpec((Q_BLOCK, A_WIDTH), row),
                  pl.BlockSpec((Q_BLOCK, d), row),
                  pl.BlockSpec((None, 3, d), lambda b, i: (b, 0, 0)),
                  pl.BlockSpec((A_HEADS, A_LATENT, A_VDIM), lambda b, i: (0, 0, 0)),
                  pl.BlockSpec((A_WIDTH, d), lambda b, i: (0, 0))],
        out_specs=pl.BlockSpec((Q_BLOCK, d), row),
        out_shape=jax.ShapeDtypeStruct(x2.shape, jnp.float32),
        scratch_shapes=[pltpu.VMEM((nkc, KEY_CHUNK, Q_BLOCK), jnp.int32),
                        pltpu.VMEM((nkc, KEY_CHUNK, Q_BLOCK), jnp.int16),
                        pltpu.VMEM((nkc, KEY_CHUNK, Q_BLOCK), jnp.int16),
                        pltpu.VMEM((nkc, KEY_CHUNK, Q_BLOCK), jnp.int16),
                        pltpu.VMEM((nkc, KEY_CHUNK, Q_BLOCK), jnp.float32),
                        pltpu.VMEM((slabs, KEY_CHUNK, slab_w), jnp.float32),
                        pltpu.VMEM((slabs, KEY_CHUNK, slab_w), jnp.float32),
                        pltpu.VMEM((slabs, 1, slab_w), jnp.float32),
                        pltpu.VMEM((slabs, 1, slab_w), jnp.float32),
                        pltpu.VMEM((slabs, 1, slab_w), jnp.float32),
                        pltpu.VMEM((slabs, V_ROWS, slab_w), jnp.float32),
                        pltpu.VMEM((Q_BLOCK, A_WIDTH), jnp.bfloat16)],
        compiler_params=pltpu.CompilerParams(vmem_limit_bytes=VMEM_LIMIT),
        name="attn_a",
    )(qcat, qidx, aux, keys.reshape(bsz, nkc, KEY_CHUNK, A_QPAD), vt, kdup.reshape(bsz, nkc, KEY_CHUNK, LANES),
      gate, x2, mod, w_uv, w_out)


_PERM_DILS = tuple(dil for _, dil in B_GROUPS if dil > 1)


def _dilation_perms(tm):
    dst = jnp.arange(tm)
    perms = []
    for dil in _PERM_DILS:
        src = (dst % (tm // dil)) * dil + dst // (tm // dil)
        perms.append((src[:, None] == dst[None, :]).astype(jnp.bfloat16))
    return jnp.stack(perms)


_MATMUL_REORDER_DIL = 16


def _store_dilated(val, out_refs, tmp_ref, perm_ref):
    vb = _bf16(val)
    tm, width = vb.shape
    heads, _, hdim = tmp_ref.shape
    in_tmp = False
    for out_ref in out_refs:
        dil = out_ref.shape[1] // width
        rows = tm // dil
        if dil == 1:
            out_ref[...] = vb
        elif dil >= _MATMUL_REORDER_DIL:
            res = _bf16(_dot(perm_ref[_PERM_DILS.index(dil)], vb))
            for r in range(dil):
                out_ref[:, r * width:(r + 1) * width] = res[r * rows:(r + 1) * rows, :]
        else:
            if not in_tmp:
                for hd in range(heads):
                    tmp_ref[hd] = val[:, hd * hdim:(hd + 1) * hdim]
                in_tmp = True
            for r in range(dil):
                for hd in range(heads):
                    col = r * width + hd * hdim
                    out_ref[:, col:col + hdim] = _bf16(tmp_ref[hd, pl.ds(r, rows, stride=dil), :])


def _proj_b_kernel(*refs, with_kv):
    n_g = len(B_GROUPS)
    if with_kv:
        h_ref, mod_ref, g_ref, kvg_ref, wkv_ref, win_ref, cos_ref, sin_ref, perm_ref = refs[:9]
        outs = refs[9:]
        k_refs, v_refs, outs = outs[:n_g], outs[n_g:2 * n_g], outs[2 * n_g:]
    else:
        h_ref, mod_ref, g_ref, win_ref, cos_ref, sin_ref, perm_ref = refs[:7]
        outs = refs[7:]
    q_refs, gate_ref, tmp_ref = outs[:n_g], outs[n_g], outs[n_g + 1]
    h = h_ref[...]
    cos, sin = cos_ref[...], sin_ref[...]
    width = B_WIDTH

    def rope(t, scale):
        return jnp.concatenate([_rope_lanes(t[:, hd * B_HDIM:(hd + 1) * B_HDIM], cos, sin, B_HDIM) * scale
                                for hd in range(B_HEADS)], axis=1)

    if with_kv:
        kvn = _bf16(h * lax.rsqrt(jnp.mean(h * h, axis=-1, keepdims=True) + EPS) * kvg_ref[...])
        _store_dilated(rope(_dot(kvn, wkv_ref[:, 0:width]), 1.0), k_refs, tmp_ref, perm_ref)
        _store_dilated(_dot(kvn, wkv_ref[:, width:2 * width]), v_refs, tmp_ref, perm_ref)
    hn = _bf16(_prenorm(h, g_ref[...], mod_ref))
    qscale = B_HDIM ** -0.5 * LOG2E
    for gi, q_ref in enumerate(q_refs):
        _store_dilated(rope(_dot(hn, win_ref[:, gi * width:(gi + 1) * width]), qscale), [q_ref], tmp_ref, perm_ref)
    gate_ref[...] = _bf16(_silu(_dot(hn, win_ref[:, n_g * width:(n_g + 1) * width])))


def _proj_b(h2, mod, g, kvg, w_kv, w_in, cos, sin, bsz, seq, with_kv, tm=256):
    n, d = h2.shape
    tps = seq // tm
    row = lambda i: (i, 0)
    const = lambda i: (0, 0)
    in_specs = [pl.BlockSpec((tm, d), row), pl.BlockSpec((None, 3, d), lambda i: (i // tps, 0, 0)),
                pl.BlockSpec((1, d), const)]
    args = [h2, mod, g]
    if with_kv:
        in_specs += [pl.BlockSpec((1, d), const), pl.BlockSpec(w_kv.shape, const)]
        args += [kvg, w_kv]
    perms = _dilation_perms(tm)
    table = pl.BlockSpec((None, tm, LANES), lambda i: (ROPE_DIMS.index(B_HDIM), i, 0))
    in_specs += [pl.BlockSpec(w_in.shape, const), table, table, pl.BlockSpec(perms.shape, lambda i: (0, 0, 0))]
    args += [w_in, cos, sin, perms]
    dil_specs = [pl.BlockSpec((None, tm // dil, dil * B_WIDTH), lambda i: (i // tps, i % tps, 0)) for _, dil in B_GROUPS]
    dil_shapes = [jax.ShapeDtypeStruct((bsz, seq // dil, dil * B_WIDTH), jnp.bfloat16) for _, dil in B_GROUPS]
    n_sets = 3 if with_kv else 1
    return pl.pallas_call(
        functools.partial(_proj_b_kernel, with_kv=with_kv),
        grid=(n // tm,),
        in_specs=in_specs,
        out_specs=dil_specs * n_sets + [pl.BlockSpec((tm, B_WIDTH), row)],
        out_shape=dil_shapes * n_sets + [jax.ShapeDtypeStruct((n, B_WIDTH), jnp.bfloat16)],
        scratch_shapes=[pltpu.VMEM((B_HEADS, tm, B_HDIM), jnp.float32)],
        compiler_params=pltpu.CompilerParams(vmem_limit_bytes=VMEM_LIMIT),
        name="proj_b",
    )(*args)


def _dilated_kernel(q_ref, kc_ref, vc_ref, o_ref, lse_ref, kp_ref, vp_ref):
    j = pl.program_id(2)
    w = q_ref.shape[0]

    @pl.when(j == 0)
    def _():
        kp_ref[...] = jnp.zeros(kp_ref.shape, kp_ref.dtype)
        vp_ref[...] = jnp.zeros(vp_ref.shape, vp_ref.dtype)

    qi = lax.broadcasted_iota(jnp.int32, (w, w), 0)
    ki = lax.broadcasted_iota(jnp.int32, (w, w), 1)
    bias = jnp.concatenate([jnp.where(jnp.logical_and(ki >= qi, j > 0), 0.0, NEG),
                            jnp.where(ki <= qi, 0.0, NEG)], axis=1)
    ones = jnp.ones((2 * w, B_HDIM), jnp.bfloat16)
    lses = []
    for hd in range(B_HEADS):
        sl = slice(hd * B_HDIM, (hd + 1) * B_HDIM)
        keys = jnp.concatenate([kp_ref[:, sl], kc_ref[:, sl]], axis=0)
        vals = jnp.concatenate([vp_ref[:, sl], vc_ref[:, sl]], axis=0)
        s = _dot_nt(q_ref[:, sl], keys) + bias
        m = jnp.max(s, axis=-1, keepdims=True)
        p = _bf16(jnp.exp2(s - m))
        ol = _dot(p, jnp.concatenate([vals, ones], axis=1))
        l = ol[:, B_HDIM:]
        o_ref[:, sl] = _bf16(ol[:, :B_HDIM] / l)
        lses.append(m + jnp.log2(l[:, 0:1]))
    lse_ref[...] = jnp.concatenate(lses + [jnp.zeros((w, LANES - B_HEADS), jnp.float32)], axis=1)
    kp_ref[...] = kc_ref[...]
    vp_ref[...] = vc_ref[...]


def _dilated(q, k, v, window, dil):
    bsz, n, _ = q.shape
    w = window // dil
    nb = n // w
    cur = lambda b, r, j: (b, j, r)
    blk = pl.BlockSpec((None, w, B_WIDTH), cur)
    return pl.pallas_call(
        _dilated_kernel,
        grid=(bsz, dil, nb),
        in_specs=[blk, blk, blk],
        out_specs=[blk, pl.BlockSpec((None, None, w, LANES), lambda b, r, j: (b, r, j, 0))],
        out_shape=[jax.ShapeDtypeStruct(q.shape, jnp.bfloat16),
                   jax.ShapeDtypeStruct((bsz, dil, n, LANES), jnp.float32)],
        scratch_shapes=[pltpu.VMEM((w, B_WIDTH), jnp.bfloat16), pltpu.VMEM((w, B_WIDTH), jnp.bfloat16)],
        compiler_params=pltpu.CompilerParams(dimension_semantics=("arbitrary", "arbitrary", "arbitrary")),
        name=f"dilated_{dil}",
    )(q, k, v)


def _merge_kernel(*refs, final):
    n_g = len(B_GROUPS)
    o_refs, lse_refs = refs[:n_g], refs[n_g:2 * n_g]
    gate_ref, h_ref, mod_ref, wout_ref, fin_ref, unperm_ref, out_ref, ls_ref, og_ref = refs[2 * n_g:]
    tm = h_ref.shape[0]
    o_seq = []
    for gi, (_, dil) in enumerate(B_GROUPS):
        rows = tm // dil
        for r in range(dil):
            dst = pl.ds(r, rows, stride=dil) if dil > 1 else slice(None)
            ls_ref[gi, dst, :] = lse_refs[gi][r]
        if dil == 1:
            o_seq.append(o_refs[gi][...].astype(jnp.float32))
        else:
            stacked = jnp.concatenate([o_refs[gi][:, r * B_WIDTH:(r + 1) * B_WIDTH] for r in range(dil)], axis=0)
            o_seq.append(_dot(unperm_ref[_PERM_DILS.index(dil)], stacked))
    lses = [ls_ref[gi] for gi in range(n_g)]
    m = functools.reduce(jnp.maximum, lses)
    e = [jnp.exp2(t - m) for t in lses]
    inv = 1.0 / functools.reduce(lambda a, b: a + b, e)
    for hd in range(B_HEADS):
        sl = slice(hd * B_HDIM, (hd + 1) * B_HDIM)
        o = functools.reduce(lambda a, b: a + b,
                             [(e[gi] * inv)[:, hd:hd + 1] * o_seq[gi][:, sl] for gi in range(n_g)])
        og_ref[:, sl] = _bf16(o * gate_ref[:, sl])
    h = h_ref[...] + mod_ref[2:3, :] * _dot(og_ref[...], wout_ref[...])
    if final:
        h = h * lax.rsqrt(jnp.mean(h * h, axis=-1, keepdims=True) + EPS) * fin_ref[...]
    out_ref[...] = h


def _merge(os_, lses, gate, h2, mod, w_out, fin, seq, final, tm=256):
    n, d = h2.shape
    tps = seq // tm
    row = lambda i: (i, 0)
    rows = pl.BlockSpec((tm, d), row)
    n_g = len(B_GROUPS)
    o_specs = [pl.BlockSpec((None, tm // dil, dil * B_WIDTH), lambda i: (i // tps, i % tps, 0)) for _, dil in B_GROUPS]
    lse_specs = [pl.BlockSpec((None, dil, tm // dil, LANES), lambda i: (i // tps, 0, i % tps, 0)) for _, dil in B_GROUPS]
    unperms = jnp.swapaxes(_dilation_perms(tm), 1, 2)
    return pl.pallas_call(
        functools.partial(_merge_kernel, final=final),
        grid=(n // tm,),
        in_specs=o_specs + lse_specs + [rows, rows, pl.BlockSpec((None, 3, d), lambda i: (i // tps, 0, 0)),
                                        pl.BlockSpec(w_out.shape, lambda i: (0, 0)),
                                        pl.BlockSpec((1, d), lambda i: (0, 0)),
                                        pl.BlockSpec(unperms.shape, lambda i: (0, 0, 0))],
        out_specs=rows,
        out_shape=jax.ShapeDtypeStruct((n, d), jnp.float32),
        scratch_shapes=[pltpu.VMEM((n_g, tm, LANES), jnp.float32),
                        pltpu.VMEM((tm, B_WIDTH), jnp.bfloat16)],
        compiler_params=pltpu.CompilerParams(vmem_limit_bytes=VMEM_LIMIT),
        name="merge_out",
    )(*os_, *lses, gate, h2, mod, w_out, fin, unperms)


def _final_norm_kernel(h_ref, g_ref, out_ref):
    h = h_ref[...]
    out_ref[...] = h * lax.rsqrt(jnp.mean(h * h, axis=-1, keepdims=True) + EPS) * g_ref[...]


def kernel(x, c, positions, a_norm, a_ada_w, a_ada_b, a_w_in, a_kv_norm, a_w_uv, a_w_out, kv_norm, w_kv, b_norm, b_ada_w, b_ada_b, b_w_in, b_w_out, final_norm):
    bsz, seq, d = x.shape
    n = bsz * seq
    n_a, n_b = a_norm.shape[0], b_norm.shape[0]
    topk = min(TOPK_MAX, seq // 4)
    assert topk % Q_BLOCK == 0 and seq % KEY_CHUNK == 0 and d == B_WIDTH
    assert all(seq % window == 0 for window, _ in B_GROUPS)

    cos, sin = _rope_tables(positions, ROPE_DIMS)
    h = x.reshape(n, d)
    for li in range(n_a):
        mod = _ada(c, a_ada_w[li], a_ada_b[li])
        qcat, keys, vt, kdup, qidx, aux, gate = _proj_a(
            h, mod, a_norm[li].reshape(1, d), a_kv_norm[li].reshape(1, A_LATENT), _a_weight(a_w_in[li]),
            cos, sin, bsz, seq)
        h = _attn_a(qcat, qidx, aux, keys, vt, kdup, gate, h, mod, _bf16(a_w_uv[li]), _bf16(a_w_out[li]), topk)
    k_sh = v_sh = None
    for li in range(n_b):
        mod = _ada(c, b_ada_w[li], b_ada_b[li])
        outs = _proj_b(h, mod, b_norm[li].reshape(1, d), kv_norm.reshape(1, d), _bf16(w_kv), _bf16(b_w_in[li]),
                       cos, sin, bsz, seq, with_kv=(li == 0))
        n_g = len(B_GROUPS)
        if li == 0:
            k_sh, v_sh = outs[:n_g], outs[n_g:2 * n_g]
            outs = outs[2 * n_g:]
        q_groups, gate = outs[:n_g], outs[n_g]
        res = [_dilated(qg, kg, vg, window, dil)
               for qg, kg, vg, (window, dil) in zip(q_groups, k_sh, v_sh, B_GROUPS)]
        h = _merge([r[0] for r in res], [r[1] for r in res], gate, h, mod, _bf16(b_w_out[li]),
                   final_norm.reshape(1, d), seq, final=(li == n_b - 1))
    if n_b == 0:
        h = pl.pallas_call(
            _final_norm_kernel, grid=(n // 256,),
            in_specs=[pl.BlockSpec((256, d), lambda i: (i, 0)), pl.BlockSpec((1, d), lambda i: (0, 0))],
            out_specs=pl.BlockSpec((256, d), lambda i: (i, 0)),
            out_shape=jax.ShapeDtypeStruct((n, d), jnp.float32), name="final_norm",
        )(h, final_norm.reshape(1, d))
    return h.reshape(bsz, seq, d)
```

```python
import functools
import math

import jax
import jax.numpy as jnp
from jax import lax
from jax.experimental import pallas as pl
from jax.experimental.pallas import tpu as pltpu

ROPE_THETA = 10000.0
EPS = 1e-6
LOG2E = 1.4426950408889634

A_HEADS = 16
A_LATENT = 128
A_ROPE = 32
A_VDIM = 128
A_WIDTH = A_HEADS * A_VDIM
IDX_HEADS = 8
IDX_DIM = 64
TOPK_MAX = 256
Q_BLOCK = 128
KEY_CHUNK = 256
A_QPAD = 256
HEADS_PER_SLAB = 2
V_ROWS = A_LATENT + 16

B_HEADS = 8
B_HDIM = 128
B_GROUPS = ((128, 1), (512, 4), (2048, 16))
B_WIDTH = B_HEADS * B_HDIM
ROPE_DIMS = (A_ROPE, IDX_DIM, B_HDIM)

LANES = 128
NEG = -1e30
INT_MIN = -(2 ** 31)
HALF = 2 ** 15
VMEM_LIMIT = 56 * 1024 * 1024

AUX_W_OFF = A_ROPE


def _bf16(t):
    return t.astype(jnp.bfloat16)


def _dot(a, b):
    return jnp.dot(a, b, preferred_element_type=jnp.float32)


def _dot_nt(a, b):
    return lax.dot_general(a, b, (((1,), (1,)), ((), ())), preferred_element_type=jnp.float32)


def _silu(t):
    return t * (1.0 / (1.0 + jnp.exp(-t)))


def _lane_iota(shape):
    return lax.broadcasted_iota(jnp.int32, shape, len(shape) - 1)


def _ada_kernel(c_ref, w_ref, b_ref, out_ref):
    out_ref[...] = _dot(_bf16(_silu(c_ref[...])), _bf16(w_ref[...])) + b_ref[...]


def _ada(c, w, b):
    bsz, d = c.shape
    out = pl.pallas_call(
        _ada_kernel,
        grid=(3,),
        in_specs=[pl.BlockSpec((bsz, d), lambda j: (0, 0)),
                  pl.BlockSpec((d, d), lambda j: (0, j)),
                  pl.BlockSpec((1, d), lambda j: (0, j))],
        out_specs=pl.BlockSpec((bsz, d), lambda j: (0, j)),
        out_shape=jax.ShapeDtypeStruct((bsz, 3 * d), jnp.float32),
        name="ada_mod",
    )(c, w, b.reshape(1, 3 * d))
    return out.reshape(bsz, 3, d)


def _rope_table_kernel(pos_ref, inv_ref, sel_ref, sign_ref, cos_ref, sin_ref):
    half_rows = pos_ref.shape[0] // 2
    upper = _lane_iota((half_rows, LANES)) >= LANES // 2
    ang = jnp.where(upper, pos_ref[half_rows:, :], pos_ref[:half_rows, :]) * inv_ref[...]
    cos, sin = jnp.cos(ang), jnp.sin(ang)
    for t in range(cos_ref.shape[0]):
        for part in range(2):
            pick = lambda v: jnp.dot(v, sel_ref[part, t], precision=lax.Precision.HIGHEST,
                                     preferred_element_type=jnp.float32)
            rows = slice(part * half_rows, (part + 1) * half_rows)
            cos_ref[t, rows, :] = pick(cos)
            sin_ref[t, rows, :] = pick(sin) * sign_ref[t]


def _rope_tables(positions, dims, tm=512):
    n = positions.size
    base = max(dims)
    assert base == LANES and all(base % d == 0 and d & (d - 1) == 0 for d in dims)
    posf = jnp.broadcast_to(positions.reshape(n, 1).astype(jnp.float32), (n, LANES))
    inv = ROPE_THETA ** (-jnp.arange(0, base, 2, dtype=jnp.float32) / base)
    lane = jnp.arange(LANES)
    inv = inv[lane % (base // 2)].reshape(1, LANES)
    src = [((lane % d) % (d // 2)) * (base // d) for d in dims]
    sel = jnp.stack([jnp.stack([(lane[:, None] == (s + part * (base // 2))[None, :]).astype(jnp.float32) for s in src])
                     for part in range(2)])
    sign = jnp.stack([jnp.where(lane % d < d // 2, -1.0, 1.0) for d in dims]).astype(jnp.float32)
    sign = sign.reshape(len(dims), 1, LANES)
    shape = jax.ShapeDtypeStruct((len(dims), n, LANES), jnp.float32)
    return pl.pallas_call(
        _rope_table_kernel,
        grid=(n // tm,),
        in_specs=[pl.BlockSpec((tm, LANES), lambda i: (i, 0)),
                  pl.BlockSpec((1, LANES), lambda i: (0, 0)),
                  pl.BlockSpec(sel.shape, lambda i: (0, 0, 0, 0)),
                  pl.BlockSpec(sign.shape, lambda i: (0, 0, 0))],
        out_specs=[pl.BlockSpec((len(dims), tm, LANES), lambda i: (0, i, 0))] * 2,
        out_shape=[shape, shape],
        name="rope_tables",
    )(posf, inv, sel, sign)


def _rope_lanes(xv, cos, sin_signed, dim):
    half = dim // 2
    if dim == LANES:
        partner = pltpu.roll(xv, half, axis=1)
    else:
        first = (_lane_iota(xv.shape) % dim) < half
        partner = jnp.where(first, pltpu.roll(xv, LANES - half, axis=1), pltpu.roll(xv, half, axis=1))
    return xv * cos + partner * sin_signed


def _prenorm(h, g, mod_ref):
    y = h * lax.rsqrt(jnp.mean(h * h, axis=-1, keepdims=True) + EPS) * g
    return y * (1.0 + mod_ref[1:2, :]) + mod_ref[0:1, :]


_A_COLS = {}
_o = 0
for _name, _w in (("q_lat", A_HEADS * A_LATENT), ("q_rope", A_HEADS * A_ROPE), ("c_kv", A_LATENT),
                  ("aux", LANES), ("kdup", LANES), ("gate", A_WIDTH), ("q_idx", IDX_HEADS * IDX_DIM)):
    _A_COLS[_name] = (_o, _o + _w)
    _o += _w
A_COLS_TOTAL = _o


def _a_weight(w_in):
    sizes = (A_HEADS * A_LATENT, A_HEADS * A_ROPE, A_LATENT, A_ROPE, A_WIDTH, IDX_HEADS * IDX_DIM, IDX_DIM, IDX_HEADS)
    parts, o = [], 0
    for s in sizes:
        parts.append(w_in[:, o:o + s])
        o += s
    q_lat, q_rope, c_kv, k_rope, gate, q_idx, k_idx, w_idx = parts
    d = w_in.shape[0]
    aux = jnp.concatenate([k_rope, w_idx, jnp.zeros((d, LANES - A_ROPE - IDX_HEADS), w_in.dtype)], axis=1)
    kdup = jnp.concatenate([k_idx, k_idx], axis=1)
    return _bf16(jnp.concatenate([q_lat, q_rope, c_kv, aux, kdup, gate, q_idx], axis=1))


def _proj_a_kernel(x_ref, mod_ref, g_ref, kvg_ref, w_ref, cos_ref, sin_ref,
                   qcat_ref, keys_ref, vt_ref, kdup_ref, qidx_ref, aux_ref, gate_ref):
    hn = _bf16(_prenorm(x_ref[...], g_ref[...], mod_ref))
    cos32, sin32 = cos_ref[0], sin_ref[0]
    cos64, sin64 = cos_ref[1], sin_ref[1]
    lane = _lane_iota(cos32.shape)
    qscale = (A_LATENT + A_ROPE) ** -0.5 * LOG2E

    def proj(name):
        lo, hi = _A_COLS[name]
        return _dot(hn, w_ref[:, lo:hi])

    q_blocks = x_ref.shape[0] // Q_BLOCK

    def put_q(h, row0, piece_t):
        slab, col0 = h // HEADS_PER_SLAB, (h % HEADS_PER_SLAB) * Q_BLOCK
        for qb in range(q_blocks):
            qcat_ref[qb, slab, row0:row0 + piece_t.shape[0], col0:col0 + Q_BLOCK] = _bf16(
                piece_t[:, qb * Q_BLOCK:(qb + 1) * Q_BLOCK])

    q_lat = proj("q_lat") * qscale
    q_rope = proj("q_rope")
    per_slab = LANES // A_ROPE
    pad = jnp.zeros((A_QPAD - A_LATENT - A_ROPE, x_ref.shape[0]), jnp.float32)
    for h in range(A_HEADS):
        put_q(h, 0, q_lat[:, h * A_LATENT:(h + 1) * A_LATENT].T)
    for j in range(A_HEADS // per_slab):
        roped_t = (_rope_lanes(q_rope[:, j * LANES:(j + 1) * LANES], cos32, sin32, A_ROPE) * qscale).T
        for t in range(per_slab):
            put_q(j * per_slab + t, A_LATENT, roped_t[t * A_ROPE:(t + 1) * A_ROPE, :])
            put_q(j * per_slab + t, A_LATENT + A_ROPE, pad)

    c_kv = proj("c_kv")
    c_kv = c_kv * lax.rsqrt(jnp.mean(c_kv * c_kv, axis=-1, keepdims=True) + EPS) * kvg_ref[...]
    keys_ref[:, 0:A_LATENT] = _bf16(c_kv)
    vt_ref[0:A_LATENT, :] = _bf16(c_kv.T)
    vt_ref[A_LATENT:V_ROWS, :] = jnp.ones((V_ROWS - A_LATENT, vt_ref.shape[1]), jnp.bfloat16)
    aux = proj("aux")
    k_rope = _rope_lanes(aux, cos32, sin32, A_ROPE)
    keys_ref[:, A_LATENT:A_QPAD] = _bf16(jnp.where(lane < A_ROPE, k_rope, 0.0))
    aux_ref[...] = aux * IDX_HEADS ** -0.5

    kdup_ref[...] = _bf16(_rope_lanes(proj("kdup"), cos64, sin64, IDX_DIM))
    q_idx = proj("q_idx")
    per_slab_i = LANES // IDX_DIM
    feat = lax.broadcasted_iota(jnp.int32, (LANES, x_ref.shape[0]), 0)
    for j in range(IDX_HEADS // per_slab_i):
        roped_t = _rope_lanes(q_idx[:, j * LANES:(j + 1) * LANES], cos64, sin64, IDX_DIM).T
        for t in range(per_slab_i):
            keep = (feat >= t * IDX_DIM) & (feat < (t + 1) * IDX_DIM)
            head_t = _bf16(jnp.where(keep, roped_t, 0.0))
            h = j * per_slab_i + t
            for qb in range(q_blocks):
                qidx_ref[qb, :, h * Q_BLOCK:(h + 1) * Q_BLOCK] = head_t[:, qb * Q_BLOCK:(qb + 1) * Q_BLOCK]

    gate_ref[...] = _bf16(_silu(proj("gate")))


def _proj_a(x2, mod, g, kvg, w, cos, sin, bsz, seq):
    n, d = x2.shape
    tm = KEY_CHUNK
    tps = seq // tm
    row = lambda i: (i, 0)
    qbt = tm // Q_BLOCK
    nq = seq // Q_BLOCK
    slabs, slab_w = A_HEADS // HEADS_PER_SLAB, HEADS_PER_SLAB * Q_BLOCK
    return pl.pallas_call(
        _proj_a_kernel,
        grid=(n // tm,),
        in_specs=[pl.BlockSpec((tm, d), row),
                  pl.BlockSpec((None, 3, d), lambda i: (i // tps, 0, 0)),
                  pl.BlockSpec((1, d), lambda i: (0, 0)),
                  pl.BlockSpec((1, A_LATENT), lambda i: (0, 0)),
                  pl.BlockSpec((d, A_COLS_TOTAL), lambda i: (0, 0)),
                  pl.BlockSpec((2, tm, LANES), lambda i: (0, i, 0)),
                  pl.BlockSpec((2, tm, LANES), lambda i: (0, i, 0))],
        out_specs=[pl.BlockSpec((None, qbt, slabs, A_QPAD, slab_w), lambda i: (i // tps, i % tps, 0, 0, 0)),
                   pl.BlockSpec((tm, A_QPAD), row),
                   pl.BlockSpec((None, None, V_ROWS, tm), lambda i: (i // tps, i % tps, 0, 0)),
                   pl.BlockSpec((tm, LANES), row),
                   pl.BlockSpec((None, qbt, LANES, IDX_HEADS * Q_BLOCK), lambda i: (i // tps, i % tps, 0, 0)),
                   pl.BlockSpec((tm, LANES), row),
                   pl.BlockSpec((tm, A_WIDTH), row)],
        out_shape=[jax.ShapeDtypeStruct((bsz, nq, slabs, A_QPAD, slab_w), jnp.bfloat16),
                   jax.ShapeDtypeStruct((n, A_QPAD), jnp.bfloat16),
                   jax.ShapeDtypeStruct((bsz, seq // tm, V_ROWS, tm), jnp.bfloat16),
                   jax.ShapeDtypeStruct((n, LANES), jnp.bfloat16),
                   jax.ShapeDtypeStruct((bsz, nq, LANES, IDX_HEADS * Q_BLOCK), jnp.bfloat16),
                   jax.ShapeDtypeStruct((n, LANES), jnp.float32),
                   jax.ShapeDtypeStruct((n, A_WIDTH), jnp.bfloat16)],
        compiler_params=pltpu.CompilerParams(vmem_limit_bytes=VMEM_LIMIT),
        name="proj_a",
    )(x2, mod, g, kvg, w, cos, sin)


def _attn_a_kernel(qcat_ref, qidx_ref, aux_ref, keys_ref, vt_ref, kdup_ref, gate_ref, x_ref, mod_ref, wuv_ref,
                   wout_ref, out_ref, sc_ref, hi_ref, lo_ref, lo2_ref, bias_ref, sa_ref, sb_ref, mxa_ref, mxb_ref, m_ref, acc_ref, og_ref, *, topk):
    i = pl.program_id(1)
    n_chunks = (i + 2) // 2
    seq = keys_ref.shape[0] * KEY_CHUNK

    w_t = aux_ref[...].T
    qpos = i * Q_BLOCK + lax.broadcasted_iota(jnp.int32, (KEY_CHUNK, Q_BLOCK), 1)
    krow = lax.broadcasted_iota(jnp.int32, (KEY_CHUNK, Q_BLOCK), 0)

    n_pairs = n_chunks // 2
    odd = n_chunks % 2 == 1

    def for_chunk_pairs(fn, init, padded=False):
        def body(t, carry):
            return fn(2 * t + 1, fn(2 * t, carry))
        if padded:
            return lax.fori_loop(0, (n_chunks + 1) // 2, body, init)
        carry = lax.fori_loop(0, n_pairs, body, init)
        return lax.cond(odd, lambda cr: fn(n_chunks - 1, cr), lambda cr: cr, carry)

    def score_chunk(c, carry):
        kd = kdup_ref[c]
        acc = jnp.zeros((KEY_CHUNK, Q_BLOCK), jnp.float32)
        for h0 in range(0, IDX_HEADS, HEADS_PER_SLAB):
            cols = slice(h0 * Q_BLOCK, (h0 + HEADS_PER_SLAB) * Q_BLOCK)
            rel = _dot(kd, qidx_ref[:, cols])
            for t in range(HEADS_PER_SLAB):
                h = h0 + t
                acc = acc + w_t[AUX_W_OFF + h:AUX_W_OFF + h + 1, :] * jnp.maximum(rel[:, t * Q_BLOCK:(t + 1) * Q_BLOCK], 0.0)
        bits = lax.bitcast_convert_type(acc, jnp.int32)
        key = bits ^ ((bits >> 31) & 0x7FFFFFFF)
        key = jnp.where(c * KEY_CHUNK + krow <= qpos, key, INT_MIN)
        sc_ref[c] = key
        hi_ref[c] = (key >> 16).astype(jnp.int16)
        lo_ref[c] = ((key & 0xFFFF) - HALF).astype(jnp.int16)
        return carry

    for_chunk_pairs(score_chunk, 0)

    @pl.when(odd)
    def _():
        sc_ref[n_chunks] = jnp.full((KEY_CHUNK, Q_BLOCK), INT_MIN, jnp.int32)
        hi_ref[n_chunks] = jnp.full((KEY_CHUNK, Q_BLOCK), -HALF, jnp.int16)
        lo_ref[n_chunks] = jnp.full((KEY_CHUNK, Q_BLOCK), -HALF, jnp.int16)

    def count(pred):
        def body(c, acc):
            hit = jnp.where(pred(sc_ref[c], c), 1, 0)
            return acc + jnp.sum(hit.reshape(KEY_CHUNK // 32, 32, Q_BLOCK), axis=0)
        acc = for_chunk_pairs(body, jnp.zeros((32, Q_BLOCK), jnp.int32), padded=True)
        return jnp.sum(acc, axis=0, keepdims=True)

    def count16(ref16, pred):
        def body(c, acc):
            hit = jnp.where(pred(ref16[c]), jnp.int16(1), jnp.int16(0))
            words = pltpu.bitcast(hit, jnp.int32)
            return acc + jnp.sum(words.reshape(KEY_CHUNK // 64, 32, Q_BLOCK), axis=0)
        acc = for_chunk_pairs(body, jnp.zeros((32, Q_BLOCK), jnp.int32), padded=True)
        acc = jnp.sum(acc, axis=0, keepdims=True)
        return (acc & 0xFFFF) + (acc >> 16)

    def bisect16(ref16, target, n_bits):
        def bit(b, t):
            cand = t + lax.shift_left(jnp.int32(1), 15 - b)
            cand16 = cand.astype(jnp.int16)
            return jnp.where(count16(ref16, lambda v: v >= cand16) >= target, cand, t)
        return lax.fori_loop(0, n_bits, bit, jnp.full((1, Q_BLOCK), -HALF, jnp.int32))

    all_selected = (i + 1) * Q_BLOCK <= topk
    n_bits = jnp.where(all_selected, 0, 16)
    t_hi = bisect16(hi_ref, topk, n_bits)
    t_hi16 = t_hi.astype(jnp.int16)
    above = count16(hi_ref, lambda v: v > t_hi16)

    def low_chunk(c, carry):
        lo2_ref[c] = jnp.where(hi_ref[c] == t_hi16, lo_ref[c], jnp.int16(-HALF))
        return carry

    for_chunk_pairs(low_chunk, 0, padded=True)
    t_lo = bisect16(lo2_ref, topk - above, n_bits)
    thr = lax.shift_left(t_hi, 16) + (t_lo + HALF)

    need = topk - count(lambda k, c: k > thr)
    n_eq = count(lambda k, c: k == thr)
    excess = jnp.max(jnp.where(n_eq > need, 1, 0)) > 0
    idx_bits = (seq - 1).bit_length()

    def tie_bound():
        def bit(b, bound):
            cand = bound + lax.shift_left(jnp.int32(1), idx_bits - 1 - b)
            below = count(lambda k, c: jnp.where(k == thr, c * KEY_CHUNK + krow, seq) < cand)
            return jnp.where(below < need, cand, bound)
        return lax.fori_loop(0, idx_bits, bit, jnp.zeros((1, Q_BLOCK), jnp.int32))

    tie_idx = lax.cond(jnp.logical_and(excess, jnp.logical_not(all_selected)), tie_bound,
                       lambda: jnp.where(all_selected, -1, seq) + jnp.zeros((1, Q_BLOCK), jnp.int32))

    def bias_chunk(c, carry):
        k = sc_ref[c]
        tie_ok = jnp.where(c * KEY_CHUNK + krow <= tie_idx, 0.0, NEG)
        bias_ref[c] = jnp.where(k > thr, 0.0, jnp.where(k == thr, tie_ok, NEG))
        return carry

    for_chunk_pairs(bias_chunk, 0, padded=True)

    m_ref[...] = jnp.full(m_ref.shape, NEG, jnp.float32)
    acc_ref[...] = jnp.zeros(acc_ref.shape, jnp.float32)

    slabs = A_HEADS // HEADS_PER_SLAB
    last_chunk = n_chunks - 1

    def scores(c, s_buf, mx_buf):
        c = jnp.minimum(c, last_chunk)
        kc = keys_ref[c]
        bias2 = jnp.concatenate([bias_ref[c]] * HEADS_PER_SLAB, axis=1)
        for pr in range(slabs):
            s = _dot(kc, qcat_ref[pr]) + bias2
            s_buf[pr] = s
            mx_buf[pr] = jnp.max(s, axis=0, keepdims=True)

    def accumulate(c, s_buf, mx_buf):
        vt = vt_ref[c]
        for pr in range(slabs):
            m_old = m_ref[pr]
            m_new = jnp.maximum(m_old, mx_buf[pr])
            alpha = jnp.exp2(m_old - m_new)
            p = jnp.exp2(s_buf[pr] - m_new)
            m_ref[pr] = m_new
            acc_ref[pr] = acc_ref[pr] * alpha + _dot(vt, _bf16(p))

    scores(0, sa_ref, mxa_ref)

    def attend_pair(t, carry):
        scores(2 * t + 1, sb_ref, mxb_ref)
        accumulate(2 * t, sa_ref, mxa_ref)
        scores(2 * t + 2, sa_ref, mxa_ref)
        accumulate(2 * t + 1, sb_ref, mxb_ref)
        return carry

    lax.fori_loop(0, n_pairs, attend_pair, 0)

    @pl.when(odd)
    def _():
        accumulate(n_chunks - 1, sa_ref, mxa_ref)

    for pr in range(slabs):
        o_t = acc_ref[pr, 0:A_LATENT, :] / acc_ref[pr, A_LATENT:A_LATENT + 1, :]
        for t in range(HEADS_PER_SLAB):
            h = pr * HEADS_PER_SLAB + t
            o_lat = _bf16(o_t[:, t * Q_BLOCK:(t + 1) * Q_BLOCK].T)
            o = _dot(o_lat, wuv_ref[h]) * gate_ref[:, h * A_VDIM:(h + 1) * A_VDIM]
            og_ref[:, h * A_VDIM:(h + 1) * A_VDIM] = _bf16(o)
    out_ref[...] = x_ref[...] + mod_ref[2:3, :] * _dot(og_ref[...], wout_ref[...])


def _attn_a(qcat, qidx, aux, keys, vt, kdup, gate, x2, mod, w_uv, w_out, topk):
    bsz, nq = qcat.shape[:2]
    seq = nq * Q_BLOCK
    d = x2.shape[1]
    nkc = seq // KEY_CHUNK
    row = lambda b, i: (b * nq + i, 0)
    slabs = A_HEADS // HEADS_PER_SLAB
    slab_w = HEADS_PER_SLAB * Q_BLOCK
    return pl.pallas_call(
        functools.partial(_attn_a_kernel, topk=topk),
        grid=(bsz, nq),
        in_specs=[pl.BlockSpec((None, None, slabs, A_QPAD, slab_w), lambda b, i: (b, i, 0, 0, 0)),
                  pl.BlockSpec((None, None, LANES, IDX_HEADS * Q_BLOCK), lambda b, i: (b, i, 0, 0)),
                  pl.BlockSpec((Q_BLOCK, LANES), row),
                  pl.BlockSpec((None, nkc, KEY_CHUNK, A_QPAD), lambda b, i: (b, 0, 0, 0)),
                  pl.BlockSpec((None, nkc, V_ROWS, KEY_CHUNK), lambda b, i: (b, 0, 0, 0)),
                  pl.BlockSpec((None, nkc, KEY_CHUNK, LANES), lambda b, i: (b, 0, 0, 0)),
                  pl.BlockSpec((Q_BLOCK, A_WIDTH), row),
                  pl.BlockSpec((Q_BLOCK, d), row),
                  pl.BlockSpec((None, 3, d), lambda b, i: (b, 0, 0)),
                  pl.BlockSpec((A_HEADS, A_LATENT, A_VDIM), lambda b, i: (0, 0, 0)),
                  pl.BlockSpec((A_WIDTH, d), lambda b, i: (0, 0))],
        out_specs=pl.BlockSpec((Q_BLOCK, d), row),
        out_shape=jax.ShapeDtypeStruct(x2.shape, jnp.float32),
        scratch_shapes=[pltpu.VMEM((nkc, KEY_CHUNK, Q_BLOCK), jnp.int32),
                        pltpu.VMEM((nkc, KEY_CHUNK, Q_BLOCK), jnp.int16),
                        pltpu.VMEM((nkc, KEY_CHUNK, Q_BLOCK), jnp.int16),
                        pltpu.VMEM((nkc, KEY_CHUNK, Q_BLOCK), jnp.int16),
                        pltpu.VMEM((nkc, KEY_CHUNK, Q_BLOCK), jnp.float32),
                        pltpu.VMEM((slabs, KEY_CHUNK, slab_w), jnp.float32),
                        pltpu.VMEM((slabs, KEY_CHUNK, slab_w), jnp.float32),
                        pltpu.VMEM((slabs, 1, slab_w), jnp.float32),
                        pltpu.VMEM((slabs, 1, slab_w), jnp.float32),
                        pltpu.VMEM((slabs, 1, slab_w), jnp.float32),
                        pltpu.VMEM((slabs, V_ROWS, slab_w), jnp.float32),
                        pltpu.VMEM((Q_BLOCK, A_WIDTH), jnp.bfloat16)],
        compiler_params=pltpu.CompilerParams(vmem_limit_bytes=VMEM_LIMIT),
        name="attn_a",
    )(qcat, qidx, aux, keys.reshape(bsz, nkc, KEY_CHUNK, A_QPAD), vt, kdup.reshape(bsz, nkc, KEY_CHUNK, LANES),
      gate, x2, mod, w_uv, w_out)


_PERM_DILS = tuple(dil for _, dil in B_GROUPS if dil > 1)


def _dilation_perms(tm):
    dst = jnp.arange(tm)
    perms = []
    for dil in _PERM_DILS:
        src = (dst % (tm // dil)) * dil + dst // (tm // dil)
        perms.append((src[:, None] == dst[None, :]).astype(jnp.bfloat16))
    return jnp.stack(perms)


_MATMUL_REORDER_DIL = 16


def _store_dilated(val, out_refs, tmp_ref, perm_ref):
    vb = _bf16(val)
    tm, width = vb.shape
    heads, _, hdim = tmp_ref.shape
    in_tmp = False
    for out_ref in out_refs:
        dil = out_ref.shape[1] // width
        rows = tm // dil
        if dil == 1:
            out_ref[...] = vb
        elif dil >= _MATMUL_REORDER_DIL:
            res = _bf16(_dot(perm_ref[_PERM_DILS.index(dil)], vb))
            for r in range(dil):
                out_ref[:, r * width:(r + 1) * width] = res[r * rows:(r + 1) * rows, :]
        else:
            if not in_tmp:
                for hd in range(heads):
                    tmp_ref[hd] = val[:, hd * hdim:(hd + 1) * hdim]
                in_tmp = True
            for r in range(dil):
                for hd in range(heads):
                    col = r * width + hd * hdim
                    out_ref[:, col:col + hdim] = _bf16(tmp_ref[hd, pl.ds(r, rows, stride=dil), :])


def _proj_b_kernel(*refs, with_kv):
    n_g = len(B_GROUPS)
    if with_kv:
        h_ref, mod_ref, g_ref, kvg_ref, wkv_ref, win_ref, cos_ref, sin_ref, perm_ref = refs[:9]
        outs = refs[9:]
        k_refs, v_refs, outs = outs[:n_g], outs[n_g:2 * n_g], outs[2 * n_g:]
    else:
        h_ref, mod_ref, g_ref, win_ref, cos_ref, sin_ref, perm_ref = refs[:7]
        outs = refs[7:]
    q_refs, gate_ref, tmp_ref = outs[:n_g], outs[n_g], outs[n_g + 1]
    h = h_ref[...]
    cos, sin = cos_ref[...], sin_ref[...]
    width = B_WIDTH

    def rope(t, scale):
        return jnp.concatenate([_rope_lanes(t[:, hd * B_HDIM:(hd + 1) * B_HDIM], cos, sin, B_HDIM) * scale
                                for hd in range(B_HEADS)], axis=1)

    if with_kv:
        kvn = _bf16(h * lax.rsqrt(jnp.mean(h * h, axis=-1, keepdims=True) + EPS) * kvg_ref[...])
        _store_dilated(rope(_dot(kvn, wkv_ref[:, 0:width]), 1.0), k_refs, tmp_ref, perm_ref)
        _store_dilated(_dot(kvn, wkv_ref[:, width:2 * width]), v_refs, tmp_ref, perm_ref)
    hn = _bf16(_prenorm(h, g_ref[...], mod_ref))
    qscale = B_HDIM ** -0.5 * LOG2E
    for gi, q_ref in enumerate(q_refs):
        _store_dilated(rope(_dot(hn, win_ref[:, gi * width:(gi + 1) * width]), qscale), [q_ref], tmp_ref, perm_ref)
    gate_ref[...] = _bf16(_silu(_dot(hn, win_ref[:, n_g * width:(n_g + 1) * width])))


def _proj_b(h2, mod, g, kvg, w_kv, w_in, cos, sin, bsz, seq, with_kv, tm=256):
    n, d = h2.shape
    tps = seq // tm
    row = lambda i: (i, 0)
    const = lambda i: (0, 0)
    in_specs = [pl.BlockSpec((tm, d), row), pl.BlockSpec((None, 3, d), lambda i: (i // tps, 0, 0)),
                pl.BlockSpec((1, d), const)]
    args = [h2, mod, g]
    if with_kv:
        in_specs += [pl.BlockSpec((1, d), const), pl.BlockSpec(w_kv.shape, const)]
        args += [kvg, w_kv]
    perms = _dilation_perms(tm)
    table = pl.BlockSpec((None, tm, LANES), lambda i: (ROPE_DIMS.index(B_HDIM), i, 0))
    in_specs += [pl.BlockSpec(w_in.shape, const), table, table, pl.BlockSpec(perms.shape, lambda i: (0, 0, 0))]
    args += [w_in, cos, sin, perms]
    dil_specs = [pl.BlockSpec((None, tm // dil, dil * B_WIDTH), lambda i: (i // tps, i % tps, 0)) for _, dil in B_GROUPS]
    dil_shapes = [jax.ShapeDtypeStruct((bsz, seq // dil, dil * B_WIDTH), jnp.bfloat16) for _, dil in B_GROUPS]
    n_sets = 3 if with_kv else 1
    return pl.pallas_call(
        functools.partial(_proj_b_kernel, with_kv=with_kv),
        grid=(n // tm,),
        in_specs=in_specs,
        out_specs=dil_specs * n_sets + [pl.BlockSpec((tm, B_WIDTH), row)],
        out_shape=dil_shapes * n_sets + [jax.ShapeDtypeStruct((n, B_WIDTH), jnp.bfloat16)],
        scratch_shapes=[pltpu.VMEM((B_HEADS, tm, B_HDIM), jnp.float32)],
        compiler_params=pltpu.CompilerParams(vmem_limit_bytes=VMEM_LIMIT),
        name="proj_b",
    )(*args)


def _dilated_kernel(q_ref, kc_ref, vc_ref, o_ref, lse_ref, kp_ref, vp_ref):
    j = pl.program_id(2)
    w = q_ref.shape[0]

    @pl.when(j == 0)
    def _():
        kp_ref[...] = jnp.zeros(kp_ref.shape, kp_ref.dtype)
        vp_ref[...] = jnp.zeros(vp_ref.shape, vp_ref.dtype)

    qi = lax.broadcasted_iota(jnp.int32, (w, w), 0)
    ki = lax.broadcasted_iota(jnp.int32, (w, w), 1)
    bias = jnp.concatenate([jnp.where(jnp.logical_and(ki >= qi, j > 0), 0.0, NEG),
                            jnp.where(ki <= qi, 0.0, NEG)], axis=1)
    ones = jnp.ones((2 * w, B_HDIM), jnp.bfloat16)
    lses = []
    for hd in range(B_HEADS):
        sl = slice(hd * B_HDIM, (hd + 1) * B_HDIM)
        keys = jnp.concatenate([kp_ref[:, sl], kc_ref[:, sl]], axis=0)
        vals = jnp.concatenate([vp_ref[:, sl], vc_ref[:, sl]], axis=0)
        s = _dot_nt(q_ref[:, sl], keys) + bias
        m = jnp.max(s, axis=-1, keepdims=True)
        p = _bf16(jnp.exp2(s - m))
        ol = _dot(p, jnp.concatenate([vals, ones], axis=1))
        l = ol[:, B_HDIM:]
        o_ref[:, sl] = _bf16(ol[:, :B_HDIM] / l)
        lses.append(m + jnp.log2(l[:, 0:1]))
    lse_ref[...] = jnp.concatenate(lses + [jnp.zeros((w, LANES - B_HEADS), jnp.float32)], axis=1)
    kp_ref[...] = kc_ref[...]
    vp_ref[...] = vc_ref[...]


def _dilated(q, k, v, window, dil):
    bsz, n, _ = q.shape
    w = window // dil
    nb = n // w
    cur = lambda b, r, j: (b, j, r)
    blk = pl.BlockSpec((None, w, B_WIDTH), cur)
    return pl.pallas_call(
        _dilated_kernel,
        grid=(bsz, dil, nb),
        in_specs=[blk, blk, blk],
        out_specs=[blk, pl.BlockSpec((None, None, w, LANES), lambda b, r, j: (b, r, j, 0))],
        out_shape=[jax.ShapeDtypeStruct(q.shape, jnp.bfloat16),
                   jax.ShapeDtypeStruct((bsz, dil, n, LANES), jnp.float32)],
        scratch_shapes=[pltpu.VMEM((w, B_WIDTH), jnp.bfloat16), pltpu.VMEM((w, B_WIDTH), jnp.bfloat16)],
        compiler_params=pltpu.CompilerParams(dimension_semantics=("arbitrary", "arbitrary", "arbitrary")),
        name=f"dilated_{dil}",
    )(q, k, v)


def _merge_kernel(*refs, final):
    n_g = len(B_GROUPS)
    o_refs, lse_refs = refs[:n_g], refs[n_g:2 * n_g]
    gate_ref, h_ref, mod_ref, wout_ref, fin_ref, unperm_ref, out_ref, ls_ref, og_ref = refs[2 * n_g:]
    tm = h_ref.shape[0]
    o_seq = []
    for gi, (_, dil) in enumerate(B_GROUPS):
        rows = tm // dil
        for r in range(dil):
            dst = pl.ds(r, rows, stride=dil) if dil > 1 else slice(None)
            ls_ref[gi, dst, :] = lse_refs[gi][r]
        if dil == 1:
            o_seq.append(o_refs[gi][...].astype(jnp.float32))
        else:
            stacked = jnp.concatenate([o_refs[gi][:, r * B_WIDTH:(r + 1) * B_WIDTH] for r in range(dil)], axis=0)
            o_seq.append(_dot(unperm_ref[_PERM_DILS.index(dil)], stacked))
    lses = [ls_ref[gi] for gi in range(n_g)]
    m = functools.reduce(jnp.maximum, lses)
    e = [jnp.exp2(t - m) for t in lses]
    inv = 1.0 / functools.reduce(lambda a, b: a + b, e)
    for hd in range(B_HEADS):
        sl = slice(hd * B_HDIM, (hd + 1) * B_HDIM)
        o = functools.reduce(lambda a, b: a + b,
                             [(e[gi] * inv)[:, hd:hd + 1] * o_seq[gi][:, sl] for gi in range(n_g)])
        og_ref[:, sl] = _bf16(o * gate_ref[:, sl])
    h = h_ref[...] + mod_ref[2:3, :] * _dot(og_ref[...], wout_ref[...])
    if final:
        h = h * lax.rsqrt(jnp.mean(h * h, axis=-1, keepdims=True) + EPS) * fin_ref[...]
    out_ref[...] = h


def _merge(os_, lses, gate, h2, mod, w_out, fin, seq, final, tm=256):
    n, d = h2.shape
    tps = seq // tm
    row = lambda i: (i, 0)
    rows = pl.BlockSpec((tm, d), row)
    n_g = len(B_GROUPS)
    o_specs = [pl.BlockSpec((None, tm // dil, dil * B_WIDTH), lambda i: (i // tps, i % tps, 0)) for _, dil in B_GROUPS]
    lse_specs = [pl.BlockSpec((None, dil, tm // dil, LANES), lambda i: (i // tps, 0, i % tps, 0)) for _, dil in B_GROUPS]
    unperms = jnp.swapaxes(_dilation_perms(tm), 1, 2)
    return pl.pallas_call(
        functools.partial(_merge_kernel, final=final),
        grid=(n // tm,),
        in_specs=o_specs + lse_specs + [rows, rows, pl.BlockSpec((None, 3, d), lambda i: (i // tps, 0, 0)),
                                        pl.BlockSpec(w_out.shape, lambda i: (0, 0)),
                                        pl.BlockSpec((1, d), lambda i: (0, 0)),
                                        pl.BlockSpec(unperms.shape, lambda i: (0, 0, 0))],
        out_specs=rows,
        out_shape=jax.ShapeDtypeStruct((n, d), jnp.float32),
        scratch_shapes=[pltpu.VMEM((n_g, tm, LANES), jnp.float32),
                        pltpu.VMEM((tm, B_WIDTH), jnp.bfloat16)],
        compiler_params=pltpu.CompilerParams(vmem_limit_bytes=VMEM_LIMIT),
        name="merge_out",
    )(*os_, *lses, gate, h2, mod, w_out, fin, unperms)


def _final_norm_kernel(h_ref, g_ref, out_ref):
    h = h_ref[...]
    out_ref[...] = h * lax.rsqrt(jnp.mean(h * h, axis=-1, keepdims=True) + EPS) * g_ref[...]


def kernel(x, c, positions, a_norm, a_ada_w, a_ada_b, a_w_in, a_kv_norm, a_w_uv, a_w_out, kv_norm, w_kv, b_norm, b_ada_w, b_ada_b, b_w_in, b_w_out, final_norm):
    bsz, seq, d = x.shape
    n = bsz * seq
    n_a, n_b = a_norm.shape[0], b_norm.shape[0]
    topk = min(TOPK_MAX, seq // 4)
    assert topk % Q_BLOCK == 0 and seq % KEY_CHUNK == 0 and d == B_WIDTH
    assert all(seq % window == 0 for window, _ in B_GROUPS)

    cos, sin = _rope_tables(positions, ROPE_DIMS)
    h = x.reshape(n, d)
    for li in range(n_a):
        mod = _ada(c, a_ada_w[li], a_ada_b[li])
        qcat, keys, vt, kdup, qidx, aux, gate = _proj_a(
            h, mod, a_norm[li].reshape(1, d), a_kv_norm[li].reshape(1, A_LATENT), _a_weight(a_w_in[li]),
            cos, sin, bsz, seq)
        h = _attn_a(qcat, qidx, aux, keys, vt, kdup, gate, h, mod, _bf16(a_w_uv[li]), _bf16(a_w_out[li]), topk)
    k_sh = v_sh = None
    for li in range(n_b):
        mod = _ada(c, b_ada_w[li], b_ada_b[li])
        outs = _proj_b(h, mod, b_norm[li].reshape(1, d), kv_norm.reshape(1, d), _bf16(w_kv), _bf16(b_w_in[li]),
                       cos, sin, bsz, seq, with_kv=(li == 0))
        n_g = len(B_GROUPS)
        if li == 0:
            k_sh, v_sh = outs[:n_g], outs[n_g:2 * n_g]
            outs = outs[2 * n_g:]
        q_groups, gate = outs[:n_g], outs[n_g]
        res = [_dilated(qg, kg, vg, window, dil)
               for qg, kg, vg, (window, dil) in zip(q_groups, k_sh, v_sh, B_GROUPS)]
        h = _merge([r[0] for r in res], [r[1] for r in res], gate, h, mod, _bf16(b_w_out[li]),
                   final_norm.reshape(1, d), seq, final=(li == n_b - 1))
    if n_b == 0:
        h = pl.pallas_call(
            _final_norm_kernel, grid=(n // 256,),
            in_specs=[pl.BlockSpec((256, d), lambda i: (i, 0)), pl.BlockSpec((1, d), lambda i: (0, 0))],
            out_specs=pl.BlockSpec((256, d), lambda i: (i, 0)),
            out_shape=jax.ShapeDtypeStruct((n, d), jnp.float32), name="final_norm",
        )(h, final_norm.reshape(1, d))
    return h.reshape(bsz, seq, d)
```

```python
import functools
import math

import jax
import jax.numpy as jnp
from jax import lax
from jax.experimental import pallas as pl
from jax.experimental.pallas import tpu as pltpu

ROPE_THETA = 10000.0
EPS = 1e-6
LOG2E = 1.4426950408889634

A_HEADS = 16
A_LATENT = 128
A_ROPE = 32
A_VDIM = 128
A_WIDTH = A_HEADS * A_VDIM
IDX_HEADS = 8
IDX_DIM = 64
TOPK_MAX = 256
Q_BLOCK = 128
KEY_CHUNK = 256
A_QPAD = 256
HEADS_PER_SLAB = 2
V_ROWS = A_LATENT + 16

B_HEADS = 8
B_HDIM = 128
B_GROUPS = ((128, 1), (512, 4), (2048, 16))
B_WIDTH = B_HEADS * B_HDIM
ROPE_DIMS = (A_ROPE, IDX_DIM, B_HDIM)

LANES = 128
NEG = -1e30
INT_MIN = -(2 ** 31)
HALF = 2 ** 15
VMEM_LIMIT = 56 * 1024 * 1024

AUX_W_OFF = A_ROPE


def _bf16(t):
    return t.astype(jnp.bfloat16)


def _dot(a, b):
    return jnp.dot(a, b, preferred_element_type=jnp.float32)


def _dot_nt(a, b):
    return lax.dot_general(a, b, (((1,), (1,)), ((), ())), preferred_element_type=jnp.float32)


def _silu(t):
    return t * (1.0 / (1.0 + jnp.exp(-t)))


def _lane_iota(shape):
    return lax.broadcasted_iota(jnp.int32, shape, len(shape) - 1)


def _ada_kernel(c_ref, w_ref, b_ref, out_ref):
    out_ref[...] = _dot(_bf16(_silu(c_ref[...])), _bf16(w_ref[...])) + b_ref[...]


def _ada(c, w, b):
    bsz, d = c.shape
    out = pl.pallas_call(
        _ada_kernel,
        grid=(3,),
        in_specs=[pl.BlockSpec((bsz, d), lambda j: (0, 0)),
                  pl.BlockSpec((d, d), lambda j: (0, j)),
                  pl.BlockSpec((1, d), lambda j: (0, j))],
        out_specs=pl.BlockSpec((bsz, d), lambda j: (0, j)),
        out_shape=jax.ShapeDtypeStruct((bsz, 3 * d), jnp.float32),
        name="ada_mod",
    )(c, w, b.reshape(1, 3 * d))
    return out.reshape(bsz, 3, d)


def _rope_table_kernel(pos_ref, inv_ref, sel_ref, sign_ref, cos_ref, sin_ref):
    half_rows = pos_ref.shape[0] // 2
    upper = _lane_iota((half_rows, LANES)) >= LANES // 2
    ang = jnp.where(upper, pos_ref[half_rows:, :], pos_ref[:half_rows, :]) * inv_ref[...]
    cos, sin = jnp.cos(ang), jnp.sin(ang)
    for t in range(cos_ref.shape[0]):
        for part in range(2):
            pick = lambda v: jnp.dot(v, sel_ref[part, t], precision=lax.Precision.HIGHEST,
                                     preferred_element_type=jnp.float32)
            rows = slice(part * half_rows, (part + 1) * half_rows)
            cos_ref[t, rows, :] = pick(cos)
            sin_ref[t, rows, :] = pick(sin) * sign_ref[t]


def _rope_tables(positions, dims, tm=512):
    n = positions.size
    base = max(dims)
    assert base == LANES and all(base % d == 0 and d & (d - 1) == 0 for d in dims)
    posf = jnp.broadcast_to(positions.reshape(n, 1).astype(jnp.float32), (n, LANES))
    inv = ROPE_THETA ** (-jnp.arange(0, base, 2, dtype=jnp.float32) / base)
    lane = jnp.arange(LANES)
    inv = inv[lane % (base // 2)].reshape(1, LANES)
    src = [((lane % d) % (d // 2)) * (base // d) for d in dims]
    sel = jnp.stack([jnp.stack([(lane[:, None] == (s + part * (base // 2))[None, :]).astype(jnp.float32) for s in src])
                     for part in range(2)])
    sign = jnp.stack([jnp.where(lane % d < d // 2, -1.0, 1.0) for d in dims]).astype(jnp.float32)
    sign = sign.reshape(len(dims), 1, LANES)
    shape = jax.ShapeDtypeStruct((len(dims), n, LANES), jnp.float32)
    return pl.pallas_call(
        _rope_table_kernel,
        grid=(n // tm,),
        in_specs=[pl.BlockSpec((tm, LANES), lambda i: (i, 0)),
                  pl.BlockSpec((1, LANES), lambda i: (0, 0)),
                  pl.BlockSpec(sel.shape, lambda i: (0, 0, 0, 0)),
                  pl.BlockSpec(sign.shape, lambda i: (0, 0, 0))],
        out_specs=[pl.BlockSpec((len(dims), tm, LANES), lambda i: (0, i, 0))] * 2,
        out_shape=[shape, shape],
        name="rope_tables",
    )(posf, inv, sel, sign)


def _rope_lanes(xv, cos, sin_signed, dim):
    half = dim // 2
    if dim == LANES:
        partner = pltpu.roll(xv, half, axis=1)
    else:
        first = (_lane_iota(xv.shape) % dim) < half
        partner = jnp.where(first, pltpu.roll(xv, LANES - half, axis=1), pltpu.roll(xv, half, axis=1))
    return xv * cos + partner * sin_signed


def _prenorm(h, g, mod_ref):
    y = h * lax.rsqrt(jnp.mean(h * h, axis=-1, keepdims=True) + EPS) * g
    return y * (1.0 + mod_ref[1:2, :]) + mod_ref[0:1, :]


_A_COLS = {}
_o = 0
for _name, _w in (("q_lat", A_HEADS * A_LATENT), ("q_rope", A_HEADS * A_ROPE), ("c_kv", A_LATENT),
                  ("aux", LANES), ("kdup", LANES), ("gate", A_WIDTH), ("q_idx", IDX_HEADS * IDX_DIM)):
    _A_COLS[_name] = (_o, _o + _w)
    _o += _w
A_COLS_TOTAL = _o


def _a_weight(w_in):
    sizes = (A_HEADS * A_LATENT, A_HEADS * A_ROPE, A_LATENT, A_ROPE, A_WIDTH, IDX_HEADS * IDX_DIM, IDX_DIM, IDX_HEADS)
    parts, o = [], 0
    for s in sizes:
        parts.append(w_in[:, o:o + s])
        o += s
    q_lat, q_rope, c_kv, k_rope, gate, q_idx, k_idx, w_idx = parts
    d = w_in.shape[0]
    aux = jnp.concatenate([k_rope, w_idx, jnp.zeros((d, LANES - A_ROPE - IDX_HEADS), w_in.dtype)], axis=1)
    kdup = jnp.concatenate([k_idx, k_idx], axis=1)
    return _bf16(jnp.concatenate([q_lat, q_rope, c_kv, aux, kdup, gate, q_idx], axis=1))


def _proj_a_kernel(x_ref, mod_ref, g_ref, kvg_ref, w_ref, cos_ref, sin_ref,
                   qcat_ref, keys_ref, vt_ref, kdup_ref, qidx_ref, aux_ref, gate_ref):
    hn = _bf16(_prenorm(x_ref[...], g_ref[...], mod_ref))
    cos32, sin32 = cos_ref[0], sin_ref[0]
    cos64, sin64 = cos_ref[1], sin_ref[1]
    lane = _lane_iota(cos32.shape)
    qscale = (A_LATENT + A_ROPE) ** -0.5 * LOG2E

    def proj(name):
        lo, hi = _A_COLS[name]
        return _dot(hn, w_ref[:, lo:hi])

    q_blocks = x_ref.shape[0] // Q_BLOCK
    q_lat = proj("q_lat") * qscale
    q_rope = proj("q_rope")
    per_slab = LANES // A_ROPE
    for h in range(A_HEADS):
        qcat_ref[h, :, 0:A_LATENT] = _bf16(q_lat[:, h * A_LATENT:(h + 1) * A_LATENT])
    for j in range(A_HEADS // per_slab):
        roped = _rope_lanes(q_rope[:, j * LANES:(j + 1) * LANES], cos32, sin32, A_ROPE) * qscale
        for t in range(per_slab):
            piece = roped if t == 0 else pltpu.roll(roped, LANES - A_ROPE * t, axis=1)
            qcat_ref[j * per_slab + t, :, A_LATENT:A_QPAD] = _bf16(jnp.where(lane < A_ROPE, piece, 0.0))

    c_kv = proj("c_kv")
    c_kv = c_kv * lax.rsqrt(jnp.mean(c_kv * c_kv, axis=-1, keepdims=True) + EPS) * kvg_ref[...]
    c_kv_t = _bf16(c_kv.T)
    keys_ref[0:A_LATENT, :] = c_kv_t
    vt_ref[0:A_LATENT, :] = c_kv_t
    vt_ref[A_LATENT:V_ROWS, :] = jnp.ones((V_ROWS - A_LATENT, vt_ref.shape[1]), jnp.bfloat16)
    aux = proj("aux")
    k_rope = _rope_lanes(aux, cos32, sin32, A_ROPE)
    keys_ref[A_LATENT:A_QPAD, :] = _bf16(jnp.where(lane < A_ROPE, k_rope, 0.0).T)
    aux_ref[...] = aux * IDX_HEADS ** -0.5

    kdup_ref[...] = _bf16(_rope_lanes(proj("kdup"), cos64, sin64, IDX_DIM))
    q_idx = proj("q_idx")
    per_slab_i = LANES // IDX_DIM
    feat = lax.broadcasted_iota(jnp.int32, (LANES, x_ref.shape[0]), 0)
    for j in range(IDX_HEADS // per_slab_i):
        roped_t = _rope_lanes(q_idx[:, j * LANES:(j + 1) * LANES], cos64, sin64, IDX_DIM).T
        for t in range(per_slab_i):
            keep = (feat >= t * IDX_DIM) & (feat < (t + 1) * IDX_DIM)
            head_t = _bf16(jnp.where(keep, roped_t, 0.0))
            h = j * per_slab_i + t
            for qb in range(q_blocks):
                qidx_ref[qb, :, h * Q_BLOCK:(h + 1) * Q_BLOCK] = head_t[:, qb * Q_BLOCK:(qb + 1) * Q_BLOCK]

    gate_ref[...] = _bf16(_silu(proj("gate")))


def _proj_a(x2, mod, g, kvg, w, cos, sin, bsz, seq):
    n, d = x2.shape
    tm = KEY_CHUNK
    tps = seq // tm
    row = lambda i: (i, 0)
    qbt = tm // Q_BLOCK
    nq = seq // Q_BLOCK
    return pl.pallas_call(
        _proj_a_kernel,
        grid=(n // tm,),
        in_specs=[pl.BlockSpec((tm, d), row),
                  pl.BlockSpec((None, 3, d), lambda i: (i // tps, 0, 0)),
                  pl.BlockSpec((1, d), lambda i: (0, 0)),
                  pl.BlockSpec((1, A_LATENT), lambda i: (0, 0)),
                  pl.BlockSpec((d, A_COLS_TOTAL), lambda i: (0, 0)),
                  pl.BlockSpec((2, tm, LANES), lambda i: (0, i, 0)),
                  pl.BlockSpec((2, tm, LANES), lambda i: (0, i, 0))],
        out_specs=[pl.BlockSpec((None, A_HEADS, tm, A_QPAD), lambda i: (i // tps, 0, i % tps, 0)),
                   pl.BlockSpec((None, None, A_QPAD, tm), lambda i: (i // tps, i % tps, 0, 0)),
                   pl.BlockSpec((None, None, V_ROWS, tm), lambda i: (i // tps, i % tps, 0, 0)),
                   pl.BlockSpec((tm, LANES), row),
                   pl.BlockSpec((None, qbt, LANES, IDX_HEADS * Q_BLOCK), lambda i: (i // tps, i % tps, 0, 0)),
                   pl.BlockSpec((tm, LANES), row),
                   pl.BlockSpec((tm, A_WIDTH), row)],
        out_shape=[jax.ShapeDtypeStruct((bsz, A_HEADS, seq, A_QPAD), jnp.bfloat16),
                   jax.ShapeDtypeStruct((bsz, seq // tm, A_QPAD, tm), jnp.bfloat16),
                   jax.ShapeDtypeStruct((bsz, seq // tm, V_ROWS, tm), jnp.bfloat16),
                   jax.ShapeDtypeStruct((n, LANES), jnp.bfloat16),
                   jax.ShapeDtypeStruct((bsz, nq, LANES, IDX_HEADS * Q_BLOCK), jnp.bfloat16),
                   jax.ShapeDtypeStruct((n, LANES), jnp.float32),
                   jax.ShapeDtypeStruct((n, A_WIDTH), jnp.bfloat16)],
        compiler_params=pltpu.CompilerParams(vmem_limit_bytes=VMEM_LIMIT),
        name="proj_a",
    )(x2, mod, g, kvg, w, cos, sin)


def _attn_a_kernel(qcat_ref, qidx_ref, aux_ref, keys_ref, vt_ref, kdup_ref, gate_ref, x_ref, mod_ref, wuv_ref,
                   wout_ref, out_ref, sc_ref, hi_ref, lo_ref, lo2_ref, bias_ref, sa_ref, sb_ref, mxa_ref, mxb_ref, m_ref, acc_ref, og_ref, *, topk):
    i = pl.program_id(1)
    n_chunks = (i + 2) // 2
    seq = keys_ref.shape[0] * KEY_CHUNK

    w_t = aux_ref[...].T
    qpos = i * Q_BLOCK + lax.broadcasted_iota(jnp.int32, (KEY_CHUNK, Q_BLOCK), 1)
    krow = lax.broadcasted_iota(jnp.int32, (KEY_CHUNK, Q_BLOCK), 0)

    n_pairs = n_chunks // 2
    odd = n_chunks % 2 == 1

    def for_chunk_pairs(fn, init, padded=False):
        def body(t, carry):
            return fn(2 * t + 1, fn(2 * t, carry))
        if padded:
            return lax.fori_loop(0, (n_chunks + 1) // 2, body, init)
        carry = lax.fori_loop(0, n_pairs, body, init)
        return lax.cond(odd, lambda cr: fn(n_chunks - 1, cr), lambda cr: cr, carry)

    def score_chunk(c, carry):
        kd = kdup_ref[c]
        acc = jnp.zeros((KEY_CHUNK, Q_BLOCK), jnp.float32)
        for h0 in range(0, IDX_HEADS, HEADS_PER_SLAB):
            cols = slice(h0 * Q_BLOCK, (h0 + HEADS_PER_SLAB) * Q_BLOCK)
            rel = _dot(kd, qidx_ref[:, cols])
            for t in range(HEADS_PER_SLAB):
                h = h0 + t
                acc = acc + w_t[AUX_W_OFF + h:AUX_W_OFF + h + 1, :] * jnp.maximum(rel[:, t * Q_BLOCK:(t + 1) * Q_BLOCK], 0.0)
        bits = lax.bitcast_convert_type(acc, jnp.int32)
        key = bits ^ ((bits >> 31) & 0x7FFFFFFF)
        key = jnp.where(c * KEY_CHUNK + krow <= qpos, key, INT_MIN)
        sc_ref[c] = key
        hi_ref[c] = (key >> 16).astype(jnp.int16)
        lo_ref[c] = ((key & 0xFFFF) - HALF).astype(jnp.int16)
        return carry

    for_chunk_pairs(score_chunk, 0)

    @pl.when(odd)
    def _():
        sc_ref[n_chunks] = jnp.full((KEY_CHUNK, Q_BLOCK), INT_MIN, jnp.int32)
        hi_ref[n_chunks] = jnp.full((KEY_CHUNK, Q_BLOCK), -HALF, jnp.int16)
        lo_ref[n_chunks] = jnp.full((KEY_CHUNK, Q_BLOCK), -HALF, jnp.int16)

    def count(pred):
        def body(c, acc):
            hit = jnp.where(pred(sc_ref[c], c), 1, 0)
            return acc + jnp.sum(hit.reshape(KEY_CHUNK // 32, 32, Q_BLOCK), axis=0)
        acc = for_chunk_pairs(body, jnp.zeros((32, Q_BLOCK), jnp.int32), padded=True)
        return jnp.sum(acc, axis=0, keepdims=True)

    def count16(ref16, pred):
        def body(c, acc):
            hit = jnp.where(pred(ref16[c]), jnp.int16(1), jnp.int16(0))
            words = pltpu.bitcast(hit, jnp.int32)
            return acc + jnp.sum(words.reshape(KEY_CHUNK // 64, 32, Q_BLOCK), axis=0)
        acc = for_chunk_pairs(body, jnp.zeros((32, Q_BLOCK), jnp.int32), padded=True)
        acc = jnp.sum(acc, axis=0, keepdims=True)
        return (acc & 0xFFFF) + (acc >> 16)

    def bisect16(ref16, target, n_bits):
        def bit(b, t):
            cand = t + lax.shift_left(jnp.int32(1), 15 - b)
            cand16 = cand.astype(jnp.int16)
            return jnp.where(count16(ref16, lambda v: v >= cand16) >= target, cand, t)
        return lax.fori_loop(0, n_bits, bit, jnp.full((1, Q_BLOCK), -HALF, jnp.int32))

    all_selected = (i + 1) * Q_BLOCK <= topk
    n_bits = jnp.where(all_selected, 0, 16)
    t_hi = bisect16(hi_ref, topk, n_bits)
    t_hi16 = t_hi.astype(jnp.int16)
    above = count16(hi_ref, lambda v: v > t_hi16)

    def low_chunk(c, carry):
        lo2_ref[c] = jnp.where(hi_ref[c] == t_hi16, lo_ref[c], jnp.int16(-HALF))
        return carry

    for_chunk_pairs(low_chunk, 0, padded=True)
    t_lo = bisect16(lo2_ref, topk - above, n_bits)
    thr = lax.shift_left(t_hi, 16) + (t_lo + HALF)

    need = topk - count(lambda k, c: k > thr)
    n_eq = count(lambda k, c: k == thr)
    excess = jnp.max(jnp.where(n_eq > need, 1, 0)) > 0
    idx_bits = (seq - 1).bit_length()

    def tie_bound():
        def bit(b, bound):
            cand = bound + lax.shift_left(jnp.int32(1), idx_bits - 1 - b)
            below = count(lambda k, c: jnp.where(k == thr, c * KEY_CHUNK + krow, seq) < cand)
            return jnp.where(below < need, cand, bound)
        return lax.fori_loop(0, idx_bits, bit, jnp.zeros((1, Q_BLOCK), jnp.int32))

    tie_idx = lax.cond(jnp.logical_and(excess, jnp.logical_not(all_selected)), tie_bound,
                       lambda: jnp.where(all_selected, -1, seq) + jnp.zeros((1, Q_BLOCK), jnp.int32))

    def bias_chunk(c, carry):
        k = sc_ref[c]
        tie_ok = jnp.where(c * KEY_CHUNK + krow <= tie_idx, 0.0, NEG)
        bias_ref[c] = jnp.where(k > thr, 0.0, jnp.where(k == thr, tie_ok, NEG))
        return carry

    for_chunk_pairs(bias_chunk, 0, padded=True)

    m_ref[...] = jnp.full(m_ref.shape, NEG, jnp.float32)
    acc_ref[...] = jnp.zeros(acc_ref.shape, jnp.float32)

    slabs = A_HEADS // HEADS_PER_SLAB
    last_chunk = n_chunks - 1

    def scores(c, s_buf, mx_buf):
        c = jnp.minimum(c, last_chunk)
        s_all = _dot(qcat_ref[...].reshape(A_HEADS * Q_BLOCK, A_QPAD), keys_ref[c])
        bias2 = jnp.concatenate([bias_ref[c]] * HEADS_PER_SLAB, axis=1)
        rows = HEADS_PER_SLAB * Q_BLOCK
        for pr in range(slabs):
            s = s_all[pr * rows:(pr + 1) * rows, :].T + bias2
            s_buf[pr] = s
            mx_buf[pr] = jnp.max(s, axis=0, keepdims=True)

    def accumulate(c, s_buf, mx_buf):
        vt = vt_ref[c]
        for pr in range(slabs):
            m_old = m_ref[pr]
            m_new = jnp.maximum(m_old, mx_buf[pr])
            alpha = jnp.exp2(m_old - m_new)
            p = jnp.exp2(s_buf[pr] - m_new)
            m_ref[pr] = m_new
            acc_ref[pr] = acc_ref[pr] * alpha + _dot(vt, _bf16(p))

    scores(0, sa_ref, mxa_ref)

    def attend_pair(t, carry):
        scores(2 * t + 1, sb_ref, mxb_ref)
        accumulate(2 * t, sa_ref, mxa_ref)
        scores(2 * t + 2, sa_ref, mxa_ref)
        accumulate(2 * t + 1, sb_ref, mxb_ref)
        return carry

    lax.fori_loop(0, n_pairs, attend_pair, 0)

    @pl.when(odd)
    def _():
        accumulate(n_chunks - 1, sa_ref, mxa_ref)

    for pr in range(slabs):
        o_t = acc_ref[pr, 0:A_LATENT, :] / acc_ref[pr, A_LATENT:A_LATENT + 1, :]
        for t in range(HEADS_PER_SLAB):
            h = pr * HEADS_PER_SLAB + t
            o_lat = _bf16(o_t[:, t * Q_BLOCK:(t + 1) * Q_BLOCK].T)
            o = _dot(o_lat, wuv_ref[h]) * gate_ref[:, h * A_VDIM:(h + 1) * A_VDIM]
            og_ref[:, h * A_VDIM:(h + 1) * A_VDIM] = _bf16(o)
    out_ref[...] = x_ref[...] + mod_ref[2:3, :] * _dot(og_ref[...], wout_ref[...])


def _attn_a(qcat, qidx, aux, keys, vt, kdup, gate, x2, mod, w_uv, w_out, topk):
    bsz, _, seq, _ = qcat.shape
    nq = seq // Q_BLOCK
    d = x2.shape[1]
    nkc = seq // KEY_CHUNK
    row = lambda b, i: (b * nq + i, 0)
    slabs = A_HEADS // HEADS_PER_SLAB
    slab_w = HEADS_PER_SLAB * Q_BLOCK
    return pl.pallas_call(
        functools.partial(_attn_a_kernel, topk=topk),
        grid=(bsz, nq),
        in_specs=[pl.BlockSpec((None, A_HEADS, Q_BLOCK, A_QPAD), lambda b, i: (b, 0, i, 0)),
                  pl.BlockSpec((None, None, LANES, IDX_HEADS * Q_BLOCK), lambda b, i: (b, i, 0, 0)),
                  pl.BlockSpec((Q_BLOCK, LANES), row),
                  pl.BlockSpec((None, nkc, A_QPAD, KEY_CHUNK), lambda b, i: (b, 0, 0, 0)),
                  pl.BlockSpec((None, nkc, V_ROWS, KEY_CHUNK), lambda b, i: (b, 0, 0, 0)),
                  pl.BlockSpec((None, nkc, KEY_CHUNK, LANES), lambda b, i: (b, 0, 0, 0)),
                  pl.BlockSpec((Q_BLOCK, A_WIDTH), row),
                  pl.BlockSpec((Q_BLOCK, d), row),
                  pl.BlockSpec((None, 3, d), lambda b, i: (b, 0, 0)),
                  pl.BlockSpec((A_HEADS, A_LATENT, A_VDIM), lambda b, i: (0, 0, 0)),
                  pl.BlockSpec((A_WIDTH, d), lambda b, i: (0, 0))],
        out_specs=pl.BlockSpec((Q_BLOCK, d), row),
        out_shape=jax.ShapeDtypeStruct(x2.shape, jnp.float32),
        scratch_shapes=[pltpu.VMEM((nkc, KEY_CHUNK, Q_BLOCK), jnp.int32),
                        pltpu.VMEM((nkc, KEY_CHUNK, Q_BLOCK), jnp.int16),
                        pltpu.VMEM((nkc, KEY_CHUNK, Q_BLOCK), jnp.int16),
                        pltpu.VMEM((nkc, KEY_CHUNK, Q_BLOCK), jnp.int16),
                        pltpu.VMEM((nkc, KEY_CHUNK, Q_BLOCK), jnp.float32),
                        pltpu.VMEM((slabs, KEY_CHUNK, slab_w), jnp.float32),
                        pltpu.VMEM((slabs, KEY_CHUNK, slab_w), jnp.float32),
                        pltpu.VMEM((slabs, 1, slab_w), jnp.float32),
                        pltpu.VMEM((slabs, 1, slab_w), jnp.float32),
                        pltpu.VMEM((slabs, 1, slab_w), jnp.float32),
                        pltpu.VMEM((slabs, V_ROWS, slab_w), jnp.float32),
                        pltpu.VMEM((Q_BLOCK, A_WIDTH), jnp.bfloat16)],
        compiler_params=pltpu.CompilerParams(vmem_limit_bytes=VMEM_LIMIT),
        name="attn_a",
    )(qcat, qidx, aux, keys, vt, kdup.reshape(bsz, nkc, KEY_CHUNK, LANES),
      gate, x2, mod, w_uv, w_out)


_PERM_DILS = tuple(dil for _, dil in B_GROUPS if dil > 1)


def _dilation_perms(tm):
    dst = jnp.arange(tm)
    perms = []
    for dil in _PERM_DILS:
        src = (dst % (tm // dil)) * dil + dst // (tm // dil)
        perms.append((src[:, None] == dst[None, :]).astype(jnp.bfloat16))
    return jnp.stack(perms)


_MATMUL_REORDER_DIL = 16


def _store_dilated(val, out_refs, tmp_ref, perm_ref):
    vb = _bf16(val)
    tm, width = vb.shape
    heads, _, hdim = tmp_ref.shape
    in_tmp = False
    for out_ref in out_refs:
        dil = out_ref.shape[1] // width
        rows = tm // dil
        if dil == 1:
            out_ref[...] = vb
        elif dil >= _MATMUL_REORDER_DIL:
            res = _bf16(_dot(perm_ref[_PERM_DILS.index(dil)], vb))
            for r in range(dil):
                out_ref[:, r * width:(r + 1) * width] = res[r * rows:(r + 1) * rows, :]
        else:
            if not in_tmp:
                for hd in range(heads):
                    tmp_ref[hd] = val[:, hd * hdim:(hd + 1) * hdim]
                in_tmp = True
            for r in range(dil):
                for hd in range(heads):
                    col = r * width + hd * hdim
                    out_ref[:, col:col + hdim] = _bf16(tmp_ref[hd, pl.ds(r, rows, stride=dil), :])


def _proj_b_kernel(*refs, with_kv):
    n_g = len(B_GROUPS)
    if with_kv:
        h_ref, mod_ref, g_ref, kvg_ref, wkv_ref, win_ref, cos_ref, sin_ref, perm_ref = refs[:9]
        outs = refs[9:]
        k_refs, v_refs, outs = outs[:n_g], outs[n_g:2 * n_g], outs[2 * n_g:]
    else:
        h_ref, mod_ref, g_ref, win_ref, cos_ref, sin_ref, perm_ref = refs[:7]
        outs = refs[7:]
    q_refs, gate_ref, tmp_ref = outs[:n_g], outs[n_g], outs[n_g + 1]
    h = h_ref[...]
    cos, sin = cos_ref[...], sin_ref[...]
    width = B_WIDTH

    def rope(t, scale):
        return jnp.concatenate([_rope_lanes(t[:, hd * B_HDIM:(hd + 1) * B_HDIM], cos, sin, B_HDIM) * scale
                                for hd in range(B_HEADS)], axis=1)

    if with_kv:
        kvn = _bf16(h * lax.rsqrt(jnp.mean(h * h, axis=-1, keepdims=True) + EPS) * kvg_ref[...])
        _store_dilated(rope(_dot(kvn, wkv_ref[:, 0:width]), 1.0), k_refs, tmp_ref, perm_ref)
        _store_dilated(_dot(kvn, wkv_ref[:, width:2 * width]), v_refs, tmp_ref, perm_ref)
    hn = _bf16(_prenorm(h, g_ref[...], mod_ref))
    qscale = B_HDIM ** -0.5 * LOG2E
    for gi, q_ref in enumerate(q_refs):
        _store_dilated(rope(_dot(hn, win_ref[:, gi * width:(gi + 1) * width]), qscale), [q_ref], tmp_ref, perm_ref)
    gate_ref[...] = _bf16(_silu(_dot(hn, win_ref[:, n_g * width:(n_g + 1) * width])))


def _proj_b(h2, mod, g, kvg, w_kv, w_in, cos, sin, bsz, seq, with_kv, tm=256):
    n, d = h2.shape
    tps = seq // tm
    row = lambda i: (i, 0)
    const = lambda i: (0, 0)
    in_specs = [pl.BlockSpec((tm, d), row), pl.BlockSpec((None, 3, d), lambda i: (i // tps, 0, 0)),
                pl.BlockSpec((1, d), const)]
    args = [h2, mod, g]
    if with_kv:
        in_specs += [pl.BlockSpec((1, d), const), pl.BlockSpec(w_kv.shape, const)]
        args += [kvg, w_kv]
    perms = _dilation_perms(tm)
    table = pl.BlockSpec((None, tm, LANES), lambda i: (ROPE_DIMS.index(B_HDIM), i, 0))
    in_specs += [pl.BlockSpec(w_in.shape, const), table, table, pl.BlockSpec(perms.shape, lambda i: (0, 0, 0))]
    args += [w_in, cos, sin, perms]
    dil_specs = [pl.BlockSpec((None, tm // dil, dil * B_WIDTH), lambda i: (i // tps, i % tps, 0)) for _, dil in B_GROUPS]
    dil_shapes = [jax.ShapeDtypeStruct((bsz, seq // dil, dil * B_WIDTH), jnp.bfloat16) for _, dil in B_GROUPS]
    n_sets = 3 if with_kv else 1
    return pl.pallas_call(
        functools.partial(_proj_b_kernel, with_kv=with_kv),
        grid=(n // tm,),
        in_specs=in_specs,
        out_specs=dil_specs * n_sets + [pl.BlockSpec((tm, B_WIDTH), row)],
        out_shape=dil_shapes * n_sets + [jax.ShapeDtypeStruct((n, B_WIDTH), jnp.bfloat16)],
        scratch_shapes=[pltpu.VMEM((B_HEADS, tm, B_HDIM), jnp.float32)],
        compiler_params=pltpu.CompilerParams(vmem_limit_bytes=VMEM_LIMIT),
        name="proj_b",
    )(*args)


def _dilated_kernel(q_ref, kc_ref, vc_ref, o_ref, lse_ref, kp_ref, vp_ref):
    j = pl.program_id(2)
    w = q_ref.shape[0]

    @pl.when(j == 0)
    def _():
        kp_ref[...] = jnp.zeros(kp_ref.shape, kp_ref.dtype)
        vp_ref[...] = jnp.zeros(vp_ref.shape, vp_ref.dtype)

    qi = lax.broadcasted_iota(jnp.int32, (w, w), 0)
    ki = lax.broadcasted_iota(jnp.int32, (w, w), 1)
    bias = jnp.concatenate([jnp.where(jnp.logical_and(ki >= qi, j > 0), 0.0, NEG),
                            jnp.where(ki <= qi, 0.0, NEG)], axis=1)
    ones = jnp.ones((2 * w, B_HDIM), jnp.bfloat16)
    lses = []
    for hd in range(B_HEADS):
        sl = slice(hd * B_HDIM, (hd + 1) * B_HDIM)
        keys = jnp.concatenate([kp_ref[:, sl], kc_ref[:, sl]], axis=0)
        vals = jnp.concatenate([vp_ref[:, sl], vc_ref[:, sl]], axis=0)
        s = _dot_nt(q_ref[:, sl], keys) + bias
        m = jnp.max(s, axis=-1, keepdims=True)
        p = _bf16(jnp.exp2(s - m))
        ol = _dot(p, jnp.concatenate([vals, ones], axis=1))
        l = ol[:, B_HDIM:]
        o_ref[:, sl] = _bf16(ol[:, :B_HDIM] / l)
        lses.append(m + jnp.log2(l[:, 0:1]))
    lse_ref[...] = jnp.concatenate(lses + [jnp.zeros((w, LANES - B_HEADS), jnp.float32)], axis=1)
    kp_ref[...] = kc_ref[...]
    vp_ref[...] = vc_ref[...]


def _dilated(q, k, v, window, dil):
    bsz, n, _ = q.shape
    w = window // dil
    nb = n // w
    cur = lambda b, r, j: (b, j, r)
    blk = pl.BlockSpec((None, w, B_WIDTH), cur)
    return pl.pallas_call(
        _dilated_kernel,
        grid=(bsz, dil, nb),
        in_specs=[blk, blk, blk],
        out_specs=[blk, pl.BlockSpec((None, None, w, LANES), lambda b, r, j: (b, r, j, 0))],
        out_shape=[jax.ShapeDtypeStruct(q.shape, jnp.bfloat16),
                   jax.ShapeDtypeStruct((bsz, dil, n, LANES), jnp.float32)],
        scratch_shapes=[pltpu.VMEM((w, B_WIDTH), jnp.bfloat16), pltpu.VMEM((w, B_WIDTH), jnp.bfloat16)],
        compiler_params=pltpu.CompilerParams(dimension_semantics=("arbitrary", "arbitrary", "arbitrary")),
        name=f"dilated_{dil}",
    )(q, k, v)


def _merge_kernel(*refs, final):
    n_g = len(B_GROUPS)
    o_refs, lse_refs = refs[:n_g], refs[n_g:2 * n_g]
    gate_ref, h_ref, mod_ref, wout_ref, fin_ref, unperm_ref, out_ref, ls_ref, og_ref = refs[2 * n_g:]
    tm = h_ref.shape[0]
    o_seq = []
    for gi, (_, dil) in enumerate(B_GROUPS):
        rows = tm // dil
        for r in range(dil):
            dst = pl.ds(r, rows, stride=dil) if dil > 1 else slice(None)
            ls_ref[gi, dst, :] = lse_refs[gi][r]
        if dil == 1:
            o_seq.append(o_refs[gi][...].astype(jnp.float32))
        else:
            stacked = jnp.concatenate([o_refs[gi][:, r * B_WIDTH:(r + 1) * B_WIDTH] for r in range(dil)], axis=0)
            o_seq.append(_dot(unperm_ref[_PERM_DILS.index(dil)], stacked))
    lses = [ls_ref[gi] for gi in range(n_g)]
    m = functools.reduce(jnp.maximum, lses)
    e = [jnp.exp2(t - m) for t in lses]
    inv = 1.0 / functools.reduce(lambda a, b: a + b, e)
    for hd in range(B_HEADS):
        sl = slice(hd * B_HDIM, (hd + 1) * B_HDIM)
        o = functools.reduce(lambda a, b: a + b,
                             [(e[gi] * inv)[:, hd:hd + 1] * o_seq[gi][:, sl] for gi in range(n_g)])
        og_ref[:, sl] = _bf16(o * gate_ref[:, sl])
    h = h_ref[...] + mod_ref[2:3, :] * _dot(og_ref[...], wout_ref[...])
    if final:
        h = h * lax.rsqrt(jnp.mean(h * h, axis=-1, keepdims=True) + EPS) * fin_ref[...]
    out_ref[...] = h


def _merge(os_, lses, gate, h2, mod, w_out, fin, seq, final, tm=256):
    n, d = h2.shape
    tps = seq // tm
    row = lambda i: (i, 0)
    rows = pl.BlockSpec((tm, d), row)
    n_g = len(B_GROUPS)
    o_specs = [pl.BlockSpec((None, tm // dil, dil * B_WIDTH), lambda i: (i // tps, i % tps, 0)) for _, dil in B_GROUPS]
    lse_specs = [pl.BlockSpec((None, dil, tm // dil, LANES), lambda i: (i // tps, 0, i % tps, 0)) for _, dil in B_GROUPS]
    unperms = jnp.swapaxes(_dilation_perms(tm), 1, 2)
    return pl.pallas_call(
        functools.partial(_merge_kernel, final=final),
        grid=(n // tm,),
        in_specs=o_specs + lse_specs + [rows, rows, pl.BlockSpec((None, 3, d), lambda i: (i // tps, 0, 0)),
                                        pl.BlockSpec(w_out.shape, lambda i: (0, 0)),
                                        pl.BlockSpec((1, d), lambda i: (0, 0)),
                                        pl.BlockSpec(unperms.shape, lambda i: (0, 0, 0))],
        out_specs=rows,
        out_shape=jax.ShapeDtypeStruct((n, d), jnp.float32),
        scratch_shapes=[pltpu.VMEM((n_g, tm, LANES), jnp.float32),
                        pltpu.VMEM((tm, B_WIDTH), jnp.bfloat16)],
        compiler_params=pltpu.CompilerParams(vmem_limit_bytes=VMEM_LIMIT),
        name="merge_out",
    )(*os_, *lses, gate, h2, mod, w_out, fin, unperms)


def _final_norm_kernel(h_ref, g_ref, out_ref):
    h = h_ref[...]
    out_ref[...] = h * lax.rsqrt(jnp.mean(h * h, axis=-1, keepdims=True) + EPS) * g_ref[...]


def kernel(x, c, positions, a_norm, a_ada_w, a_ada_b, a_w_in, a_kv_norm, a_w_uv, a_w_out, kv_norm, w_kv, b_norm, b_ada_w, b_ada_b, b_w_in, b_w_out, final_norm):
    bsz, seq, d = x.shape
    n = bsz * seq
    n_a, n_b = a_norm.shape[0], b_norm.shape[0]
    topk = min(TOPK_MAX, seq // 4)
    assert topk % Q_BLOCK == 0 and seq % KEY_CHUNK == 0 and d == B_WIDTH
    assert all(seq % window == 0 for window, _ in B_GROUPS)

    cos, sin = _rope_tables(positions, ROPE_DIMS)
    h = x.reshape(n, d)
    for li in range(n_a):
        mod = _ada(c, a_ada_w[li], a_ada_b[li])
        qcat, keys, vt, kdup, qidx, aux, gate = _proj_a(
            h, mod, a_norm[li].reshape(1, d), a_kv_norm[li].reshape(1, A_LATENT), _a_weight(a_w_in[li]),
            cos, sin, bsz, seq)
        h = _attn_a(qcat, qidx, aux, keys, vt, kdup, gate, h, mod, _bf16(a_w_uv[li]), _bf16(a_w_out[li]), topk)
    k_sh = v_sh = None
    for li in range(n_b):
        mod = _ada(c, b_ada_w[li], b_ada_b[li])
        outs = _proj_b(h, mod, b_norm[li].reshape(1, d), kv_norm.reshape(1, d), _bf16(w_kv), _bf16(b_w_in[li]),
                       cos, sin, bsz, seq, with_kv=(li == 0))
        n_g = len(B_GROUPS)
        if li == 0:
            k_sh, v_sh = outs[:n_g], outs[n_g:2 * n_g]
            outs = outs[2 * n_g:]
        q_groups, gate = outs[:n_g], outs[n_g]
        res = [_dilated(qg, kg, vg, window, dil)
               for qg, kg, vg, (window, dil) in zip(q_groups, k_sh, v_sh, B_GROUPS)]
        h = _merge([r[0] for r in res], [r[1] for r in res], gate, h, mod, _bf16(b_w_out[li]),
                   final_norm.reshape(1, d), seq, final=(li == n_b - 1))
    if n_b == 0:
        h = pl.pallas_call(
            _final_norm_kernel, grid=(n // 256,),
            in_specs=[pl.BlockSpec((256, d), lambda i: (i, 0)), pl.BlockSpec((1, d), lambda i: (0, 0))],
            out_specs=pl.BlockSpec((256, d), lambda i: (i, 0)),
            out_shape=jax.ShapeDtypeStruct((n, d), jnp.float32), name="final_norm",
        )(h, final_norm.reshape(1, d))
    return h.reshape(bsz, seq, d)
```

```python
import functools
import math

import jax
import jax.numpy as jnp
from jax import lax
from jax.experimental import pallas as pl
from jax.experimental.pallas import tpu as pltpu

ROPE_THETA = 10000.0
EPS = 1e-6
LOG2E = 1.4426950408889634

A_HEADS = 16
A_LATENT = 128
A_ROPE = 32
A_VDIM = 128
A_WIDTH = A_HEADS * A_VDIM
IDX_HEADS = 8
IDX_DIM = 64
TOPK_MAX = 256
Q_BLOCK = 128
KEY_CHUNK = 256
A_QPAD = 256
HEADS_PER_SLAB = 2
V_ROWS = A_LATENT + 16

B_HEADS = 8
B_HDIM = 128
B_GROUPS = ((128, 1), (512, 4), (2048, 16))
B_WIDTH = B_HEADS * B_HDIM
ROPE_DIMS = (A_ROPE, IDX_DIM, B_HDIM)

LANES = 128
NEG = -1e30
INT_MIN = -(2 ** 31)
HALF = 2 ** 15
VMEM_LIMIT = 56 * 1024 * 1024

AUX_W_OFF = A_ROPE


def _bf16(t):
    return t.astype(jnp.bfloat16)


def _dot(a, b):
    return jnp.dot(a, b, preferred_element_type=jnp.float32)


def _dot_nt(a, b):
    return lax.dot_general(a, b, (((1,), (1,)), ((), ())), preferred_element_type=jnp.float32)


def _silu(t):
    return t * (1.0 / (1.0 + jnp.exp(-t)))


def _lane_iota(shape):
    return lax.broadcasted_iota(jnp.int32, shape, len(shape) - 1)


def _ada_kernel(c_ref, w_ref, b_ref, out_ref):
    out_ref[...] = _dot(_bf16(_silu(c_ref[...])), _bf16(w_ref[...])) + b_ref[...]


def _ada(c, w, b):
    bsz, d = c.shape
    out = pl.pallas_call(
        _ada_kernel,
        grid=(3,),
        in_specs=[pl.BlockSpec((bsz, d), lambda j: (0, 0)),
                  pl.BlockSpec((d, d), lambda j: (0, j)),
                  pl.BlockSpec((1, d), lambda j: (0, j))],
        out_specs=pl.BlockSpec((bsz, d), lambda j: (0, j)),
        out_shape=jax.ShapeDtypeStruct((bsz, 3 * d), jnp.float32),
        name="ada_mod",
    )(c, w, b.reshape(1, 3 * d))
    return out.reshape(bsz, 3, d)


def _rope_table_kernel(pos_ref, inv_ref, sel_ref, sign_ref, cos_ref, sin_ref):
    half_rows = pos_ref.shape[0] // 2
    upper = _lane_iota((half_rows, LANES)) >= LANES // 2
    ang = jnp.where(upper, pos_ref[half_rows:, :], pos_ref[:half_rows, :]) * inv_ref[...]
    cos, sin = jnp.cos(ang), jnp.sin(ang)
    for t in range(cos_ref.shape[0]):
        for part in range(2):
            pick = lambda v: jnp.dot(v, sel_ref[part, t], precision=lax.Precision.HIGHEST,
                                     preferred_element_type=jnp.float32)
            rows = slice(part * half_rows, (part + 1) * half_rows)
            cos_ref[t, rows, :] = pick(cos)
            sin_ref[t, rows, :] = pick(sin) * sign_ref[t]


def _rope_tables(positions, dims, tm=512):
    n = positions.size
    base = max(dims)
    assert base == LANES and all(base % d == 0 and d & (d - 1) == 0 for d in dims)
    posf = jnp.broadcast_to(positions.reshape(n, 1).astype(jnp.float32), (n, LANES))
    inv = ROPE_THETA ** (-jnp.arange(0, base, 2, dtype=jnp.float32) / base)
    lane = jnp.arange(LANES)
    inv = inv[lane % (base // 2)].reshape(1, LANES)
    src = [((lane % d) % (d // 2)) * (base // d) for d in dims]
    sel = jnp.stack([jnp.stack([(lane[:, None] == (s + part * (base // 2))[None, :]).astype(jnp.float32) for s in src])
                     for part in range(2)])
    sign = jnp.stack([jnp.where(lane % d < d // 2, -1.0, 1.0) for d in dims]).astype(jnp.float32)
    sign = sign.reshape(len(dims), 1, LANES)
    shape = jax.ShapeDtypeStruct((len(dims), n, LANES), jnp.float32)
    return pl.pallas_call(
        _rope_table_kernel,
        grid=(n // tm,),
        in_specs=[pl.BlockSpec((tm, LANES), lambda i: (i, 0)),
                  pl.BlockSpec((1, LANES), lambda i: (0, 0)),
                  pl.BlockSpec(sel.shape, lambda i: (0, 0, 0, 0)),
                  pl.BlockSpec(sign.shape, lambda i: (0, 0, 0))],
        out_specs=[pl.BlockSpec((len(dims), tm, LANES), lambda i: (0, i, 0))] * 2,
        out_shape=[shape, shape],
        name="rope_tables",
    )(posf, inv, sel, sign)


def _rope_lanes(xv, cos, sin_signed, dim):
    half = dim // 2
    if dim == LANES:
        partner = pltpu.roll(xv, half, axis=1)
    else:
        first = (_lane_iota(xv.shape) % dim) < half
        partner = jnp.where(first, pltpu.roll(xv, LANES - half, axis=1), pltpu.roll(xv, half, axis=1))
    return xv * cos + partner * sin_signed


def _prenorm(h, g, mod_ref):
    y = h * lax.rsqrt(jnp.mean(h * h, axis=-1, keepdims=True) + EPS) * g
    return y * (1.0 + mod_ref[1:2, :]) + mod_ref[0:1, :]


_A_COLS = {}
_o = 0
for _name, _w in (("q_lat", A_HEADS * A_LATENT), ("q_rope", A_HEADS * A_ROPE), ("c_kv", A_LATENT),
                  ("aux", LANES), ("kdup", LANES), ("gate", A_WIDTH), ("q_idx", IDX_HEADS * IDX_DIM)):
    _A_COLS[_name] = (_o, _o + _w)
    _o += _w
A_COLS_TOTAL = _o


def _a_weight(w_in):
    sizes = (A_HEADS * A_LATENT, A_HEADS * A_ROPE, A_LATENT, A_ROPE, A_WIDTH, IDX_HEADS * IDX_DIM, IDX_DIM, IDX_HEADS)
    parts, o = [], 0
    for s in sizes:
        parts.append(w_in[:, o:o + s])
        o += s
    q_lat, q_rope, c_kv, k_rope, gate, q_idx, k_idx, w_idx = parts
    d = w_in.shape[0]
    aux = jnp.concatenate([k_rope, w_idx, jnp.zeros((d, LANES - A_ROPE - IDX_HEADS), w_in.dtype)], axis=1)
    kdup = jnp.concatenate([k_idx, k_idx], axis=1)
    return _bf16(jnp.concatenate([q_lat, q_rope, c_kv, aux, kdup, gate, q_idx], axis=1))


def _proj_a_kernel(x_ref, mod_ref, g_ref, kvg_ref, w_ref, cos_ref, sin_ref,
                   qcat_ref, keys_ref, vt_ref, kdup_ref, qidx_ref, aux_ref, gate_ref):
    hn = _bf16(_prenorm(x_ref[...], g_ref[...], mod_ref))
    cos32, sin32 = cos_ref[0], sin_ref[0]
    cos64, sin64 = cos_ref[1], sin_ref[1]
    lane = _lane_iota(cos32.shape)
    qscale = (A_LATENT + A_ROPE) ** -0.5 * LOG2E

    def proj(name):
        lo, hi = _A_COLS[name]
        return _dot(hn, w_ref[:, lo:hi])

    q_blocks = x_ref.shape[0] // Q_BLOCK

    def put_q(h, row0, piece_t):
        slab, col0 = h // HEADS_PER_SLAB, (h % HEADS_PER_SLAB) * Q_BLOCK
        for qb in range(q_blocks):
            qcat_ref[qb, slab, row0:row0 + piece_t.shape[0], col0:col0 + Q_BLOCK] = _bf16(
                piece_t[:, qb * Q_BLOCK:(qb + 1) * Q_BLOCK])

    q_lat = proj("q_lat") * qscale
    q_rope = proj("q_rope")
    per_slab = LANES // A_ROPE
    pad = jnp.zeros((A_QPAD - A_LATENT - A_ROPE, x_ref.shape[0]), jnp.float32)
    for h in range(A_HEADS):
        put_q(h, 0, q_lat[:, h * A_LATENT:(h + 1) * A_LATENT].T)
    for j in range(A_HEADS // per_slab):
        roped_t = (_rope_lanes(q_rope[:, j * LANES:(j + 1) * LANES], cos32, sin32, A_ROPE) * qscale).T
        for t in range(per_slab):
            put_q(j * per_slab + t, A_LATENT, roped_t[t * A_ROPE:(t + 1) * A_ROPE, :])
            put_q(j * per_slab + t, A_LATENT + A_ROPE, pad)

    c_kv = proj("c_kv")
    c_kv = c_kv * lax.rsqrt(jnp.mean(c_kv * c_kv, axis=-1, keepdims=True) + EPS) * kvg_ref[...]
    keys_ref[:, 0:A_LATENT] = _bf16(c_kv)
    vt_ref[0:A_LATENT, :] = _bf16(c_kv.T)
    vt_ref[A_LATENT:V_ROWS, :] = jnp.ones((V_ROWS - A_LATENT, vt_ref.shape[1]), jnp.bfloat16)
    aux = proj("aux")
    k_rope = _rope_lanes(aux, cos32, sin32, A_ROPE)
    keys_ref[:, A_LATENT:A_QPAD] = _bf16(jnp.where(lane < A_ROPE, k_rope, 0.0))
    aux_ref[...] = aux * IDX_HEADS ** -0.5

    kdup_ref[...] = _bf16(_rope_lanes(proj("kdup"), cos64, sin64, IDX_DIM))
    q_idx = proj("q_idx")
    per_slab_i = LANES // IDX_DIM
    feat = lax.broadcasted_iota(jnp.int32, (LANES, x_ref.shape[0]), 0)
    for j in range(IDX_HEADS // per_slab_i):
        roped_t = _rope_lanes(q_idx[:, j * LANES:(j + 1) * LANES], cos64, sin64, IDX_DIM).T
        for t in range(per_slab_i):
            keep = (feat >= t * IDX_DIM) & (feat < (t + 1) * IDX_DIM)
            head_t = _bf16(jnp.where(keep, roped_t, 0.0))
            h = j * per_slab_i + t
            for qb in range(q_blocks):
                qidx_ref[qb, :, h * Q_BLOCK:(h + 1) * Q_BLOCK] = head_t[:, qb * Q_BLOCK:(qb + 1) * Q_BLOCK]

    gate_ref[...] = _bf16(_silu(proj("gate")))


def _proj_a(x2, mod, g, kvg, w, cos, sin, bsz, seq):
    n, d = x2.shape
    tm = KEY_CHUNK
    tps = seq // tm
    row = lambda i: (i, 0)
    qbt = tm // Q_BLOCK
    nq = seq // Q_BLOCK
    slabs, slab_w = A_HEADS // HEADS_PER_SLAB, HEADS_PER_SLAB * Q_BLOCK
    return pl.pallas_call(
        _proj_a_kernel,
        grid=(n // tm,),
        in_specs=[pl.BlockSpec((tm, d), row),
                  pl.BlockSpec((None, 3, d), lambda i: (i // tps, 0, 0)),
                  pl.BlockSpec((1, d), lambda i: (0, 0)),
                  pl.BlockSpec((1, A_LATENT), lambda i: (0, 0)),
                  pl.BlockSpec((d, A_COLS_TOTAL), lambda i: (0, 0)),
                  pl.BlockSpec((2, tm, LANES), lambda i: (0, i, 0)),
                  pl.BlockSpec((2, tm, LANES), lambda i: (0, i, 0))],
        out_specs=[pl.BlockSpec((None, qbt, slabs, A_QPAD, slab_w), lambda i: (i // tps, i % tps, 0, 0, 0)),
                   pl.BlockSpec((tm, A_QPAD), row),
                   pl.BlockSpec((None, None, V_ROWS, tm), lambda i: (i // tps, i % tps, 0, 0)),
                   pl.BlockSpec((tm, LANES), row),
                   pl.BlockSpec((None, qbt, LANES, IDX_HEADS * Q_BLOCK), lambda i: (i // tps, i % tps, 0, 0)),
                   pl.BlockSpec((tm, LANES), row),
                   pl.BlockSpec((tm, A_WIDTH), row)],
        out_shape=[jax.ShapeDtypeStruct((bsz, nq, slabs, A_QPAD, slab_w), jnp.bfloat16),
                   jax.ShapeDtypeStruct((n, A_QPAD), jnp.bfloat16),
                   jax.ShapeDtypeStruct((bsz, seq // tm, V_ROWS, tm), jnp.bfloat16),
                   jax.ShapeDtypeStruct((n, LANES), jnp.bfloat16),
                   jax.ShapeDtypeStruct((bsz, nq, LANES, IDX_HEADS * Q_BLOCK), jnp.bfloat16),
                   jax.ShapeDtypeStruct((n, LANES), jnp.float32),
                   jax.ShapeDtypeStruct((n, A_WIDTH), jnp.bfloat16)],
        compiler_params=pltpu.CompilerParams(vmem_limit_bytes=VMEM_LIMIT),
        name="proj_a",
    )(x2, mod, g, kvg, w, cos, sin)


def _attn_a_kernel(qcat_ref, qidx_ref, aux_ref, keys_ref, vt_ref, kdup_ref, gate_ref, x_ref, mod_ref, wuv_ref,
                   wout_ref, out_ref, sc_ref, hi_ref, lo_ref, lo2_ref, bias_ref, sa_ref, sb_ref, mxa_ref, mxb_ref, m_ref, acc_ref, og_ref, ra_ref, rb_ref, *, topk):
    i = pl.program_id(1)
    n_chunks = (i + 2) // 2
    seq = keys_ref.shape[0] * KEY_CHUNK

    w_t = aux_ref[...].T
    qi_all = qidx_ref[...]
    qpos = i * Q_BLOCK + lax.broadcasted_iota(jnp.int32, (KEY_CHUNK, Q_BLOCK), 1)
    krow = lax.broadcasted_iota(jnp.int32, (KEY_CHUNK, Q_BLOCK), 0)

    n_pairs = n_chunks // 2
    odd = n_chunks % 2 == 1

    def for_chunk_pairs(fn, init):
        def body(t, carry):
            return fn(2 * t + 1, fn(2 * t, carry))
        carry = lax.fori_loop(0, n_pairs, body, init)
        return lax.cond(odd, lambda cr: fn(n_chunks - 1, cr), lambda cr: cr, carry)

    def idx_products(c, rel_buf):
        c = jnp.minimum(c, n_chunks - 1)
        rel_buf[...] = _dot(kdup_ref[c], qi_all)

    def idx_keys(c, rel_buf):
        acc = jnp.zeros((KEY_CHUNK, Q_BLOCK), jnp.float32)
        for h in range(IDX_HEADS):
            acc = acc + w_t[AUX_W_OFF + h:AUX_W_OFF + h + 1, :] * jnp.maximum(rel_buf[:, h * Q_BLOCK:(h + 1) * Q_BLOCK], 0.0)
        bits = lax.bitcast_convert_type(acc, jnp.int32)
        key = bits ^ ((bits >> 31) & 0x7FFFFFFF)
        key = jnp.where(c * KEY_CHUNK + krow <= qpos, key, INT_MIN)
        sc_ref[c] = key
        hi_ref[c] = (key >> 16).astype(jnp.int16)
        lo_ref[c] = ((key & 0xFFFF) - HALF).astype(jnp.int16)

    idx_products(0, ra_ref)

    def idx_pair(t, carry):
        idx_products(2 * t + 1, rb_ref)
        idx_keys(2 * t, ra_ref)
        idx_products(2 * t + 2, ra_ref)
        idx_keys(2 * t + 1, rb_ref)
        return carry

    lax.fori_loop(0, n_pairs, idx_pair, 0)

    @pl.when(odd)
    def _():
        idx_keys(n_chunks - 1, ra_ref)

    def count(pred):
        def body(c, acc):
            hit = jnp.where(pred(sc_ref[c], c), 1, 0)
            return acc + jnp.sum(hit.reshape(KEY_CHUNK // 32, 32, Q_BLOCK), axis=0)
        acc = for_chunk_pairs(body, jnp.zeros((32, Q_BLOCK), jnp.int32))
        return jnp.sum(acc, axis=0, keepdims=True)

    def count16(ref16, pred):
        def body(c, acc):
            hit = jnp.where(pred(ref16[c]), jnp.int16(1), jnp.int16(0))
            words = pltpu.bitcast(hit, jnp.int32)
            return acc + jnp.sum(words.reshape(KEY_CHUNK // 64, 32, Q_BLOCK), axis=0)
        acc = for_chunk_pairs(body, jnp.zeros((32, Q_BLOCK), jnp.int32))
        acc = jnp.sum(acc, axis=0, keepdims=True)
        return (acc & 0xFFFF) + (acc >> 16)

    def bisect16(ref16, target, n_bits):
        def bit(b, t):
            cand = t + lax.shift_left(jnp.int32(1), 15 - b)
            cand16 = cand.astype(jnp.int16)
            return jnp.where(count16(ref16, lambda v: v >= cand16) >= target, cand, t)
        return lax.fori_loop(0, n_bits, bit, jnp.full((1, Q_BLOCK), -HALF, jnp.int32))

    all_selected = (i + 1) * Q_BLOCK <= topk
    n_bits = jnp.where(all_selected, 0, 16)
    t_hi = bisect16(hi_ref, topk, n_bits)
    t_hi16 = t_hi.astype(jnp.int16)
    above = count16(hi_ref, lambda v: v > t_hi16)

    def low_chunk(c, carry):
        lo2_ref[c] = jnp.where(hi_ref[c] == t_hi16, lo_ref[c], jnp.int16(-HALF))
        return carry

    for_chunk_pairs(low_chunk, 0)
    t_lo = bisect16(lo2_ref, topk - above, n_bits)
    thr = lax.shift_left(t_hi, 16) + (t_lo + HALF)

    need = topk - count(lambda k, c: k > thr)
    n_eq = count(lambda k, c: k == thr)
    excess = jnp.max(jnp.where(n_eq > need, 1, 0)) > 0
    idx_bits = (seq - 1).bit_length()

    def tie_bound():
        def bit(b, bound):
            cand = bound + lax.shift_left(jnp.int32(1), idx_bits - 1 - b)
            below = count(lambda k, c: jnp.where(k == thr, c * KEY_CHUNK + krow, seq) < cand)
            return jnp.where(below < need, cand, bound)
        return lax.fori_loop(0, idx_bits, bit, jnp.zeros((1, Q_BLOCK), jnp.int32))

    tie_idx = lax.cond(jnp.logical_and(excess, jnp.logical_not(all_selected)), tie_bound,
                       lambda: jnp.where(all_selected, -1, seq) + jnp.zeros((1, Q_BLOCK), jnp.int32))

    def bias_chunk(c, carry):
        k = sc_ref[c]
        tie_ok = jnp.where(c * KEY_CHUNK + krow <= tie_idx, 0.0, NEG)
        bias_ref[c] = jnp.where(k > thr, 0.0, jnp.where(k == thr, tie_ok, NEG))
        return carry

    for_chunk_pairs(bias_chunk, 0)

    m_ref[...] = jnp.full(m_ref.shape, NEG, jnp.float32)
    acc_ref[...] = jnp.zeros(acc_ref.shape, jnp.float32)

    slabs = A_HEADS // HEADS_PER_SLAB
    last_chunk = n_chunks - 1

    def scores(c, s_buf, mx_buf):
        c = jnp.minimum(c, last_chunk)
        kc = keys_ref[c]
        bias2 = jnp.concatenate([bias_ref[c]] * HEADS_PER_SLAB, axis=1)
        for pr in range(slabs):
            s = _dot(kc, qcat_ref[pr]) + bias2
            s_buf[pr] = s
            mx_buf[pr] = jnp.max(s, axis=0, keepdims=True)

    def accumulate(c, s_buf, mx_buf):
        vt = vt_ref[c]
        for pr in range(slabs):
            m_old = m_ref[pr]
            m_new = jnp.maximum(m_old, mx_buf[pr])
            alpha = jnp.exp2(m_old - m_new)
            p = jnp.exp2(s_buf[pr] - m_new)
            m_ref[pr] = m_new
            acc_ref[pr] = acc_ref[pr] * alpha + _dot(vt, _bf16(p))

    scores(0, sa_ref, mxa_ref)

    def attend_pair(t, carry):
        scores(2 * t + 1, sb_ref, mxb_ref)
        accumulate(2 * t, sa_ref, mxa_ref)
        scores(2 * t + 2, sa_ref, mxa_ref)
        accumulate(2 * t + 1, sb_ref, mxb_ref)
        return carry

    lax.fori_loop(0, n_pairs, attend_pair, 0)

    @pl.when(odd)
    def _():
        accumulate(n_chunks - 1, sa_ref, mxa_ref)

    for pr in range(slabs):
        o_t = acc_ref[pr, 0:A_LATENT, :] / acc_ref[pr, A_LATENT:A_LATENT + 1, :]
        for t in range(HEADS_PER_SLAB):
            h = pr * HEADS_PER_SLAB + t
            o_lat = _bf16(o_t[:, t * Q_BLOCK:(t + 1) * Q_BLOCK].T)
            o = _dot(o_lat, wuv_ref[h]) * gate_ref[:, h * A_VDIM:(h + 1) * A_VDIM]
            og_ref[:, h * A_VDIM:(h + 1) * A_VDIM] = _bf16(o)
    out_ref[...] = x_ref[...] + mod_ref[2:3, :] * _dot(og_ref[...], wout_ref[...])


def _attn_a(qcat, qidx, aux, keys, vt, kdup, gate, x2, mod, w_uv, w_out, topk):
    bsz, nq = qcat.shape[:2]
    seq = nq * Q_BLOCK
    d = x2.shape[1]
    nkc = seq // KEY_CHUNK
    row = lambda b, i: (b * nq + i, 0)
    slabs = A_HEADS // HEADS_PER_SLAB
    slab_w = HEADS_PER_SLAB * Q_BLOCK
    return pl.pallas_call(
        functools.partial(_attn_a_kernel, topk=topk),
        grid=(bsz, nq),
        in_specs=[pl.BlockSpec((None, None, slabs, A_QPAD, slab_w), lambda b, i: (b, i, 0, 0, 0)),
                  pl.BlockSpec((None, None, LANES, IDX_HEADS * Q_BLOCK), lambda b, i: (b, i, 0, 0)),
                  pl.BlockSpec((Q_BLOCK, LANES), row),
                  pl.BlockSpec((None, nkc, KEY_CHUNK, A_QPAD), lambda b, i: (b, 0, 0, 0)),
                  pl.BlockSpec((None, nkc, V_ROWS, KEY_CHUNK), lambda b, i: (b, 0, 0, 0)),
                  pl.BlockSpec((None, nkc, KEY_CHUNK, LANES), lambda b, i: (b, 0, 0, 0)),
                  pl.BlockSpec((Q_BLOCK, A_WIDTH), row),
                  pl.BlockSpec((Q_BLOCK, d), row),
                  pl.BlockSpec((None, 3, d), lambda b, i: (b, 0, 0)),
                  pl.BlockSpec((A_HEADS, A_LATENT, A_VDIM), lambda b, i: (0, 0, 0)),
                  pl.BlockSpec((A_WIDTH, d), lambda b, i: (0, 0))],
        out_specs=pl.BlockSpec((Q_BLOCK, d), row),
        out_shape=jax.ShapeDtypeStruct(x2.shape, jnp.float32),
        scratch_shapes=[pltpu.VMEM((nkc, KEY_CHUNK, Q_BLOCK), jnp.int32),
                        pltpu.VMEM((nkc, KEY_CHUNK, Q_BLOCK), jnp.int16),
                        pltpu.VMEM((nkc, KEY_CHUNK, Q_BLOCK), jnp.int16),
                        pltpu.VMEM((nkc, KEY_CHUNK, Q_BLOCK), jnp.int16),
                        pltpu.VMEM((nkc, KEY_CHUNK, Q_BLOCK), jnp.float32),
                        pltpu.VMEM((slabs, KEY_CHUNK, slab_w), jnp.float32),
                        pltpu.VMEM((slabs, KEY_CHUNK, slab_w), jnp.float32),
                        pltpu.VMEM((slabs, 1, slab_w), jnp.float32),
                        pltpu.VMEM((slabs, 1, slab_w), jnp.float32),
                        pltpu.VMEM((slabs, 1, slab_w), jnp.float32),
                        pltpu.VMEM((slabs, V_ROWS, slab_w), jnp.float32),
                        pltpu.VMEM((Q_BLOCK, A_WIDTH), jnp.bfloat16),
                        pltpu.VMEM((KEY_CHUNK, IDX_HEADS * Q_BLOCK), jnp.float32),
                        pltpu.VMEM((KEY_CHUNK, IDX_HEADS * Q_BLOCK), jnp.float32)],
        compiler_params=pltpu.CompilerParams(vmem_limit_bytes=VMEM_LIMIT),
        name="attn_a",
    )(qcat, qidx, aux, keys.reshape(bsz, nkc, KEY_CHUNK, A_QPAD), vt, kdup.reshape(bsz, nkc, KEY_CHUNK, LANES),
      gate, x2, mod, w_uv, w_out)


_PERM_DILS = tuple(dil for _, dil in B_GROUPS if dil > 1)


def _dilation_perms(tm):
    dst = jnp.arange(tm)
    perms = []
    for dil in _PERM_DILS:
        src = (dst % (tm // dil)) * dil + dst // (tm // dil)
        perms.append((src[:, None] == dst[None, :]).astype(jnp.bfloat16))
    return jnp.stack(perms)


_MATMUL_REORDER_DIL = 16


def _store_dilated(val, out_refs, tmp_ref, perm_ref):
    vb = _bf16(val)
    tm, width = vb.shape
    heads, _, hdim = tmp_ref.shape
    in_tmp = False
    for out_ref in out_refs:
        dil = out_ref.shape[1] // width
        rows = tm // dil
        if dil == 1:
            out_ref[...] = vb
        elif dil >= _MATMUL_REORDER_DIL:
            res = _bf16(_dot(perm_ref[_PERM_DILS.index(dil)], vb))
            for r in range(dil):
                out_ref[:, r * width:(r + 1) * width] = res[r * rows:(r + 1) * rows, :]
        else:
            if not in_tmp:
                for hd in range(heads):
                    tmp_ref[hd] = val[:, hd * hdim:(hd + 1) * hdim]
                in_tmp = True
            for r in range(dil):
                for hd in range(heads):
                    col = r * width + hd * hdim
                    out_ref[:, col:col + hdim] = _bf16(tmp_ref[hd, pl.ds(r, rows, stride=dil), :])


def _proj_b_kernel(*refs, with_kv):
    n_g = len(B_GROUPS)
    if with_kv:
        h_ref, mod_ref, g_ref, kvg_ref, wkv_ref, win_ref, cos_ref, sin_ref, perm_ref = refs[:9]
        outs = refs[9:]
        k_refs, v_refs, outs = outs[:n_g], outs[n_g:2 * n_g], outs[2 * n_g:]
    else:
        h_ref, mod_ref, g_ref, win_ref, cos_ref, sin_ref, perm_ref = refs[:7]
        outs = refs[7:]
    q_refs, gate_ref, tmp_ref = outs[:n_g], outs[n_g], outs[n_g + 1]
    h = h_ref[...]
    cos, sin = cos_ref[...], sin_ref[...]
    width = B_WIDTH

    def rope(t, scale):
        return jnp.concatenate([_rope_lanes(t[:, hd * B_HDIM:(hd + 1) * B_HDIM], cos, sin, B_HDIM) * scale
                                for hd in range(B_HEADS)], axis=1)

    if with_kv:
        kvn = _bf16(h * lax.rsqrt(jnp.mean(h * h, axis=-1, keepdims=True) + EPS) * kvg_ref[...])
        _store_dilated(rope(_dot(kvn, wkv_ref[:, 0:width]), 1.0), k_refs, tmp_ref, perm_ref)
        _store_dilated(_dot(kvn, wkv_ref[:, width:2 * width]), v_refs, tmp_ref, perm_ref)
    hn = _bf16(_prenorm(h, g_ref[...], mod_ref))
    qscale = B_HDIM ** -0.5 * LOG2E
    for gi, q_ref in enumerate(q_refs):
        _store_dilated(rope(_dot(hn, win_ref[:, gi * width:(gi + 1) * width]), qscale), [q_ref], tmp_ref, perm_ref)
    gate_ref[...] = _bf16(_silu(_dot(hn, win_ref[:, n_g * width:(n_g + 1) * width])))


def _proj_b(h2, mod, g, kvg, w_kv, w_in, cos, sin, bsz, seq, with_kv, tm=256):
    n, d = h2.shape
    tps = seq // tm
    row = lambda i: (i, 0)
    const = lambda i: (0, 0)
    in_specs = [pl.BlockSpec((tm, d), row), pl.BlockSpec((None, 3, d), lambda i: (i // tps, 0, 0)),
                pl.BlockSpec((1, d), const)]
    args = [h2, mod, g]
    if with_kv:
        in_specs += [pl.BlockSpec((1, d), const), pl.BlockSpec(w_kv.shape, const)]
        args += [kvg, w_kv]
    perms = _dilation_perms(tm)
    table = pl.BlockSpec((None, tm, LANES), lambda i: (ROPE_DIMS.index(B_HDIM), i, 0))
    in_specs += [pl.BlockSpec(w_in.shape, const), table, table, pl.BlockSpec(perms.shape, lambda i: (0, 0, 0))]
    args += [w_in, cos, sin, perms]
    dil_specs = [pl.BlockSpec((None, tm // dil, dil * B_WIDTH), lambda i: (i // tps, i % tps, 0)) for _, dil in B_GROUPS]
    dil_shapes = [jax.ShapeDtypeStruct((bsz, seq // dil, dil * B_WIDTH), jnp.bfloat16) for _, dil in B_GROUPS]
    n_sets = 3 if with_kv else 1
    return pl.pallas_call(
        functools.partial(_proj_b_kernel, with_kv=with_kv),
        grid=(n // tm,),
        in_specs=in_specs,
        out_specs=dil_specs * n_sets + [pl.BlockSpec((tm, B_WIDTH), row)],
        out_shape=dil_shapes * n_sets + [jax.ShapeDtypeStruct((n, B_WIDTH), jnp.bfloat16)],
        scratch_shapes=[pltpu.VMEM((B_HEADS, tm, B_HDIM), jnp.float32)],
        compiler_params=pltpu.CompilerParams(vmem_limit_bytes=VMEM_LIMIT),
        name="proj_b",
    )(*args)


def _dilated_kernel(q_ref, kc_ref, vc_ref, o_ref, lse_ref, kp_ref, vp_ref):
    j = pl.program_id(2)
    w = q_ref.shape[0]

    @pl.when(j == 0)
    def _():
        kp_ref[...] = jnp.zeros(kp_ref.shape, kp_ref.dtype)
        vp_ref[...] = jnp.zeros(vp_ref.shape, vp_ref.dtype)

    qi = lax.broadcasted_iota(jnp.int32, (w, w), 0)
    ki = lax.broadcasted_iota(jnp.int32, (w, w), 1)
    bias = jnp.concatenate([jnp.where(jnp.logical_and(ki >= qi, j > 0), 0.0, NEG),
                            jnp.where(ki <= qi, 0.0, NEG)], axis=1)
    ones = jnp.ones((2 * w, B_HDIM), jnp.bfloat16)
    lses = []
    for hd in range(B_HEADS):
        sl = slice(hd * B_HDIM, (hd + 1) * B_HDIM)
        keys = jnp.concatenate([kp_ref[:, sl], kc_ref[:, sl]], axis=0)
        vals = jnp.concatenate([vp_ref[:, sl], vc_ref[:, sl]], axis=0)
        s = _dot_nt(q_ref[:, sl], keys) + bias
        m = jnp.max(s, axis=-1, keepdims=True)
        p = _bf16(jnp.exp2(s - m))
        ol = _dot(p, jnp.concatenate([vals, ones], axis=1))
        l = ol[:, B_HDIM:]
        o_ref[:, sl] = _bf16(ol[:, :B_HDIM] / l)
        lses.append(m + jnp.log2(l[:, 0:1]))
    lse_ref[...] = jnp.concatenate(lses + [jnp.zeros((w, LANES - B_HEADS), jnp.float32)], axis=1)
    kp_ref[...] = kc_ref[...]
    vp_ref[...] = vc_ref[...]


def _dilated(q, k, v, window, dil):
    bsz, n, _ = q.shape
    w = window // dil
    nb = n // w
    cur = lambda b, r, j: (b, j, r)
    blk = pl.BlockSpec((None, w, B_WIDTH), cur)
    return pl.pallas_call(
        _dilated_kernel,
        grid=(bsz, dil, nb),
        in_specs=[blk, blk, blk],
        out_specs=[blk, pl.BlockSpec((None, None, w, LANES), lambda b, r, j: (b, r, j, 0))],
        out_shape=[jax.ShapeDtypeStruct(q.shape, jnp.bfloat16),
                   jax.ShapeDtypeStruct((bsz, dil, n, LANES), jnp.float32)],
        scratch_shapes=[pltpu.VMEM((w, B_WIDTH), jnp.bfloat16), pltpu.VMEM((w, B_WIDTH), jnp.bfloat16)],
        compiler_params=pltpu.CompilerParams(dimension_semantics=("arbitrary", "arbitrary", "arbitrary")),
        name=f"dilated_{dil}",
    )(q, k, v)


def _merge_kernel(*refs, final):
    n_g = len(B_GROUPS)
    o_refs, lse_refs = refs[:n_g], refs[n_g:2 * n_g]
    gate_ref, h_ref, mod_ref, wout_ref, fin_ref, unperm_ref, out_ref, ls_ref, og_ref = refs[2 * n_g:]
    tm = h_ref.shape[0]
    o_seq = []
    for gi, (_, dil) in enumerate(B_GROUPS):
        rows = tm // dil
        for r in range(dil):
            dst = pl.ds(r, rows, stride=dil) if dil > 1 else slice(None)
            ls_ref[gi, dst, :] = lse_refs[gi][r]
        if dil == 1:
            o_seq.append(o_refs[gi][...].astype(jnp.float32))
        else:
            stacked = jnp.concatenate([o_refs[gi][:, r * B_WIDTH:(r + 1) * B_WIDTH] for r in range(dil)], axis=0)
            o_seq.append(_dot(unperm_ref[_PERM_DILS.index(dil)], stacked))
    lses = [ls_ref[gi] for gi in range(n_g)]
    m = functools.reduce(jnp.maximum, lses)
    e = [jnp.exp2(t - m) for t in lses]
    inv = 1.0 / functools.reduce(lambda a, b: a + b, e)
    for hd in range(B_HEADS):
        sl = slice(hd * B_HDIM, (hd + 1) * B_HDIM)
        o = functools.reduce(lambda a, b: a + b,
                             [(e[gi] * inv)[:, hd:hd + 1] * o_seq[gi][:, sl] for gi in range(n_g)])
        og_ref[:, sl] = _bf16(o * gate_ref[:, sl])
    h = h_ref[...] + mod_ref[2:3, :] * _dot(og_ref[...], wout_ref[...])
    if final:
        h = h * lax.rsqrt(jnp.mean(h * h, axis=-1, keepdims=True) + EPS) * fin_ref[...]
    out_ref[...] = h


def _merge(os_, lses, gate, h2, mod, w_out, fin, seq, final, tm=256):
    n, d = h2.shape
    tps = seq // tm
    row = lambda i: (i, 0)
    rows = pl.BlockSpec((tm, d), row)
    n_g = len(B_GROUPS)
    o_specs = [pl.BlockSpec((None, tm // dil, dil * B_WIDTH), lambda i: (i // tps, i % tps, 0)) for _, dil in B_GROUPS]
    lse_specs = [pl.BlockSpec((None, dil, tm // dil, LANES), lambda i: (i // tps, 0, i % tps, 0)) for _, dil in B_GROUPS]
    unperms = jnp.swapaxes(_dilation_perms(tm), 1, 2)
    return pl.pallas_call(
        functools.partial(_merge_kernel, final=final),
        grid=(n // tm,),
        in_specs=o_specs + lse_specs + [rows, rows, pl.BlockSpec((None, 3, d), lambda i: (i // tps, 0, 0)),
                                        pl.BlockSpec(w_out.shape, lambda i: (0, 0)),
                                        pl.BlockSpec((1, d), lambda i: (0, 0)),
                                        pl.BlockSpec(unperms.shape, lambda i: (0, 0, 0))],
        out_specs=rows,
        out_shape=jax.ShapeDtypeStruct((n, d), jnp.float32),
        scratch_shapes=[pltpu.VMEM((n_g, tm, LANES), jnp.float32),
                        pltpu.VMEM((tm, B_WIDTH), jnp.bfloat16)],
        compiler_params=pltpu.CompilerParams(vmem_limit_bytes=VMEM_LIMIT),
        name="merge_out",
    )(*os_, *lses, gate, h2, mod, w_out, fin, unperms)


def _final_norm_kernel(h_ref, g_ref, out_ref):
    h = h_ref[...]
    out_ref[...] = h * lax.rsqrt(jnp.mean(h * h, axis=-1, keepdims=True) + EPS) * g_ref[...]


def kernel(x, c, positions, a_norm, a_ada_w, a_ada_b, a_w_in, a_kv_norm, a_w_uv, a_w_out, kv_norm, w_kv, b_norm, b_ada_w, b_ada_b, b_w_in, b_w_out, final_norm):
    bsz, seq, d = x.shape
    n = bsz * seq
    n_a, n_b = a_norm.shape[0], b_norm.shape[0]
    topk = min(TOPK_MAX, seq // 4)
    assert topk % Q_BLOCK == 0 and seq % KEY_CHUNK == 0 and d == B_WIDTH
    assert all(seq % window == 0 for window, _ in B_GROUPS)

    cos, sin = _rope_tables(positions, ROPE_DIMS)
    h = x.reshape(n, d)
    for li in range(n_a):
        mod = _ada(c, a_ada_w[li], a_ada_b[li])
        qcat, keys, vt, kdup, qidx, aux, gate = _proj_a(
            h, mod, a_norm[li].reshape(1, d), a_kv_norm[li].reshape(1, A_LATENT), _a_weight(a_w_in[li]),
            cos, sin, bsz, seq)
        h = _attn_a(qcat, qidx, aux, keys, vt, kdup, gate, h, mod, _bf16(a_w_uv[li]), _bf16(a_w_out[li]), topk)
    k_sh = v_sh = None
    for li in range(n_b):
        mod = _ada(c, b_ada_w[li], b_ada_b[li])
        outs = _proj_b(h, mod, b_norm[li].reshape(1, d), kv_norm.reshape(1, d), _bf16(w_kv), _bf16(b_w_in[li]),
                       cos, sin, bsz, seq, with_kv=(li == 0))
        n_g = len(B_GROUPS)
        if li == 0:
            k_sh, v_sh = outs[:n_g], outs[n_g:2 * n_g]
            outs = outs[2 * n_g:]
        q_groups, gate = outs[:n_g], outs[n_g]
        res = [_dilated(qg, kg, vg, window, dil)
               for qg, kg, vg, (window, dil) in zip(q_groups, k_sh, v_sh, B_GROUPS)]
        h = _merge([r[0] for r in res], [r[1] for r in res], gate, h, mod, _bf16(b_w_out[li]),
                   final_norm.reshape(1, d), seq, final=(li == n_b - 1))
    if n_b == 0:
        h = pl.pallas_call(
            _final_norm_kernel, grid=(n // 256,),
            in_specs=[pl.BlockSpec((256, d), lambda i: (i, 0)), pl.BlockSpec((1, d), lambda i: (0, 0))],
            out_specs=pl.BlockSpec((256, d), lambda i: (i, 0)),
            out_shape=jax.ShapeDtypeStruct((n, d), jnp.float32), name="final_norm",
        )(h, final_norm.reshape(1, d))
    return h.reshape(bsz, seq, d)
```

```python
import functools
import math

import jax
import jax.numpy as jnp
from jax import lax
from jax.experimental import pallas as pl
from jax.experimental.pallas import tpu as pltpu

ROPE_THETA = 10000.0
EPS = 1e-6
LOG2E = 1.4426950408889634

A_HEADS = 16
A_LATENT = 128
A_ROPE = 32
A_VDIM = 128
A_WIDTH = A_HEADS * A_VDIM
IDX_HEADS = 8
IDX_DIM = 64
TOPK_MAX = 256
Q_BLOCK = 256
KEY_CHUNK = 256
A_QPAD = 256
SLAB_LANES = 256
HEADS_PER_SLAB = SLAB_LANES // Q_BLOCK
V_ROWS = A_LATENT + 16

B_HEADS = 8
B_HDIM = 128
B_GROUPS = ((128, 1), (512, 4), (2048, 16))
B_WIDTH = B_HEADS * B_HDIM
ROPE_DIMS = (A_ROPE, IDX_DIM, B_HDIM)

LANES = 128
NEG = -1e30
INT_MIN = -(2 ** 31)
HALF = 2 ** 15
VMEM_LIMIT = 56 * 1024 * 1024

AUX_W_OFF = A_ROPE


def _bf16(t):
    return t.astype(jnp.bfloat16)


def _dot(a, b):
    return jnp.dot(a, b, preferred_element_type=jnp.float32)


def _dot_nt(a, b):
    return lax.dot_general(a, b, (((1,), (1,)), ((), ())), preferred_element_type=jnp.float32)


def _silu(t):
    return t * (1.0 / (1.0 + jnp.exp(-t)))


def _lane_iota(shape):
    return lax.broadcasted_iota(jnp.int32, shape, len(shape) - 1)


def _ada_kernel(c_ref, w_ref, b_ref, out_ref):
    out_ref[...] = _dot(_bf16(_silu(c_ref[...])), _bf16(w_ref[...])) + b_ref[...]


def _ada(c, w, b):
    bsz, d = c.shape
    out = pl.pallas_call(
        _ada_kernel,
        grid=(3,),
        in_specs=[pl.BlockSpec((bsz, d), lambda j: (0, 0)),
                  pl.BlockSpec((d, d), lambda j: (0, j)),
                  pl.BlockSpec((1, d), lambda j: (0, j))],
        out_specs=pl.BlockSpec((bsz, d), lambda j: (0, j)),
        out_shape=jax.ShapeDtypeStruct((bsz, 3 * d), jnp.float32),
        name="ada_mod",
    )(c, w, b.reshape(1, 3 * d))
    return out.reshape(bsz, 3, d)


def _rope_table_kernel(pos_ref, inv_ref, sel_ref, sign_ref, cos_ref, sin_ref):
    half_rows = pos_ref.shape[0] // 2
    upper = _lane_iota((half_rows, LANES)) >= LANES // 2
    ang = jnp.where(upper, pos_ref[half_rows:, :], pos_ref[:half_rows, :]) * inv_ref[...]
    cos, sin = jnp.cos(ang), jnp.sin(ang)
    for t in range(cos_ref.shape[0]):
        for part in range(2):
            pick = lambda v: jnp.dot(v, sel_ref[part, t], precision=lax.Precision.HIGHEST,
                                     preferred_element_type=jnp.float32)
            rows = slice(part * half_rows, (part + 1) * half_rows)
            cos_ref[t, rows, :] = pick(cos)
            sin_ref[t, rows, :] = pick(sin) * sign_ref[t]


def _rope_tables(positions, dims, tm=512):
    n = positions.size
    base = max(dims)
    assert base == LANES and all(base % d == 0 and d & (d - 1) == 0 for d in dims)
    posf = jnp.broadcast_to(positions.reshape(n, 1).astype(jnp.float32), (n, LANES))
    inv = ROPE_THETA ** (-jnp.arange(0, base, 2, dtype=jnp.float32) / base)
    lane = jnp.arange(LANES)
    inv = inv[lane % (base // 2)].reshape(1, LANES)
    src = [((lane % d) % (d // 2)) * (base // d) for d in dims]
    sel = jnp.stack([jnp.stack([(lane[:, None] == (s + part * (base // 2))[None, :]).astype(jnp.float32) for s in src])
                     for part in range(2)])
    sign = jnp.stack([jnp.where(lane % d < d // 2, -1.0, 1.0) for d in dims]).astype(jnp.float32)
    sign = sign.reshape(len(dims), 1, LANES)
    shape = jax.ShapeDtypeStruct((len(dims), n, LANES), jnp.float32)
    return pl.pallas_call(
        _rope_table_kernel,
        grid=(n // tm,),
        in_specs=[pl.BlockSpec((tm, LANES), lambda i: (i, 0)),
                  pl.BlockSpec((1, LANES), lambda i: (0, 0)),
                  pl.BlockSpec(sel.shape, lambda i: (0, 0, 0, 0)),
                  pl.BlockSpec(sign.shape, lambda i: (0, 0, 0))],
        out_specs=[pl.BlockSpec((len(dims), tm, LANES), lambda i: (0, i, 0))] * 2,
        out_shape=[shape, shape],
        name="rope_tables",
    )(posf, inv, sel, sign)


def _rope_lanes(xv, cos, sin_signed, dim):
    half = dim // 2
    if dim == LANES:
        partner = pltpu.roll(xv, half, axis=1)
    else:
        first = (_lane_iota(xv.shape) % dim) < half
        partner = jnp.where(first, pltpu.roll(xv, LANES - half, axis=1), pltpu.roll(xv, half, axis=1))
    return xv * cos + partner * sin_signed


def _prenorm(h, g, mod_ref):
    y = h * lax.rsqrt(jnp.mean(h * h, axis=-1, keepdims=True) + EPS) * g
    return y * (1.0 + mod_ref[1:2, :]) + mod_ref[0:1, :]


_A_COLS = {}
_o = 0
for _name, _w in (("q_lat", A_HEADS * A_LATENT), ("q_rope", A_HEADS * A_ROPE), ("c_kv", A_LATENT),
                  ("aux", LANES), ("kdup", LANES), ("gate", A_WIDTH), ("q_idx", IDX_HEADS * IDX_DIM)):
    _A_COLS[_name] = (_o, _o + _w)
    _o += _w
A_COLS_TOTAL = _o


def _a_weight(w_in):
    sizes = (A_HEADS * A_LATENT, A_HEADS * A_ROPE, A_LATENT, A_ROPE, A_WIDTH, IDX_HEADS * IDX_DIM, IDX_DIM, IDX_HEADS)
    parts, o = [], 0
    for s in sizes:
        parts.append(w_in[:, o:o + s])
        o += s
    q_lat, q_rope, c_kv, k_rope, gate, q_idx, k_idx, w_idx = parts
    d = w_in.shape[0]
    aux = jnp.concatenate([k_rope, w_idx, jnp.zeros((d, LANES - A_ROPE - IDX_HEADS), w_in.dtype)], axis=1)
    kdup = jnp.concatenate([k_idx, k_idx], axis=1)
    return _bf16(jnp.concatenate([q_lat, q_rope, c_kv, aux, kdup, gate, q_idx], axis=1))


def _proj_a_kernel(x_ref, mod_ref, g_ref, kvg_ref, w_ref, cos_ref, sin_ref,
                   qcat_ref, keys_ref, vt_ref, kdup_ref, qidx_ref, aux_ref, gate_ref):
    hn = _bf16(_prenorm(x_ref[...], g_ref[...], mod_ref))
    cos32, sin32 = cos_ref[0], sin_ref[0]
    cos64, sin64 = cos_ref[1], sin_ref[1]
    lane = _lane_iota(cos32.shape)
    qscale = (A_LATENT + A_ROPE) ** -0.5 * LOG2E

    def proj(name):
        lo, hi = _A_COLS[name]
        return _dot(hn, w_ref[:, lo:hi])

    q_blocks = x_ref.shape[0] // Q_BLOCK

    def put_q(h, row0, piece_t):
        slab, col0 = h // HEADS_PER_SLAB, (h % HEADS_PER_SLAB) * Q_BLOCK
        for qb in range(q_blocks):
            qcat_ref[qb, slab, row0:row0 + piece_t.shape[0], col0:col0 + Q_BLOCK] = _bf16(
                piece_t[:, qb * Q_BLOCK:(qb + 1) * Q_BLOCK])

    q_lat = proj("q_lat") * qscale
    q_rope = proj("q_rope")
    per_slab = LANES // A_ROPE
    pad = jnp.zeros((A_QPAD - A_LATENT - A_ROPE, x_ref.shape[0]), jnp.float32)
    for h in range(A_HEADS):
        put_q(h, 0, q_lat[:, h * A_LATENT:(h + 1) * A_LATENT].T)
    for j in range(A_HEADS // per_slab):
        roped_t = (_rope_lanes(q_rope[:, j * LANES:(j + 1) * LANES], cos32, sin32, A_ROPE) * qscale).T
        for t in range(per_slab):
            put_q(j * per_slab + t, A_LATENT, roped_t[t * A_ROPE:(t + 1) * A_ROPE, :])
            put_q(j * per_slab + t, A_LATENT + A_ROPE, pad)

    c_kv = proj("c_kv")
    c_kv = c_kv * lax.rsqrt(jnp.mean(c_kv * c_kv, axis=-1, keepdims=True) + EPS) * kvg_ref[...]
    keys_ref[:, 0:A_LATENT] = _bf16(c_kv)
    vt_ref[0:A_LATENT, :] = _bf16(c_kv.T)
    vt_ref[A_LATENT:V_ROWS, :] = jnp.ones((V_ROWS - A_LATENT, vt_ref.shape[1]), jnp.bfloat16)
    aux = proj("aux")
    k_rope = _rope_lanes(aux, cos32, sin32, A_ROPE)
    keys_ref[:, A_LATENT:A_QPAD] = _bf16(jnp.where(lane < A_ROPE, k_rope, 0.0))
    aux_ref[...] = aux * IDX_HEADS ** -0.5

    kdup_ref[...] = _bf16(_rope_lanes(proj("kdup"), cos64, sin64, IDX_DIM))
    q_idx = proj("q_idx")
    per_slab_i = LANES // IDX_DIM
    feat = lax.broadcasted_iota(jnp.int32, (LANES, x_ref.shape[0]), 0)
    for j in range(IDX_HEADS // per_slab_i):
        roped_t = _rope_lanes(q_idx[:, j * LANES:(j + 1) * LANES], cos64, sin64, IDX_DIM).T
        for t in range(per_slab_i):
            keep = (feat >= t * IDX_DIM) & (feat < (t + 1) * IDX_DIM)
            head_t = _bf16(jnp.where(keep, roped_t, 0.0))
            h = j * per_slab_i + t
            for qb in range(q_blocks):
                qidx_ref[qb, :, h * Q_BLOCK:(h + 1) * Q_BLOCK] = head_t[:, qb * Q_BLOCK:(qb + 1) * Q_BLOCK]

    gate_ref[...] = _bf16(_silu(proj("gate")))


def _proj_a(x2, mod, g, kvg, w, cos, sin, bsz, seq):
    n, d = x2.shape
    tm = KEY_CHUNK
    tps = seq // tm
    row = lambda i: (i, 0)
    qbt = tm // Q_BLOCK
    nq = seq // Q_BLOCK
    slabs, slab_w = A_HEADS // HEADS_PER_SLAB, HEADS_PER_SLAB * Q_BLOCK
    return pl.pallas_call(
        _proj_a_kernel,
        grid=(n // tm,),
        in_specs=[pl.BlockSpec((tm, d), row),
                  pl.BlockSpec((None, 3, d), lambda i: (i // tps, 0, 0)),
                  pl.BlockSpec((1, d), lambda i: (0, 0)),
                  pl.BlockSpec((1, A_LATENT), lambda i: (0, 0)),
                  pl.BlockSpec((d, A_COLS_TOTAL), lambda i: (0, 0)),
                  pl.BlockSpec((2, tm, LANES), lambda i: (0, i, 0)),
                  pl.BlockSpec((2, tm, LANES), lambda i: (0, i, 0))],
        out_specs=[pl.BlockSpec((None, qbt, slabs, A_QPAD, slab_w), lambda i: (i // tps, i % tps, 0, 0, 0)),
                   pl.BlockSpec((tm, A_QPAD), row),
                   pl.BlockSpec((None, None, V_ROWS, tm), lambda i: (i // tps, i % tps, 0, 0)),
                   pl.BlockSpec((tm, LANES), row),
                   pl.BlockSpec((None, qbt, LANES, IDX_HEADS * Q_BLOCK), lambda i: (i // tps, i % tps, 0, 0)),
                   pl.BlockSpec((tm, LANES), row),
                   pl.BlockSpec((tm, A_WIDTH), row)],
        out_shape=[jax.ShapeDtypeStruct((bsz, nq, slabs, A_QPAD, slab_w), jnp.bfloat16),
                   jax.ShapeDtypeStruct((n, A_QPAD), jnp.bfloat16),
                   jax.ShapeDtypeStruct((bsz, seq // tm, V_ROWS, tm), jnp.bfloat16),
                   jax.ShapeDtypeStruct((n, LANES), jnp.bfloat16),
                   jax.ShapeDtypeStruct((bsz, nq, LANES, IDX_HEADS * Q_BLOCK), jnp.bfloat16),
                   jax.ShapeDtypeStruct((n, LANES), jnp.float32),
                   jax.ShapeDtypeStruct((n, A_WIDTH), jnp.bfloat16)],
        compiler_params=pltpu.CompilerParams(vmem_limit_bytes=VMEM_LIMIT),
        name="proj_a",
    )(x2, mod, g, kvg, w, cos, sin)


def _attn_a_kernel(qcat_ref, qidx_ref, aux_ref, keys_ref, vt_ref, kdup_ref, gate_ref, x_ref, mod_ref, wuv_ref,
                   wout_ref, out_ref, sc_ref, hi_ref, lo_ref, lo2_ref, bias_ref, sa_ref, sb_ref, mxa_ref, mxb_ref, m_ref, acc_ref, og_ref, *, topk):
    i = pl.program_id(1)
    n_chunks = ((i + 1) * Q_BLOCK + KEY_CHUNK - 1) // KEY_CHUNK
    seq = keys_ref.shape[0] * KEY_CHUNK

    w_t = aux_ref[...].T
    qi_all = qidx_ref[...]
    qpos = i * Q_BLOCK + lax.broadcasted_iota(jnp.int32, (KEY_CHUNK, Q_BLOCK), 1)
    krow = lax.broadcasted_iota(jnp.int32, (KEY_CHUNK, Q_BLOCK), 0)

    n_pairs = n_chunks // 2
    odd = n_chunks % 2 == 1

    def for_chunk_pairs(fn, init):
        def body(t, carry):
            return fn(2 * t + 1, fn(2 * t, carry))
        carry = lax.fori_loop(0, n_pairs, body, init)
        return lax.cond(odd, lambda cr: fn(n_chunks - 1, cr), lambda cr: cr, carry)

    def score_chunk(c, carry):
        rel = _dot(kdup_ref[c], qi_all)
        acc = jnp.zeros((KEY_CHUNK, Q_BLOCK), jnp.float32)
        for h in range(IDX_HEADS):
            acc = acc + w_t[AUX_W_OFF + h:AUX_W_OFF + h + 1, :] * jnp.maximum(rel[:, h * Q_BLOCK:(h + 1) * Q_BLOCK], 0.0)
        bits = lax.bitcast_convert_type(acc, jnp.int32)
        key = bits ^ ((bits >> 31) & 0x7FFFFFFF)
        key = jnp.where(c * KEY_CHUNK + krow <= qpos, key, INT_MIN)
        sc_ref[c] = key
        hi_ref[c] = (key >> 16).astype(jnp.int16)
        lo_ref[c] = ((key & 0xFFFF) - HALF).astype(jnp.int16)
        return carry

    for_chunk_pairs(score_chunk, 0)

    def count(pred):
        def body(c, acc):
            hit = jnp.where(pred(sc_ref[c], c), 1, 0)
            return acc + jnp.sum(hit.reshape(KEY_CHUNK // 32, 32, Q_BLOCK), axis=0)
        acc = for_chunk_pairs(body, jnp.zeros((32, Q_BLOCK), jnp.int32))
        return jnp.sum(acc, axis=0, keepdims=True)

    def count16(ref16, pred):
        def body(c, acc):
            hit = jnp.where(pred(ref16[c]), jnp.int16(1), jnp.int16(0))
            words = pltpu.bitcast(hit, jnp.int32)
            return acc + jnp.sum(words.reshape(KEY_CHUNK // 64, 32, Q_BLOCK), axis=0)
        acc = for_chunk_pairs(body, jnp.zeros((32, Q_BLOCK), jnp.int32))
        acc = jnp.sum(acc, axis=0, keepdims=True)
        return (acc & 0xFFFF) + (acc >> 16)

    def bisect16(ref16, target, n_bits):
        def bit(b, t):
            cand = t + lax.shift_left(jnp.int32(1), 15 - b)
            cand16 = cand.astype(jnp.int16)
            return jnp.where(count16(ref16, lambda v: v >= cand16) >= target, cand, t)
        return lax.fori_loop(0, n_bits, bit, jnp.full((1, Q_BLOCK), -HALF, jnp.int32))

    all_selected = (i + 1) * Q_BLOCK <= topk
    n_bits = jnp.where(all_selected, 0, 16)
    t_hi = bisect16(hi_ref, topk, n_bits)
    t_hi16 = t_hi.astype(jnp.int16)
    above = count16(hi_ref, lambda v: v > t_hi16)

    def low_chunk(c, carry):
        lo2_ref[c] = jnp.where(hi_ref[c] == t_hi16, lo_ref[c], jnp.int16(-HALF))
        return carry

    for_chunk_pairs(low_chunk, 0)
    t_lo = bisect16(lo2_ref, topk - above, n_bits)
    thr = lax.shift_left(t_hi, 16) + (t_lo + HALF)

    need = topk - count(lambda k, c: k > thr)
    n_eq = count(lambda k, c: k == thr)
    excess = jnp.max(jnp.where(n_eq > need, 1, 0)) > 0
    idx_bits = (seq - 1).bit_length()

    def tie_bound():
        def bit(b, bound):
            cand = bound + lax.shift_left(jnp.int32(1), idx_bits - 1 - b)
            below = count(lambda k, c: jnp.where(k == thr, c * KEY_CHUNK + krow, seq) < cand)
            return jnp.where(below < need, cand, bound)
        return lax.fori_loop(0, idx_bits, bit, jnp.zeros((1, Q_BLOCK), jnp.int32))

    tie_idx = lax.cond(jnp.logical_and(excess, jnp.logical_not(all_selected)), tie_bound,
                       lambda: jnp.where(all_selected, -1, seq) + jnp.zeros((1, Q_BLOCK), jnp.int32))

    def bias_chunk(c, carry):
        k = sc_ref[c]
        tie_ok = jnp.where(c * KEY_CHUNK + krow <= tie_idx, 0.0, NEG)
        bias_ref[c] = jnp.where(k > thr, 0.0, jnp.where(k == thr, tie_ok, NEG))
        return carry

    for_chunk_pairs(bias_chunk, 0)

    m_ref[...] = jnp.full(m_ref.shape, NEG, jnp.float32)
    acc_ref[...] = jnp.zeros(acc_ref.shape, jnp.float32)

    slabs = A_HEADS // HEADS_PER_SLAB
    last_chunk = n_chunks - 1

    def scores(c, s_buf, mx_buf):
        c = jnp.minimum(c, last_chunk)
        kc = keys_ref[c]
        bias2 = jnp.concatenate([bias_ref[c]] * HEADS_PER_SLAB, axis=1)
        for pr in range(slabs):
            s = _dot(kc, qcat_ref[pr]) + bias2
            s_buf[pr] = s
            mx_buf[pr] = jnp.max(s, axis=0, keepdims=True)

    def accumulate(c, s_buf, mx_buf):
        vt = vt_ref[c]
        for pr in range(slabs):
            m_old = m_ref[pr]
            m_new = jnp.maximum(m_old, mx_buf[pr])
            alpha = jnp.exp2(m_old - m_new)
            p = jnp.exp2(s_buf[pr] - m_new)
            m_ref[pr] = m_new
            acc_ref[pr] = acc_ref[pr] * alpha + _dot(vt, _bf16(p))

    scores(0, sa_ref, mxa_ref)

    def attend_pair(t, carry):
        scores(2 * t + 1, sb_ref, mxb_ref)
        accumulate(2 * t, sa_ref, mxa_ref)
        scores(2 * t + 2, sa_ref, mxa_ref)
        accumulate(2 * t + 1, sb_ref, mxb_ref)
        return carry

    lax.fori_loop(0, n_pairs, attend_pair, 0)

    @pl.when(odd)
    def _():
        accumulate(n_chunks - 1, sa_ref, mxa_ref)

    for pr in range(slabs):
        o_t = acc_ref[pr, 0:A_LATENT, :] / acc_ref[pr, A_LATENT:A_LATENT + 1, :]
        for t in range(HEADS_PER_SLAB):
            h = pr * HEADS_PER_SLAB + t
            o_lat = _bf16(o_t[:, t * Q_BLOCK:(t + 1) * Q_BLOCK].T)
            o = _dot(o_lat, wuv_ref[h]) * gate_ref[:, h * A_VDIM:(h + 1) * A_VDIM]
            og_ref[:, h * A_VDIM:(h + 1) * A_VDIM] = _bf16(o)
    out_ref[...] = x_ref[...] + mod_ref[2:3, :] * _dot(og_ref[...], wout_ref[...])


def _attn_a(qcat, qidx, aux, keys, vt, kdup, gate, x2, mod, w_uv, w_out, topk):
    bsz, nq = qcat.shape[:2]
    seq = nq * Q_BLOCK
    d = x2.shape[1]
    nkc = seq // KEY_CHUNK
    row = lambda b, i: (b * nq + i, 0)
    slabs = A_HEADS // HEADS_PER_SLAB
    slab_w = HEADS_PER_SLAB * Q_BLOCK
    once = pl.Buffered(1)
    return pl.pallas_call(
        functools.partial(_attn_a_kernel, topk=topk),
        grid=(bsz, nq),
        in_specs=[pl.BlockSpec((None, None, slabs, A_QPAD, slab_w), lambda b, i: (b, i, 0, 0, 0)),
                  pl.BlockSpec((None, None, LANES, IDX_HEADS * Q_BLOCK), lambda b, i: (b, i, 0, 0)),
                  pl.BlockSpec((Q_BLOCK, LANES), row),
                  pl.BlockSpec((None, nkc, KEY_CHUNK, A_QPAD), lambda b, i: (b, 0, 0, 0), pipeline_mode=once),
                  pl.BlockSpec((None, nkc, V_ROWS, KEY_CHUNK), lambda b, i: (b, 0, 0, 0), pipeline_mode=once),
                  pl.BlockSpec((None, nkc, KEY_CHUNK, LANES), lambda b, i: (b, 0, 0, 0), pipeline_mode=once),
                  pl.BlockSpec((Q_BLOCK, A_WIDTH), row),
                  pl.BlockSpec((Q_BLOCK, d), row),
                  pl.BlockSpec((None, 3, d), lambda b, i: (b, 0, 0)),
                  pl.BlockSpec((A_HEADS, A_LATENT, A_VDIM), lambda b, i: (0, 0, 0), pipeline_mode=once),
                  pl.BlockSpec((A_WIDTH, d), lambda b, i: (0, 0), pipeline_mode=once)],
        out_specs=pl.BlockSpec((Q_BLOCK, d), row),
        out_shape=jax.ShapeDtypeStruct(x2.shape, jnp.float32),
        scratch_shapes=[pltpu.VMEM((nkc, KEY_CHUNK, Q_BLOCK), jnp.int32),
                        pltpu.VMEM((nkc, KEY_CHUNK, Q_BLOCK), jnp.int16),
                        pltpu.VMEM((nkc, KEY_CHUNK, Q_BLOCK), jnp.int16),
                        pltpu.VMEM((nkc, KEY_CHUNK, Q_BLOCK), jnp.int16),
                        pltpu.VMEM((nkc, KEY_CHUNK, Q_BLOCK), jnp.float32),
                        pltpu.VMEM((slabs, KEY_CHUNK, slab_w), jnp.float32),
                        pltpu.VMEM((slabs, KEY_CHUNK, slab_w), jnp.float32),
                        pltpu.VMEM((slabs, 1, slab_w), jnp.float32),
                        pltpu.VMEM((slabs, 1, slab_w), jnp.float32),
                        pltpu.VMEM((slabs, 1, slab_w), jnp.float32),
                        pltpu.VMEM((slabs, V_ROWS, slab_w), jnp.float32),
                        pltpu.VMEM((Q_BLOCK, A_WIDTH), jnp.bfloat16)],
        compiler_params=pltpu.CompilerParams(vmem_limit_bytes=VMEM_LIMIT),
        name="attn_a",
    )(qcat, qidx, aux, keys.reshape(bsz, nkc, KEY_CHUNK, A_QPAD), vt, kdup.reshape(bsz, nkc, KEY_CHUNK, LANES),
      gate, x2, mod, w_uv, w_out)


_PERM_DILS = tuple(dil for _, dil in B_GROUPS if dil > 1)


def _dilation_perms(tm):
    dst = jnp.arange(tm)
    perms = []
    for dil in _PERM_DILS:
        src = (dst % (tm // dil)) * dil + dst // (tm // dil)
        perms.append((src[:, None] == dst[None, :]).astype(jnp.bfloat16))
    return jnp.stack(perms)


_MATMUL_REORDER_DIL = 16


def _store_dilated(val, out_refs, tmp_ref, perm_ref):
    vb = _bf16(val)
    tm, width = vb.shape
    heads, _, hdim = tmp_ref.shape
    in_tmp = False
    for out_ref in out_refs:
        dil = out_ref.shape[1] // width
        rows = tm // dil
        if dil == 1:
            out_ref[...] = vb
        elif dil >= _MATMUL_REORDER_DIL:
            res = _bf16(_dot(perm_ref[_PERM_DILS.index(dil)], vb))
            for r in range(dil):
                out_ref[:, r * width:(r + 1) * width] = res[r * rows:(r + 1) * rows, :]
        else:
            if not in_tmp:
                for hd in range(heads):
                    tmp_ref[hd] = val[:, hd * hdim:(hd + 1) * hdim]
                in_tmp = True
            for r in range(dil):
                for hd in range(heads):
                    col = r * width + hd * hdim
                    out_ref[:, col:col + hdim] = _bf16(tmp_ref[hd, pl.ds(r, rows, stride=dil), :])


def _proj_b_kernel(*refs, with_kv):
    n_g = len(B_GROUPS)
    if with_kv:
        h_ref, mod_ref, g_ref, kvg_ref, wkv_ref, win_ref, cos_ref, sin_ref, perm_ref = refs[:9]
        outs = refs[9:]
        k_refs, v_refs, outs = outs[:n_g], outs[n_g:2 * n_g], outs[2 * n_g:]
    else:
        h_ref, mod_ref, g_ref, win_ref, cos_ref, sin_ref, perm_ref = refs[:7]
        outs = refs[7:]
    q_refs, gate_ref, tmp_ref = outs[:n_g], outs[n_g], outs[n_g + 1]
    h = h_ref[...]
    cos, sin = cos_ref[...], sin_ref[...]
    width = B_WIDTH

    def rope(t, scale):
        return jnp.concatenate([_rope_lanes(t[:, hd * B_HDIM:(hd + 1) * B_HDIM], cos, sin, B_HDIM) * scale
                                for hd in range(B_HEADS)], axis=1)

    if with_kv:
        kvn = _bf16(h * lax.rsqrt(jnp.mean(h * h, axis=-1, keepdims=True) + EPS) * kvg_ref[...])
        _store_dilated(rope(_dot(kvn, wkv_ref[:, 0:width]), 1.0), k_refs, tmp_ref, perm_ref)
        _store_dilated(_dot(kvn, wkv_ref[:, width:2 * width]), v_refs, tmp_ref, perm_ref)
    hn = _bf16(_prenorm(h, g_ref[...], mod_ref))
    qscale = B_HDIM ** -0.5 * LOG2E
    for gi, q_ref in enumerate(q_refs):
        _store_dilated(rope(_dot(hn, win_ref[:, gi * width:(gi + 1) * width]), qscale), [q_ref], tmp_ref, perm_ref)
    gate_ref[...] = _bf16(_silu(_dot(hn, win_ref[:, n_g * width:(n_g + 1) * width])))


def _proj_b(h2, mod, g, kvg, w_kv, w_in, cos, sin, bsz, seq, with_kv, tm=256):
    n, d = h2.shape
    tps = seq // tm
    row = lambda i: (i, 0)
    const = lambda i: (0, 0)
    in_specs = [pl.BlockSpec((tm, d), row), pl.BlockSpec((None, 3, d), lambda i: (i // tps, 0, 0)),
                pl.BlockSpec((1, d), const)]
    args = [h2, mod, g]
    if with_kv:
        in_specs += [pl.BlockSpec((1, d), const), pl.BlockSpec(w_kv.shape, const)]
        args += [kvg, w_kv]
    perms = _dilation_perms(tm)
    table = pl.BlockSpec((None, tm, LANES), lambda i: (ROPE_DIMS.index(B_HDIM), i, 0))
    in_specs += [pl.BlockSpec(w_in.shape, const), table, table, pl.BlockSpec(perms.shape, lambda i: (0, 0, 0))]
    args += [w_in, cos, sin, perms]
    dil_specs = [pl.BlockSpec((None, tm // dil, dil * B_WIDTH), lambda i: (i // tps, i % tps, 0)) for _, dil in B_GROUPS]
    dil_shapes = [jax.ShapeDtypeStruct((bsz, seq // dil, dil * B_WIDTH), jnp.bfloat16) for _, dil in B_GROUPS]
    n_sets = 3 if with_kv else 1
    return pl.pallas_call(
        functools.partial(_proj_b_kernel, with_kv=with_kv),
        grid=(n // tm,),
        in_specs=in_specs,
        out_specs=dil_specs * n_sets + [pl.BlockSpec((tm, B_WIDTH), row)],
        out_shape=dil_shapes * n_sets + [jax.ShapeDtypeStruct((n, B_WIDTH), jnp.bfloat16)],
        scratch_shapes=[pltpu.VMEM((B_HEADS, tm, B_HDIM), jnp.float32)],
        compiler_params=pltpu.CompilerParams(vmem_limit_bytes=VMEM_LIMIT),
        name="proj_b",
    )(*args)


def _dilated_kernel(q_ref, kc_ref, vc_ref, o_ref, lse_ref, kp_ref, vp_ref):
    j = pl.program_id(2)
    w = q_ref.shape[0]

    @pl.when(j == 0)
    def _():
        kp_ref[...] = jnp.zeros(kp_ref.shape, kp_ref.dtype)
        vp_ref[...] = jnp.zeros(vp_ref.shape, vp_ref.dtype)

    qi = lax.broadcasted_iota(jnp.int32, (w, w), 0)
    ki = lax.broadcasted_iota(jnp.int32, (w, w), 1)
    bias = jnp.concatenate([jnp.where(jnp.logical_and(ki >= qi, j > 0), 0.0, NEG),
                            jnp.where(ki <= qi, 0.0, NEG)], axis=1)
    ones = jnp.ones((2 * w, B_HDIM), jnp.bfloat16)
    lses = []
    for hd in range(B_HEADS):
        sl = slice(hd * B_HDIM, (hd + 1) * B_HDIM)
        keys = jnp.concatenate([kp_ref[:, sl], kc_ref[:, sl]], axis=0)
        vals = jnp.concatenate([vp_ref[:, sl], vc_ref[:, sl]], axis=0)
        s = _dot_nt(q_ref[:, sl], keys) + bias
        m = jnp.max(s, axis=-1, keepdims=True)
        p = _bf16(jnp.exp2(s - m))
        ol = _dot(p, jnp.concatenate([vals, ones], axis=1))
        l = ol[:, B_HDIM:]
        o_ref[:, sl] = _bf16(ol[:, :B_HDIM] / l)
        lses.append(m + jnp.log2(l[:, 0:1]))
    lse_ref[...] = jnp.concatenate(lses + [jnp.zeros((w, LANES - B_HEADS), jnp.float32)], axis=1)
    kp_ref[...] = kc_ref[...]
    vp_ref[...] = vc_ref[...]


def _dilated(q, k, v, window, dil):
    bsz, n, _ = q.shape
    w = window // dil
    nb = n // w
    cur = lambda b, r, j: (b, j, r)
    blk = pl.BlockSpec((None, w, B_WIDTH), cur)
    return pl.pallas_call(
        _dilated_kernel,
        grid=(bsz, dil, nb),
        in_specs=[blk, blk, blk],
        out_specs=[blk, pl.BlockSpec((None, None, w, LANES), lambda b, r, j: (b, r, j, 0))],
        out_shape=[jax.ShapeDtypeStruct(q.shape, jnp.bfloat16),
                   jax.ShapeDtypeStruct((bsz, dil, n, LANES), jnp.float32)],
        scratch_shapes=[pltpu.VMEM((w, B_WIDTH), jnp.bfloat16), pltpu.VMEM((w, B_WIDTH), jnp.bfloat16)],
        compiler_params=pltpu.CompilerParams(dimension_semantics=("arbitrary", "arbitrary", "arbitrary")),
        name=f"dilated_{dil}",
    )(q, k, v)


def _merge_kernel(*refs, final):
    n_g = len(B_GROUPS)
    o_refs, lse_refs = refs[:n_g], refs[n_g:2 * n_g]
    gate_ref, h_ref, mod_ref, wout_ref, fin_ref, unperm_ref, out_ref, ls_ref, og_ref = refs[2 * n_g:]
    tm = h_ref.shape[0]
    o_seq = []
    for gi, (_, dil) in enumerate(B_GROUPS):
        rows = tm // dil
        for r in range(dil):
            dst = pl.ds(r, rows, stride=dil) if dil > 1 else slice(None)
            ls_ref[gi, dst, :] = lse_refs[gi][r]
        if dil == 1:
            o_seq.append(o_refs[gi][...].astype(jnp.float32))
        else:
            stacked = jnp.concatenate([o_refs[gi][:, r * B_WIDTH:(r + 1) * B_WIDTH] for r in range(dil)], axis=0)
            o_seq.append(_dot(unperm_ref[_PERM_DILS.index(dil)], stacked))
    lses = [ls_ref[gi] for gi in range(n_g)]
    m = functools.reduce(jnp.maximum, lses)
    e = [jnp.exp2(t - m) for t in lses]
    inv = 1.0 / functools.reduce(lambda a, b: a + b, e)
    for hd in range(B_HEADS):
        sl = slice(hd * B_HDIM, (hd + 1) * B_HDIM)
        o = functools.reduce(lambda a, b: a + b,
                             [(e[gi] * inv)[:, hd:hd + 1] * o_seq[gi][:, sl] for gi in range(n_g)])
        og_ref[:, sl] = _bf16(o * gate_ref[:, sl])
    h = h_ref[...] + mod_ref[2:3, :] * _dot(og_ref[...], wout_ref[...])
    if final:
        h = h * lax.rsqrt(jnp.mean(h * h, axis=-1, keepdims=True) + EPS) * fin_ref[...]
    out_ref[...] = h


def _merge(os_, lses, gate, h2, mod, w_out, fin, seq, final, tm=256):
    n, d = h2.shape
    tps = seq // tm
    row = lambda i: (i, 0)
    rows = pl.BlockSpec((tm, d), row)
    n_g = len(B_GROUPS)
    o_specs = [pl.BlockSpec((None, tm // dil, dil * B_WIDTH), lambda i: (i // tps, i % tps, 0)) for _, dil in B_GROUPS]
    lse_specs = [pl.BlockSpec((None, dil, tm // dil, LANES), lambda i: (i // tps, 0, i % tps, 0)) for _, dil in B_GROUPS]
    unperms = jnp.swapaxes(_dilation_perms(tm), 1, 2)
    return pl.pallas_call(
        functools.partial(_merge_kernel, final=final),
        grid=(n // tm,),
        in_specs=o_specs + lse_specs + [rows, rows, pl.BlockSpec((None, 3, d), lambda i: (i // tps, 0, 0)),
                                        pl.BlockSpec(w_out.shape, lambda i: (0, 0)),
                                        pl.BlockSpec((1, d), lambda i: (0, 0)),
                                        pl.BlockSpec(unperms.shape, lambda i: (0, 0, 0))],
        out_specs=rows,
        out_shape=jax.ShapeDtypeStruct((n, d), jnp.float32),
        scratch_shapes=[pltpu.VMEM((n_g, tm, LANES), jnp.float32),
                        pltpu.VMEM((tm, B_WIDTH), jnp.bfloat16)],
        compiler_params=pltpu.CompilerParams(vmem_limit_bytes=VMEM_LIMIT),
        name="merge_out",
    )(*os_, *lses, gate, h2, mod, w_out, fin, unperms)


def _final_norm_kernel(h_ref, g_ref, out_ref):
    h = h_ref[...]
    out_ref[...] = h * lax.rsqrt(jnp.mean(h * h, axis=-1, keepdims=True) + EPS) * g_ref[...]


def kernel(x, c, positions, a_norm, a_ada_w, a_ada_b, a_w_in, a_kv_norm, a_w_uv, a_w_out, kv_norm, w_kv, b_norm, b_ada_w, b_ada_b, b_w_in, b_w_out, final_norm):
    bsz, seq, d = x.shape
    n = bsz * seq
    n_a, n_b = a_norm.shape[0], b_norm.shape[0]
    topk = min(TOPK_MAX, seq // 4)
    assert topk % Q_BLOCK == 0 and seq % KEY_CHUNK == 0 and d == B_WIDTH
    assert all(seq % window == 0 for window, _ in B_GROUPS)

    cos, sin = _rope_tables(positions, ROPE_DIMS)
    h = x.reshape(n, d)
    for li in range(n_a):
        mod = _ada(c, a_ada_w[li], a_ada_b[li])
        qcat, keys, vt, kdup, qidx, aux, gate = _proj_a(
            h, mod, a_norm[li].reshape(1, d), a_kv_norm[li].reshape(1, A_LATENT), _a_weight(a_w_in[li]),
            cos, sin, bsz, seq)
        h = _attn_a(qcat, qidx, aux, keys, vt, kdup, gate, h, mod, _bf16(a_w_uv[li]), _bf16(a_w_out[li]), topk)
    k_sh = v_sh = None
    for li in range(n_b):
        mod = _ada(c, b_ada_w[li], b_ada_b[li])
        outs = _proj_b(h, mod, b_norm[li].reshape(1, d), kv_norm.reshape(1, d), _bf16(w_kv), _bf16(b_w_in[li]),
                       cos, sin, bsz, seq, with_kv=(li == 0))
        n_g = len(B_GROUPS)
        if li == 0:
            k_sh, v_sh = outs[:n_g], outs[n_g:2 * n_g]
            outs = outs[2 * n_g:]
        q_groups, gate = outs[:n_g], outs[n_g]
        res = [_dilated(qg, kg, vg, window, dil)
               for qg, kg, vg, (window, dil) in zip(q_groups, k_sh, v_sh, B_GROUPS)]
        h = _merge([r[0] for r in res], [r[1] for r in res], gate, h, mod, _bf16(b_w_out[li]),
                   final_norm.reshape(1, d), seq, final=(li == n_b - 1))
    if n_b == 0:
        h = pl.pallas_call(
            _final_norm_kernel, grid=(n // 256,),
            in_specs=[pl.BlockSpec((256, d), lambda i: (i, 0)), pl.BlockSpec((1, d), lambda i: (0, 0))],
            out_specs=pl.BlockSpec((256, d), lambda i: (i, 0)),
            out_shape=jax.ShapeDtypeStruct((n, d), jnp.float32), name="final_norm",
        )(h, final_norm.reshape(1, d))
    return h.reshape(bsz, seq, d)
```

```python
import functools
import math

import jax
import jax.numpy as jnp
from jax import lax
from jax.experimental import pallas as pl
from jax.experimental.pallas import tpu as pltpu

ROPE_THETA = 10000.0
EPS = 1e-6
LOG2E = 1.4426950408889634

A_HEADS = 16
A_LATENT = 128
A_ROPE = 32
A_VDIM = 128
A_WIDTH = A_HEADS * A_VDIM
IDX_HEADS = 8
IDX_DIM = 64
TOPK_MAX = 256
Q_BLOCK = 256
KEY_CHUNK = 256
A_QPAD = 256
SLAB_LANES = 256
COUNT_UNROLL = 4
HEADS_PER_SLAB = SLAB_LANES // Q_BLOCK
V_ROWS = A_LATENT + 16

B_HEADS = 8
B_HDIM = 128
B_GROUPS = ((128, 1), (512, 4), (2048, 16))
B_WIDTH = B_HEADS * B_HDIM
ROPE_DIMS = (A_ROPE, IDX_DIM, B_HDIM)

LANES = 128
NEG = -1e30
INT_MIN = -(2 ** 31)
HALF = 2 ** 15
VMEM_LIMIT = 56 * 1024 * 1024

AUX_W_OFF = A_ROPE


def _bf16(t):
    return t.astype(jnp.bfloat16)


def _dot(a, b):
    return jnp.dot(a, b, preferred_element_type=jnp.float32)


def _dot_nt(a, b):
    return lax.dot_general(a, b, (((1,), (1,)), ((), ())), preferred_element_type=jnp.float32)


def _silu(t):
    return t * (1.0 / (1.0 + jnp.exp(-t)))


def _lane_iota(shape):
    return lax.broadcasted_iota(jnp.int32, shape, len(shape) - 1)


def _ada_kernel(c_ref, w_ref, b_ref, out_ref):
    out_ref[...] = _dot(_bf16(_silu(c_ref[...])), _bf16(w_ref[...])) + b_ref[...]


def _ada(c, w, b):
    bsz, d = c.shape
    out = pl.pallas_call(
        _ada_kernel,
        grid=(3,),
        in_specs=[pl.BlockSpec((bsz, d), lambda j: (0, 0)),
                  pl.BlockSpec((d, d), lambda j: (0, j)),
                  pl.BlockSpec((1, d), lambda j: (0, j))],
        out_specs=pl.BlockSpec((bsz, d), lambda j: (0, j)),
        out_shape=jax.ShapeDtypeStruct((bsz, 3 * d), jnp.float32),
        name="ada_mod",
    )(c, w, b.reshape(1, 3 * d))
    return out.reshape(bsz, 3, d)


def _rope_table_kernel(pos_ref, inv_ref, sel_ref, sign_ref, cos_ref, sin_ref):
    half_rows = pos_ref.shape[0] // 2
    upper = _lane_iota((half_rows, LANES)) >= LANES // 2
    ang = jnp.where(upper, pos_ref[half_rows:, :], pos_ref[:half_rows, :]) * inv_ref[...]
    cos, sin = jnp.cos(ang), jnp.sin(ang)
    for t in range(cos_ref.shape[0]):
        for part in range(2):
            pick = lambda v: jnp.dot(v, sel_ref[part, t], precision=lax.Precision.HIGHEST,
                                     preferred_element_type=jnp.float32)
            rows = slice(part * half_rows, (part + 1) * half_rows)
            cos_ref[t, rows, :] = pick(cos)
            sin_ref[t, rows, :] = pick(sin) * sign_ref[t]


def _rope_tables(positions, dims, tm=512):
    n = positions.size
    base = max(dims)
    assert base == LANES and all(base % d == 0 and d & (d - 1) == 0 for d in dims)
    posf = jnp.broadcast_to(positions.reshape(n, 1).astype(jnp.float32), (n, LANES))
    inv = ROPE_THETA ** (-jnp.arange(0, base, 2, dtype=jnp.float32) / base)
    lane = jnp.arange(LANES)
    inv = inv[lane % (base // 2)].reshape(1, LANES)
    src = [((lane % d) % (d // 2)) * (base // d) for d in dims]
    sel = jnp.stack([jnp.stack([(lane[:, None] == (s + part * (base // 2))[None, :]).astype(jnp.float32) for s in src])
                     for part in range(2)])
    sign = jnp.stack([jnp.where(lane % d < d // 2, -1.0, 1.0) for d in dims]).astype(jnp.float32)
    sign = sign.reshape(len(dims), 1, LANES)
    shape = jax.ShapeDtypeStruct((len(dims), n, LANES), jnp.float32)
    return pl.pallas_call(
        _rope_table_kernel,
        grid=(n // tm,),
        in_specs=[pl.BlockSpec((tm, LANES), lambda i: (i, 0)),
                  pl.BlockSpec((1, LANES), lambda i: (0, 0)),
                  pl.BlockSpec(sel.shape, lambda i: (0, 0, 0, 0)),
                  pl.BlockSpec(sign.shape, lambda i: (0, 0, 0))],
        out_specs=[pl.BlockSpec((len(dims), tm, LANES), lambda i: (0, i, 0))] * 2,
        out_shape=[shape, shape],
        name="rope_tables",
    )(posf, inv, sel, sign)


def _rope_lanes(xv, cos, sin_signed, dim):
    half = dim // 2
    if dim == LANES:
        partner = pltpu.roll(xv, half, axis=1)
    else:
        first = (_lane_iota(xv.shape) % dim) < half
        partner = jnp.where(first, pltpu.roll(xv, LANES - half, axis=1), pltpu.roll(xv, half, axis=1))
    return xv * cos + partner * sin_signed


def _prenorm(h, g, mod_ref):
    y = h * lax.rsqrt(jnp.mean(h * h, axis=-1, keepdims=True) + EPS) * g
    return y * (1.0 + mod_ref[1:2, :]) + mod_ref[0:1, :]


_A_COLS = {}
_o = 0
for _name, _w in (("q_lat", A_HEADS * A_LATENT), ("q_rope", A_HEADS * A_ROPE), ("c_kv", A_LATENT),
                  ("aux", LANES), ("kdup", LANES), ("gate", A_WIDTH), ("q_idx", IDX_HEADS * IDX_DIM)):
    _A_COLS[_name] = (_o, _o + _w)
    _o += _w
A_COLS_TOTAL = _o


def _a_weight(w_in):
    sizes = (A_HEADS * A_LATENT, A_HEADS * A_ROPE, A_LATENT, A_ROPE, A_WIDTH, IDX_HEADS * IDX_DIM, IDX_DIM, IDX_HEADS)
    parts, o = [], 0
    for s in sizes:
        parts.append(w_in[:, o:o + s])
        o += s
    q_lat, q_rope, c_kv, k_rope, gate, q_idx, k_idx, w_idx = parts
    d = w_in.shape[0]
    aux = jnp.concatenate([k_rope, w_idx, jnp.zeros((d, LANES - A_ROPE - IDX_HEADS), w_in.dtype)], axis=1)
    kdup = jnp.concatenate([k_idx, k_idx], axis=1)
    return _bf16(jnp.concatenate([q_lat, q_rope, c_kv, aux, kdup, gate, q_idx], axis=1))


def _proj_a_kernel(x_ref, mod_ref, g_ref, kvg_ref, w_ref, cos_ref, sin_ref,
                   qcat_ref, keys_ref, vt_ref, kdup_ref, qidx_ref, aux_ref, gate_ref):
    hn = _bf16(_prenorm(x_ref[...], g_ref[...], mod_ref))
    cos32, sin32 = cos_ref[0], sin_ref[0]
    cos64, sin64 = cos_ref[1], sin_ref[1]
    lane = _lane_iota(cos32.shape)
    qscale = (A_LATENT + A_ROPE) ** -0.5 * LOG2E

    def proj(name):
        lo, hi = _A_COLS[name]
        return _dot(hn, w_ref[:, lo:hi])

    q_blocks = x_ref.shape[0] // Q_BLOCK

    def put_q(h, row0, piece_t):
        slab, col0 = h // HEADS_PER_SLAB, (h % HEADS_PER_SLAB) * Q_BLOCK
        for qb in range(q_blocks):
            qcat_ref[qb, slab, row0:row0 + piece_t.shape[0], col0:col0 + Q_BLOCK] = _bf16(
                piece_t[:, qb * Q_BLOCK:(qb + 1) * Q_BLOCK])

    q_lat = proj("q_lat") * qscale
    q_rope = proj("q_rope")
    per_slab = LANES // A_ROPE
    pad = jnp.zeros((A_QPAD - A_LATENT - A_ROPE, x_ref.shape[0]), jnp.float32)
    for h in range(A_HEADS):
        put_q(h, 0, q_lat[:, h * A_LATENT:(h + 1) * A_LATENT].T)
    for j in range(A_HEADS // per_slab):
        roped_t = (_rope_lanes(q_rope[:, j * LANES:(j + 1) * LANES], cos32, sin32, A_ROPE) * qscale).T
        for t in range(per_slab):
            put_q(j * per_slab + t, A_LATENT, roped_t[t * A_ROPE:(t + 1) * A_ROPE, :])
            put_q(j * per_slab + t, A_LATENT + A_ROPE, pad)

    c_kv = proj("c_kv")
    c_kv = c_kv * lax.rsqrt(jnp.mean(c_kv * c_kv, axis=-1, keepdims=True) + EPS) * kvg_ref[...]
    keys_ref[:, 0:A_LATENT] = _bf16(c_kv)
    vt_ref[0:A_LATENT, :] = _bf16(c_kv.T)
    vt_ref[A_LATENT:V_ROWS, :] = jnp.ones((V_ROWS - A_LATENT, vt_ref.shape[1]), jnp.bfloat16)
    aux = proj("aux")
    k_rope = _rope_lanes(aux, cos32, sin32, A_ROPE)
    keys_ref[:, A_LATENT:A_QPAD] = _bf16(jnp.where(lane < A_ROPE, k_rope, 0.0))
    aux_ref[...] = aux * IDX_HEADS ** -0.5

    kdup_ref[...] = _bf16(_rope_lanes(proj("kdup"), cos64, sin64, IDX_DIM))
    q_idx = proj("q_idx")
    per_slab_i = LANES // IDX_DIM
    feat = lax.broadcasted_iota(jnp.int32, (LANES, x_ref.shape[0]), 0)
    for j in range(IDX_HEADS // per_slab_i):
        roped_t = _rope_lanes(q_idx[:, j * LANES:(j + 1) * LANES], cos64, sin64, IDX_DIM).T
        for t in range(per_slab_i):
            keep = (feat >= t * IDX_DIM) & (feat < (t + 1) * IDX_DIM)
            head_t = _bf16(jnp.where(keep, roped_t, 0.0))
            h = j * per_slab_i + t
            for qb in range(q_blocks):
                qidx_ref[qb, :, h * Q_BLOCK:(h + 1) * Q_BLOCK] = head_t[:, qb * Q_BLOCK:(qb + 1) * Q_BLOCK]

    gate_ref[...] = _bf16(_silu(proj("gate")))


def _proj_a(x2, mod, g, kvg, w, cos, sin, bsz, seq):
    n, d = x2.shape
    tm = KEY_CHUNK
    tps = seq // tm
    row = lambda i: (i, 0)
    qbt = tm // Q_BLOCK
    nq = seq // Q_BLOCK
    slabs, slab_w = A_HEADS // HEADS_PER_SLAB, HEADS_PER_SLAB * Q_BLOCK
    return pl.pallas_call(
        _proj_a_kernel,
        grid=(n // tm,),
        in_specs=[pl.BlockSpec((tm, d), row),
                  pl.BlockSpec((None, 3, d), lambda i: (i // tps, 0, 0)),
                  pl.BlockSpec((1, d), lambda i: (0, 0)),
                  pl.BlockSpec((1, A_LATENT), lambda i: (0, 0)),
                  pl.BlockSpec((d, A_COLS_TOTAL), lambda i: (0, 0)),
                  pl.BlockSpec((2, tm, LANES), lambda i: (0, i, 0)),
                  pl.BlockSpec((2, tm, LANES), lambda i: (0, i, 0))],
        out_specs=[pl.BlockSpec((None, qbt, slabs, A_QPAD, slab_w), lambda i: (i // tps, i % tps, 0, 0, 0)),
                   pl.BlockSpec((tm, A_QPAD), row),
                   pl.BlockSpec((None, None, V_ROWS, tm), lambda i: (i // tps, i % tps, 0, 0)),
                   pl.BlockSpec((tm, LANES), row),
                   pl.BlockSpec((None, qbt, LANES, IDX_HEADS * Q_BLOCK), lambda i: (i // tps, i % tps, 0, 0)),
                   pl.BlockSpec((tm, LANES), row),
                   pl.BlockSpec((tm, A_WIDTH), row)],
        out_shape=[jax.ShapeDtypeStruct((bsz, nq, slabs, A_QPAD, slab_w), jnp.bfloat16),
                   jax.ShapeDtypeStruct((n, A_QPAD), jnp.bfloat16),
                   jax.ShapeDtypeStruct((bsz, seq // tm, V_ROWS, tm), jnp.bfloat16),
                   jax.ShapeDtypeStruct((n, LANES), jnp.bfloat16),
                   jax.ShapeDtypeStruct((bsz, nq, LANES, IDX_HEADS * Q_BLOCK), jnp.bfloat16),
                   jax.ShapeDtypeStruct((n, LANES), jnp.float32),
                   jax.ShapeDtypeStruct((n, A_WIDTH), jnp.bfloat16)],
        compiler_params=pltpu.CompilerParams(vmem_limit_bytes=VMEM_LIMIT),
        name="proj_a",
    )(x2, mod, g, kvg, w, cos, sin)


def _attn_a_kernel(qcat_ref, qidx_ref, aux_ref, keys_ref, vt_ref, kdup_ref, gate_ref, x_ref, mod_ref, wuv_ref,
                   wout_ref, out_ref, sc_ref, hi_ref, lo_ref, lo2_ref, bias_ref, sa_ref, sb_ref, mxa_ref, mxb_ref, m_ref, acc_ref, og_ref, *, topk):
    i = pl.program_id(1)
    n_chunks = ((i + 1) * Q_BLOCK + KEY_CHUNK - 1) // KEY_CHUNK
    seq = keys_ref.shape[0] * KEY_CHUNK

    w_t = aux_ref[...].T
    qi_all = qidx_ref[...]
    qpos = i * Q_BLOCK + lax.broadcasted_iota(jnp.int32, (KEY_CHUNK, Q_BLOCK), 1)
    krow = lax.broadcasted_iota(jnp.int32, (KEY_CHUNK, Q_BLOCK), 0)

    n_pairs = n_chunks // 2
    odd = n_chunks % 2 == 1

    def for_chunk_pairs(fn, init):
        def body(t, carry):
            return fn(2 * t + 1, fn(2 * t, carry))
        carry = lax.fori_loop(0, n_pairs, body, init)
        return lax.cond(odd, lambda cr: fn(n_chunks - 1, cr), lambda cr: cr, carry)

    n_padded = (n_chunks + COUNT_UNROLL - 1) // COUNT_UNROLL * COUNT_UNROLL

    def for_chunks_padded(fn, init):
        def body(t, carry):
            for u in range(COUNT_UNROLL):
                carry = fn(COUNT_UNROLL * t + u, carry)
            return carry
        return lax.fori_loop(0, n_padded // COUNT_UNROLL, body, init)

    def score_chunk(c, carry):
        rel = _dot(kdup_ref[c], qi_all)
        acc = jnp.zeros((KEY_CHUNK, Q_BLOCK), jnp.float32)
        for h in range(IDX_HEADS):
            acc = acc + w_t[AUX_W_OFF + h:AUX_W_OFF + h + 1, :] * jnp.maximum(rel[:, h * Q_BLOCK:(h + 1) * Q_BLOCK], 0.0)
        bits = lax.bitcast_convert_type(acc, jnp.int32)
        key = bits ^ ((bits >> 31) & 0x7FFFFFFF)
        key = jnp.where(c * KEY_CHUNK + krow <= qpos, key, INT_MIN)
        sc_ref[c] = key
        hi_ref[c] = (key >> 16).astype(jnp.int16)
        lo_ref[c] = ((key & 0xFFFF) - HALF).astype(jnp.int16)
        return carry

    for_chunk_pairs(score_chunk, 0)

    def fill_chunk(c, carry):
        sc_ref[c] = jnp.full((KEY_CHUNK, Q_BLOCK), INT_MIN, jnp.int32)
        hi_ref[c] = jnp.full((KEY_CHUNK, Q_BLOCK), -HALF, jnp.int16)
        lo_ref[c] = jnp.full((KEY_CHUNK, Q_BLOCK), -HALF, jnp.int16)
        return carry

    lax.fori_loop(n_chunks, n_padded, fill_chunk, 0)

    def count(pred):
        def body(c, acc):
            hit = jnp.where(pred(sc_ref[c], c), 1, 0)
            return acc + jnp.sum(hit.reshape(KEY_CHUNK // 32, 32, Q_BLOCK), axis=0)
        acc = for_chunks_padded(body, jnp.zeros((32, Q_BLOCK), jnp.int32))
        return jnp.sum(acc, axis=0, keepdims=True)

    def count16(ref16, pred):
        def body(c, acc):
            hit = jnp.where(pred(ref16[c]), jnp.int16(1), jnp.int16(0))
            words = pltpu.bitcast(hit, jnp.int32)
            return acc + jnp.sum(words.reshape(KEY_CHUNK // 64, 32, Q_BLOCK), axis=0)
        acc = for_chunks_padded(body, jnp.zeros((32, Q_BLOCK), jnp.int32))
        acc = jnp.sum(acc, axis=0, keepdims=True)
        return (acc & 0xFFFF) + (acc >> 16)

    def bisect16(ref16, target, n_bits):
        def bit(b, t):
            cand = t + lax.shift_left(jnp.int32(1), 15 - b)
            cand16 = cand.astype(jnp.int16)
            return jnp.where(count16(ref16, lambda v: v >= cand16) >= target, cand, t)
        return lax.fori_loop(0, n_bits, bit, jnp.full((1, Q_BLOCK), -HALF, jnp.int32))

    all_selected = (i + 1) * Q_BLOCK <= topk
    n_bits = jnp.where(all_selected, 0, 16)
    t_hi = bisect16(hi_ref, topk, n_bits)
    t_hi16 = t_hi.astype(jnp.int16)
    above = count16(hi_ref, lambda v: v > t_hi16)

    def low_chunk(c, carry):
        lo2_ref[c] = jnp.where(hi_ref[c] == t_hi16, lo_ref[c], jnp.int16(-HALF))
        return carry

    for_chunks_padded(low_chunk, 0)
    t_lo = bisect16(lo2_ref, topk - above, n_bits)
    thr = lax.shift_left(t_hi, 16) + (t_lo + HALF)

    need = topk - count(lambda k, c: k > thr)
    n_eq = count(lambda k, c: k == thr)
    excess = jnp.max(jnp.where(n_eq > need, 1, 0)) > 0
    idx_bits = (seq - 1).bit_length()

    def tie_bound():
        def bit(b, bound):
            cand = bound + lax.shift_left(jnp.int32(1), idx_bits - 1 - b)
            below = count(lambda k, c: jnp.where(k == thr, c * KEY_CHUNK + krow, seq) < cand)
            return jnp.where(below < need, cand, bound)
        return lax.fori_loop(0, idx_bits, bit, jnp.zeros((1, Q_BLOCK), jnp.int32))

    tie_idx = lax.cond(jnp.logical_and(excess, jnp.logical_not(all_selected)), tie_bound,
                       lambda: jnp.where(all_selected, -1, seq) + jnp.zeros((1, Q_BLOCK), jnp.int32))

    def bias_chunk(c, carry):
        k = sc_ref[c]
        tie_ok = jnp.where(c * KEY_CHUNK + krow <= tie_idx, 0.0, NEG)
        bias_ref[c] = jnp.where(k > thr, 0.0, jnp.where(k == thr, tie_ok, NEG))
        return carry

    for_chunks_padded(bias_chunk, 0)

    m_ref[...] = jnp.full(m_ref.shape, NEG, jnp.float32)
    acc_ref[...] = jnp.zeros(acc_ref.shape, jnp.float32)

    slabs = A_HEADS // HEADS_PER_SLAB
    last_chunk = n_chunks - 1

    def scores(c, s_buf, mx_buf):
        c = jnp.minimum(c, last_chunk)
        kc = keys_ref[c]
        bias2 = jnp.concatenate([bias_ref[c]] * HEADS_PER_SLAB, axis=1)
        for pr in range(slabs):
            s = _dot(kc, qcat_ref[pr]) + bias2
            s_buf[pr] = s
            mx_buf[pr] = jnp.max(s, axis=0, keepdims=True)

    def accumulate(c, s_buf, mx_buf):
        vt = vt_ref[c]
        for pr in range(slabs):
            m_old = m_ref[pr]
            m_new = jnp.maximum(m_old, mx_buf[pr])
            alpha = jnp.exp2(m_old - m_new)
            p = jnp.exp2(s_buf[pr] - m_new)
            m_ref[pr] = m_new
            acc_ref[pr] = acc_ref[pr] * alpha + _dot(vt, _bf16(p))

    scores(0, sa_ref, mxa_ref)

    def attend_pair(t, carry):
        scores(2 * t + 1, sb_ref, mxb_ref)
        accumulate(2 * t, sa_ref, mxa_ref)
        scores(2 * t + 2, sa_ref, mxa_ref)
        accumulate(2 * t + 1, sb_ref, mxb_ref)
        return carry

    lax.fori_loop(0, n_pairs, attend_pair, 0)

    @pl.when(odd)
    def _():
        accumulate(n_chunks - 1, sa_ref, mxa_ref)

    for pr in range(slabs):
        o_t = acc_ref[pr, 0:A_LATENT, :] / acc_ref[pr, A_LATENT:A_LATENT + 1, :]
        for t in range(HEADS_PER_SLAB):
            h = pr * HEADS_PER_SLAB + t
            o_lat = _bf16(o_t[:, t * Q_BLOCK:(t + 1) * Q_BLOCK].T)
            o = _dot(o_lat, wuv_ref[h]) * gate_ref[:, h * A_VDIM:(h + 1) * A_VDIM]
            og_ref[:, h * A_VDIM:(h + 1) * A_VDIM] = _bf16(o)
    out_ref[...] = x_ref[...] + mod_ref[2:3, :] * _dot(og_ref[...], wout_ref[...])


def _attn_a(qcat, qidx, aux, keys, vt, kdup, gate, x2, mod, w_uv, w_out, topk):
    bsz, nq = qcat.shape[:2]
    seq = nq * Q_BLOCK
    d = x2.shape[1]
    nkc = seq // KEY_CHUNK
    row = lambda b, i: (b * nq + i, 0)
    slabs = A_HEADS // HEADS_PER_SLAB
    slab_w = HEADS_PER_SLAB * Q_BLOCK
    once = pl.Buffered(1)
    return pl.pallas_call(
        functools.partial(_attn_a_kernel, topk=topk),
        grid=(bsz, nq),
        in_specs=[pl.BlockSpec((None, None, slabs, A_QPAD, slab_w), lambda b, i: (b, i, 0, 0, 0)),
                  pl.BlockSpec((None, None, LANES, IDX_HEADS * Q_BLOCK), lambda b, i: (b, i, 0, 0)),
                  pl.BlockSpec((Q_BLOCK, LANES), row),
                  pl.BlockSpec((None, nkc, KEY_CHUNK, A_QPAD), lambda b, i: (b, 0, 0, 0), pipeline_mode=once),
                  pl.BlockSpec((None, nkc, V_ROWS, KEY_CHUNK), lambda b, i: (b, 0, 0, 0), pipeline_mode=once),
                  pl.BlockSpec((None, nkc, KEY_CHUNK, LANES), lambda b, i: (b, 0, 0, 0), pipeline_mode=once),
                  pl.BlockSpec((Q_BLOCK, A_WIDTH), row),
                  pl.BlockSpec((Q_BLOCK, d), row),
                  pl.BlockSpec((None, 3, d), lambda b, i: (b, 0, 0)),
                  pl.BlockSpec((A_HEADS, A_LATENT, A_VDIM), lambda b, i: (0, 0, 0), pipeline_mode=once),
                  pl.BlockSpec((A_WIDTH, d), lambda b, i: (0, 0), pipeline_mode=once)],
        out_specs=pl.BlockSpec((Q_BLOCK, d), row),
        out_shape=jax.ShapeDtypeStruct(x2.shape, jnp.float32),
        scratch_shapes=[pltpu.VMEM((nkc, KEY_CHUNK, Q_BLOCK), jnp.int32),
                        pltpu.VMEM((nkc, KEY_CHUNK, Q_BLOCK), jnp.int16),
                        pltpu.VMEM((nkc, KEY_CHUNK, Q_BLOCK), jnp.int16),
                        pltpu.VMEM((nkc, KEY_CHUNK, Q_BLOCK), jnp.int16),
                        pltpu.VMEM((nkc, KEY_CHUNK, Q_BLOCK), jnp.float32),
                        pltpu.VMEM((slabs, KEY_CHUNK, slab_w), jnp.float32),
                        pltpu.VMEM((slabs, KEY_CHUNK, slab_w), jnp.float32),
                        pltpu.VMEM((slabs, 1, slab_w), jnp.float32),
                        pltpu.VMEM((slabs, 1, slab_w), jnp.float32),
                        pltpu.VMEM((slabs, 1, slab_w), jnp.float32),
                        pltpu.VMEM((slabs, V_ROWS, slab_w), jnp.float32),
                        pltpu.VMEM((Q_BLOCK, A_WIDTH), jnp.bfloat16)],
        compiler_params=pltpu.CompilerParams(vmem_limit_bytes=VMEM_LIMIT),
        name="attn_a",
    )(qcat, qidx, aux, keys.reshape(bsz, nkc, KEY_CHUNK, A_QPAD), vt, kdup.reshape(bsz, nkc, KEY_CHUNK, LANES),
      gate, x2, mod, w_uv, w_out)


_PERM_DILS = tuple(dil for _, dil in B_GROUPS if dil > 1)


def _dilation_perms(tm):
    dst = jnp.arange(tm)
    perms = []
    for dil in _PERM_DILS:
        src = (dst % (tm // dil)) * dil + dst // (tm // dil)
        perms.append((src[:, None] == dst[None, :]).astype(jnp.bfloat16))
    return jnp.stack(perms)


_MATMUL_REORDER_DIL = 16


def _store_dilated(val, out_refs, tmp_ref, perm_ref):
    vb = _bf16(val)
    tm, width = vb.shape
    heads, _, hdim = tmp_ref.shape
    in_tmp = False
    for out_ref in out_refs:
        dil = out_ref.shape[1] // width
        rows = tm // dil
        if dil == 1:
            out_ref[...] = vb
        elif dil >= _MATMUL_REORDER_DIL:
            res = _bf16(_dot(perm_ref[_PERM_DILS.index(dil)], vb))
            for r in range(dil):
                out_ref[:, r * width:(r + 1) * width] = res[r * rows:(r + 1) * rows, :]
        else:
            if not in_tmp:
                for hd in range(heads):
                    tmp_ref[hd] = val[:, hd * hdim:(hd + 1) * hdim]
                in_tmp = True
            for r in range(dil):
                for hd in range(heads):
                    col = r * width + hd * hdim
                    out_ref[:, col:col + hdim] = _bf16(tmp_ref[hd, pl.ds(r, rows, stride=dil), :])


def _proj_b_kernel(*refs, with_kv):
    n_g = len(B_GROUPS)
    if with_kv:
        h_ref, mod_ref, g_ref, kvg_ref, wkv_ref, win_ref, cos_ref, sin_ref, perm_ref = refs[:9]
        outs = refs[9:]
        k_refs, v_refs, outs = outs[:n_g], outs[n_g:2 * n_g], outs[2 * n_g:]
    else:
        h_ref, mod_ref, g_ref, win_ref, cos_ref, sin_ref, perm_ref = refs[:7]
        outs = refs[7:]
    q_refs, gate_ref, tmp_ref = outs[:n_g], outs[n_g], outs[n_g + 1]
    h = h_ref[...]
    cos, sin = cos_ref[...], sin_ref[...]
    width = B_WIDTH

    def rope(t, scale):
        return jnp.concatenate([_rope_lanes(t[:, hd * B_HDIM:(hd + 1) * B_HDIM], cos, sin, B_HDIM) * scale
                                for hd in range(B_HEADS)], axis=1)

    if with_kv:
        kvn = _bf16(h * lax.rsqrt(jnp.mean(h * h, axis=-1, keepdims=True) + EPS) * kvg_ref[...])
        _store_dilated(rope(_dot(kvn, wkv_ref[:, 0:width]), 1.0), k_refs, tmp_ref, perm_ref)
        _store_dilated(_dot(kvn, wkv_ref[:, width:2 * width]), v_refs, tmp_ref, perm_ref)
    hn = _bf16(_prenorm(h, g_ref[...], mod_ref))
    qscale = B_HDIM ** -0.5 * LOG2E
    for gi, q_ref in enumerate(q_refs):
        _store_dilated(rope(_dot(hn, win_ref[:, gi * width:(gi + 1) * width]), qscale), [q_ref], tmp_ref, perm_ref)
    gate_ref[...] = _bf16(_silu(_dot(hn, win_ref[:, n_g * width:(n_g + 1) * width])))


def _proj_b(h2, mod, g, kvg, w_kv, w_in, cos, sin, bsz, seq, with_kv, tm=256):
    n, d = h2.shape
    tps = seq // tm
    row = lambda i: (i, 0)
    const = lambda i: (0, 0)
    in_specs = [pl.BlockSpec((tm, d), row), pl.BlockSpec((None, 3, d), lambda i: (i // tps, 0, 0)),
                pl.BlockSpec((1, d), const)]
    args = [h2, mod, g]
    if with_kv:
        in_specs += [pl.BlockSpec((1, d), const), pl.BlockSpec(w_kv.shape, const)]
        args += [kvg, w_kv]
    perms = _dilation_perms(tm)
    table = pl.BlockSpec((None, tm, LANES), lambda i: (ROPE_DIMS.index(B_HDIM), i, 0))
    in_specs += [pl.BlockSpec(w_in.shape, const), table, table, pl.BlockSpec(perms.shape, lambda i: (0, 0, 0))]
    args += [w_in, cos, sin, perms]
    dil_specs = [pl.BlockSpec((None, tm // dil, dil * B_WIDTH), lambda i: (i // tps, i % tps, 0)) for _, dil in B_GROUPS]
    dil_shapes = [jax.ShapeDtypeStruct((bsz, seq // dil, dil * B_WIDTH), jnp.bfloat16) for _, dil in B_GROUPS]
    n_sets = 3 if with_kv else 1
    return pl.pallas_call(
        functools.partial(_proj_b_kernel, with_kv=with_kv),
        grid=(n // tm,),
        in_specs=in_specs,
        out_specs=dil_specs * n_sets + [pl.BlockSpec((tm, B_WIDTH), row)],
        out_shape=dil_shapes * n_sets + [jax.ShapeDtypeStruct((n, B_WIDTH), jnp.bfloat16)],
        scratch_shapes=[pltpu.VMEM((B_HEADS, tm, B_HDIM), jnp.float32)],
        compiler_params=pltpu.CompilerParams(vmem_limit_bytes=VMEM_LIMIT),
        name="proj_b",
    )(*args)


def _dilated_kernel(q_ref, kc_ref, vc_ref, o_ref, lse_ref, kp_ref, vp_ref):
    j = pl.program_id(2)
    w = q_ref.shape[0]

    @pl.when(j == 0)
    def _():
        kp_ref[...] = jnp.zeros(kp_ref.shape, kp_ref.dtype)
        vp_ref[...] = jnp.zeros(vp_ref.shape, vp_ref.dtype)

    qi = lax.broadcasted_iota(jnp.int32, (w, w), 0)
    ki = lax.broadcasted_iota(jnp.int32, (w, w), 1)
    bias = jnp.concatenate([jnp.where(jnp.logical_and(ki >= qi, j > 0), 0.0, NEG),
                            jnp.where(ki <= qi, 0.0, NEG)], axis=1)
    ones = jnp.ones((2 * w, B_HDIM), jnp.bfloat16)
    lses = []
    for hd in range(B_HEADS):
        sl = slice(hd * B_HDIM, (hd + 1) * B_HDIM)
        keys = jnp.concatenate([kp_ref[:, sl], kc_ref[:, sl]], axis=0)
        vals = jnp.concatenate([vp_ref[:, sl], vc_ref[:, sl]], axis=0)
        s = _dot_nt(q_ref[:, sl], keys) + bias
        m = jnp.max(s, axis=-1, keepdims=True)
        p = _bf16(jnp.exp2(s - m))
        ol = _dot(p, jnp.concatenate([vals, ones], axis=1))
        l = ol[:, B_HDIM:]
        o_ref[:, sl] = _bf16(ol[:, :B_HDIM] / l)
        lses.append(m + jnp.log2(l[:, 0:1]))
    lse_ref[...] = jnp.concatenate(lses + [jnp.zeros((w, LANES - B_HEADS), jnp.float32)], axis=1)
    kp_ref[...] = kc_ref[...]
    vp_ref[...] = vc_ref[...]


def _dilated(q, k, v, window, dil):
    bsz, n, _ = q.shape
    w = window // dil
    nb = n // w
    cur = lambda b, r, j: (b, j, r)
    blk = pl.BlockSpec((None, w, B_WIDTH), cur)
    return pl.pallas_call(
        _dilated_kernel,
        grid=(bsz, dil, nb),
        in_specs=[blk, blk, blk],
        out_specs=[blk, pl.BlockSpec((None, None, w, LANES), lambda b, r, j: (b, r, j, 0))],
        out_shape=[jax.ShapeDtypeStruct(q.shape, jnp.bfloat16),
                   jax.ShapeDtypeStruct((bsz, dil, n, LANES), jnp.float32)],
        scratch_shapes=[pltpu.VMEM((w, B_WIDTH), jnp.bfloat16), pltpu.VMEM((w, B_WIDTH), jnp.bfloat16)],
        compiler_params=pltpu.CompilerParams(dimension_semantics=("arbitrary", "arbitrary", "arbitrary")),
        name=f"dilated_{dil}",
    )(q, k, v)


def _merge_kernel(*refs, final):
    n_g = len(B_GROUPS)
    o_refs, lse_refs = refs[:n_g], refs[n_g:2 * n_g]
    gate_ref, h_ref, mod_ref, wout_ref, fin_ref, unperm_ref, out_ref, ls_ref, og_ref = refs[2 * n_g:]
    tm = h_ref.shape[0]
    o_seq = []
    for gi, (_, dil) in enumerate(B_GROUPS):
        rows = tm // dil
        for r in range(dil):
            dst = pl.ds(r, rows, stride=dil) if dil > 1 else slice(None)
            ls_ref[gi, dst, :] = lse_refs[gi][r]
        if dil == 1:
            o_seq.append(o_refs[gi][...].astype(jnp.float32))
        else:
            stacked = jnp.concatenate([o_refs[gi][:, r * B_WIDTH:(r + 1) * B_WIDTH] for r in range(dil)], axis=0)
            o_seq.append(_dot(unperm_ref[_PERM_DILS.index(dil)], stacked))
    lses = [ls_ref[gi] for gi in range(n_g)]
    m = functools.reduce(jnp.maximum, lses)
    e = [jnp.exp2(t - m) for t in lses]
    inv = 1.0 / functools.reduce(lambda a, b: a + b, e)
    for hd in range(B_HEADS):
        sl = slice(hd * B_HDIM, (hd + 1) * B_HDIM)
        o = functools.reduce(lambda a, b: a + b,
                             [(e[gi] * inv)[:, hd:hd + 1] * o_seq[gi][:, sl] for gi in range(n_g)])
        og_ref[:, sl] = _bf16(o * gate_ref[:, sl])
    h = h_ref[...] + mod_ref[2:3, :] * _dot(og_ref[...], wout_ref[...])
    if final:
        h = h * lax.rsqrt(jnp.mean(h * h, axis=-1, keepdims=True) + EPS) * fin_ref[...]
    out_ref[...] = h


def _merge(os_, lses, gate, h2, mod, w_out, fin, seq, final, tm=256):
    n, d = h2.shape
    tps = seq // tm
    row = lambda i: (i, 0)
    rows = pl.BlockSpec((tm, d), row)
    n_g = len(B_GROUPS)
    o_specs = [pl.BlockSpec((None, tm // dil, dil * B_WIDTH), lambda i: (i // tps, i % tps, 0)) for _, dil in B_GROUPS]
    lse_specs = [pl.BlockSpec((None, dil, tm // dil, LANES), lambda i: (i // tps, 0, i % tps, 0)) for _, dil in B_GROUPS]
    unperms = jnp.swapaxes(_dilation_perms(tm), 1, 2)
    return pl.pallas_call(
        functools.partial(_merge_kernel, final=final),
        grid=(n // tm,),
        in_specs=o_specs + lse_specs + [rows, rows, pl.BlockSpec((None, 3, d), lambda i: (i // tps, 0, 0)),
                                        pl.BlockSpec(w_out.shape, lambda i: (0, 0)),
                                        pl.BlockSpec((1, d), lambda i: (0, 0)),
                                        pl.BlockSpec(unperms.shape, lambda i: (0, 0, 0))],
        out_specs=rows,
        out_shape=jax.ShapeDtypeStruct((n, d), jnp.float32),
        scratch_shapes=[pltpu.VMEM((n_g, tm, LANES), jnp.float32),
                        pltpu.VMEM((tm, B_WIDTH), jnp.bfloat16)],
        compiler_params=pltpu.CompilerParams(vmem_limit_bytes=VMEM_LIMIT),
        name="merge_out",
    )(*os_, *lses, gate, h2, mod, w_out, fin, unperms)


def _final_norm_kernel(h_ref, g_ref, out_ref):
    h = h_ref[...]
    out_ref[...] = h * lax.rsqrt(jnp.mean(h * h, axis=-1, keepdims=True) + EPS) * g_ref[...]


def kernel(x, c, positions, a_norm, a_ada_w, a_ada_b, a_w_in, a_kv_norm, a_w_uv, a_w_out, kv_norm, w_kv, b_norm, b_ada_w, b_ada_b, b_w_in, b_w_out, final_norm):
    bsz, seq, d = x.shape
    n = bsz * seq
    n_a, n_b = a_norm.shape[0], b_norm.shape[0]
    topk = min(TOPK_MAX, seq // 4)
    assert topk % Q_BLOCK == 0 and seq % (KEY_CHUNK * COUNT_UNROLL) == 0 and d == B_WIDTH
    assert all(seq % window == 0 for window, _ in B_GROUPS)

    cos, sin = _rope_tables(positions, ROPE_DIMS)
    h = x.reshape(n, d)
    for li in range(n_a):
        mod = _ada(c, a_ada_w[li], a_ada_b[li])
        qcat, keys, vt, kdup, qidx, aux, gate = _proj_a(
            h, mod, a_norm[li].reshape(1, d), a_kv_norm[li].reshape(1, A_LATENT), _a_weight(a_w_in[li]),
            cos, sin, bsz, seq)
        h = _attn_a(qcat, qidx, aux, keys, vt, kdup, gate, h, mod, _bf16(a_w_uv[li]), _bf16(a_w_out[li]), topk)
    k_sh = v_sh = None
    for li in range(n_b):
        mod = _ada(c, b_ada_w[li], b_ada_b[li])
        outs = _proj_b(h, mod, b_norm[li].reshape(1, d), kv_norm.reshape(1, d), _bf16(w_kv), _bf16(b_w_in[li]),
                       cos, sin, bsz, seq, with_kv=(li == 0))
        n_g = len(B_GROUPS)
        if li == 0:
            k_sh, v_sh = outs[:n_g], outs[n_g:2 * n_g]
            outs = outs[2 * n_g:]
        q_groups, gate = outs[:n_g], outs[n_g]
        res = [_dilated(qg, kg, vg, window, dil)
               for qg, kg, vg, (window, dil) in zip(q_groups, k_sh, v_sh, B_GROUPS)]
        h = _merge([r[0] for r in res], [r[1] for r in res], gate, h, mod, _bf16(b_w_out[li]),
                   final_norm.reshape(1, d), seq, final=(li == n_b - 1))
    if n_b == 0:
        h = pl.pallas_call(
            _final_norm_kernel, grid=(n // 256,),
            in_specs=[pl.BlockSpec((256, d), lambda i: (i, 0)), pl.BlockSpec((1, d), lambda i: (0, 0))],
            out_specs=pl.BlockSpec((256, d), lambda i: (i, 0)),
            out_shape=jax.ShapeDtypeStruct((n, d), jnp.float32), name="final_norm",
        )(h, final_norm.reshape(1, d))
    return h.reshape(bsz, seq, d)
```

```python
import functools
import math

import jax
import jax.numpy as jnp
from jax import lax
from jax.experimental import pallas as pl
from jax.experimental.pallas import tpu as pltpu

ROPE_THETA = 10000.0
EPS = 1e-6
LOG2E = 1.4426950408889634

A_HEADS = 16
A_LATENT = 128
A_ROPE = 32
A_VDIM = 128
A_WIDTH = A_HEADS * A_VDIM
IDX_HEADS = 8
IDX_DIM = 64
TOPK_MAX = 256
Q_BLOCK = 256
KEY_CHUNK = 256
A_QPAD = 256
SLAB_LANES = 256
HEADS_PER_SLAB = SLAB_LANES // Q_BLOCK
V_ROWS = A_LATENT + 16

B_HEADS = 8
B_HDIM = 128
B_GROUPS = ((128, 1), (512, 4), (2048, 16))
B_WIDTH = B_HEADS * B_HDIM
ROPE_DIMS = (A_ROPE, IDX_DIM, B_HDIM)

LANES = 128
NEG = -1e30
INT_MIN = -(2 ** 31)
HALF = 2 ** 15
VMEM_LIMIT = 56 * 1024 * 1024

AUX_W_OFF = A_ROPE


def _bf16(t):
    return t.astype(jnp.bfloat16)


def _dot(a, b):
    return jnp.dot(a, b, preferred_element_type=jnp.float32)


def _dot_nt(a, b):
    return lax.dot_general(a, b, (((1,), (1,)), ((), ())), preferred_element_type=jnp.float32)


def _silu(t):
    return t * (1.0 / (1.0 + jnp.exp(-t)))


def _lane_iota(shape):
    return lax.broadcasted_iota(jnp.int32, shape, len(shape) - 1)


def _ada_kernel(c_ref, w_ref, b_ref, out_ref):
    out_ref[...] = _dot(_bf16(_silu(c_ref[...])), _bf16(w_ref[...])) + b_ref[...]


def _ada(c, w, b):
    bsz, d = c.shape
    out = pl.pallas_call(
        _ada_kernel,
        grid=(3,),
        in_specs=[pl.BlockSpec((bsz, d), lambda j: (0, 0)),
                  pl.BlockSpec((d, d), lambda j: (0, j)),
                  pl.BlockSpec((1, d), lambda j: (0, j))],
        out_specs=pl.BlockSpec((bsz, d), lambda j: (0, j)),
        out_shape=jax.ShapeDtypeStruct((bsz, 3 * d), jnp.float32),
        name="ada_mod",
    )(c, w, b.reshape(1, 3 * d))
    return out.reshape(bsz, 3, d)


def _rope_table_kernel(pos_ref, inv_ref, sel_ref, sign_ref, cos_ref, sin_ref):
    half_rows = pos_ref.shape[0] // 2
    upper = _lane_iota((half_rows, LANES)) >= LANES // 2
    ang = jnp.where(upper, pos_ref[half_rows:, :], pos_ref[:half_rows, :]) * inv_ref[...]
    cos, sin = jnp.cos(ang), jnp.sin(ang)
    for t in range(cos_ref.shape[0]):
        for part in range(2):
            pick = lambda v: jnp.dot(v, sel_ref[part, t], precision=lax.Precision.HIGHEST,
                                     preferred_element_type=jnp.float32)
            rows = slice(part * half_rows, (part + 1) * half_rows)
            cos_ref[t, rows, :] = pick(cos)
            sin_ref[t, rows, :] = pick(sin) * sign_ref[t]


def _rope_tables(positions, dims, tm=512):
    n = positions.size
    base = max(dims)
    assert base == LANES and all(base % d == 0 and d & (d - 1) == 0 for d in dims)
    posf = jnp.broadcast_to(positions.reshape(n, 1).astype(jnp.float32), (n, LANES))
    inv = ROPE_THETA ** (-jnp.arange(0, base, 2, dtype=jnp.float32) / base)
    lane = jnp.arange(LANES)
    inv = inv[lane % (base // 2)].reshape(1, LANES)
    src = [((lane % d) % (d // 2)) * (base // d) for d in dims]
    sel = jnp.stack([jnp.stack([(lane[:, None] == (s + part * (base // 2))[None, :]).astype(jnp.float32) for s in src])
                     for part in range(2)])
    sign = jnp.stack([jnp.where(lane % d < d // 2, -1.0, 1.0) for d in dims]).astype(jnp.float32)
    sign = sign.reshape(len(dims), 1, LANES)
    shape = jax.ShapeDtypeStruct((len(dims), n, LANES), jnp.float32)
    return pl.pallas_call(
        _rope_table_kernel,
        grid=(n // tm,),
        in_specs=[pl.BlockSpec((tm, LANES), lambda i: (i, 0)),
                  pl.BlockSpec((1, LANES), lambda i: (0, 0)),
                  pl.BlockSpec(sel.shape, lambda i: (0, 0, 0, 0)),
                  pl.BlockSpec(sign.shape, lambda i: (0, 0, 0))],
        out_specs=[pl.BlockSpec((len(dims), tm, LANES), lambda i: (0, i, 0))] * 2,
        out_shape=[shape, shape],
        name="rope_tables",
    )(posf, inv, sel, sign)


def _rope_lanes(xv, cos, sin_signed, dim):
    half = dim // 2
    if dim == LANES:
        partner = pltpu.roll(xv, half, axis=1)
    else:
        first = (_lane_iota(xv.shape) % dim) < half
        partner = jnp.where(first, pltpu.roll(xv, LANES - half, axis=1), pltpu.roll(xv, half, axis=1))
    return xv * cos + partner * sin_signed


def _prenorm(h, g, mod_ref):
    y = h * lax.rsqrt(jnp.mean(h * h, axis=-1, keepdims=True) + EPS) * g
    return y * (1.0 + mod_ref[1:2, :]) + mod_ref[0:1, :]


_A_COLS = {}
_o = 0
for _name, _w in (("q_lat", A_HEADS * A_LATENT), ("q_rope", A_HEADS * A_ROPE), ("c_kv", A_LATENT),
                  ("aux", LANES), ("kdup", LANES), ("gate", A_WIDTH), ("q_idx", IDX_HEADS * IDX_DIM)):
    _A_COLS[_name] = (_o, _o + _w)
    _o += _w
A_COLS_TOTAL = _o


def _a_weight(w_in):
    sizes = (A_HEADS * A_LATENT, A_HEADS * A_ROPE, A_LATENT, A_ROPE, A_WIDTH, IDX_HEADS * IDX_DIM, IDX_DIM, IDX_HEADS)
    parts, o = [], 0
    for s in sizes:
        parts.append(w_in[:, o:o + s])
        o += s
    q_lat, q_rope, c_kv, k_rope, gate, q_idx, k_idx, w_idx = parts
    d = w_in.shape[0]
    aux = jnp.concatenate([k_rope, w_idx, jnp.zeros((d, LANES - A_ROPE - IDX_HEADS), w_in.dtype)], axis=1)
    kdup = jnp.concatenate([k_idx, k_idx], axis=1)
    return _bf16(jnp.concatenate([q_lat, q_rope, c_kv, aux, kdup, gate, q_idx], axis=1))


def _proj_a_kernel(x_ref, mod_ref, g_ref, kvg_ref, w_ref, cos_ref, sin_ref,
                   qcat_ref, keys_ref, vt_ref, kdup_ref, qidx_ref, aux_ref, gate_ref):
    hn = _bf16(_prenorm(x_ref[...], g_ref[...], mod_ref))
    cos32, sin32 = cos_ref[0], sin_ref[0]
    cos64, sin64 = cos_ref[1], sin_ref[1]
    lane = _lane_iota(cos32.shape)
    qscale = (A_LATENT + A_ROPE) ** -0.5 * LOG2E

    def proj(name):
        lo, hi = _A_COLS[name]
        return _dot(hn, w_ref[:, lo:hi])

    q_blocks = x_ref.shape[0] // Q_BLOCK

    def put_q(h, row0, piece_t):
        slab, col0 = h // HEADS_PER_SLAB, (h % HEADS_PER_SLAB) * Q_BLOCK
        for qb in range(q_blocks):
            qcat_ref[qb, slab, row0:row0 + piece_t.shape[0], col0:col0 + Q_BLOCK] = _bf16(
                piece_t[:, qb * Q_BLOCK:(qb + 1) * Q_BLOCK])

    q_lat = proj("q_lat") * qscale
    q_rope = proj("q_rope")
    per_slab = LANES // A_ROPE
    pad = jnp.zeros((A_QPAD - A_LATENT - A_ROPE, x_ref.shape[0]), jnp.float32)
    for h in range(A_HEADS):
        put_q(h, 0, q_lat[:, h * A_LATENT:(h + 1) * A_LATENT].T)
    for j in range(A_HEADS // per_slab):
        roped_t = (_rope_lanes(q_rope[:, j * LANES:(j + 1) * LANES], cos32, sin32, A_ROPE) * qscale).T
        for t in range(per_slab):
            put_q(j * per_slab + t, A_LATENT, roped_t[t * A_ROPE:(t + 1) * A_ROPE, :])
            put_q(j * per_slab + t, A_LATENT + A_ROPE, pad)

    c_kv = proj("c_kv")
    c_kv = c_kv * lax.rsqrt(jnp.mean(c_kv * c_kv, axis=-1, keepdims=True) + EPS) * kvg_ref[...]
    keys_ref[:, 0:A_LATENT] = _bf16(c_kv)
    vt_ref[0:A_LATENT, :] = _bf16(c_kv.T)
    vt_ref[A_LATENT:V_ROWS, :] = jnp.ones((V_ROWS - A_LATENT, vt_ref.shape[1]), jnp.bfloat16)
    aux = proj("aux")
    k_rope = _rope_lanes(aux, cos32, sin32, A_ROPE)
    keys_ref[:, A_LATENT:A_QPAD] = _bf16(jnp.where(lane < A_ROPE, k_rope, 0.0))
    aux_ref[...] = aux * IDX_HEADS ** -0.5

    kdup_ref[...] = _bf16(_rope_lanes(proj("kdup"), cos64, sin64, IDX_DIM))
    q_idx = proj("q_idx")
    per_slab_i = LANES // IDX_DIM
    feat = lax.broadcasted_iota(jnp.int32, (LANES, x_ref.shape[0]), 0)
    for j in range(IDX_HEADS // per_slab_i):
        roped_t = _rope_lanes(q_idx[:, j * LANES:(j + 1) * LANES], cos64, sin64, IDX_DIM).T
        for t in range(per_slab_i):
            keep = (feat >= t * IDX_DIM) & (feat < (t + 1) * IDX_DIM)
            head_t = _bf16(jnp.where(keep, roped_t, 0.0))
            h = j * per_slab_i + t
            for qb in range(q_blocks):
                qidx_ref[qb, :, h * Q_BLOCK:(h + 1) * Q_BLOCK] = head_t[:, qb * Q_BLOCK:(qb + 1) * Q_BLOCK]

    gate_ref[...] = _bf16(_silu(proj("gate")))


def _proj_a(x2, mod, g, kvg, w, cos, sin, bsz, seq):
    n, d = x2.shape
    tm = KEY_CHUNK
    tps = seq // tm
    row = lambda i: (i, 0)
    qbt = tm // Q_BLOCK
    nq = seq // Q_BLOCK
    slabs, slab_w = A_HEADS // HEADS_PER_SLAB, HEADS_PER_SLAB * Q_BLOCK
    return pl.pallas_call(
        _proj_a_kernel,
        grid=(n // tm,),
        in_specs=[pl.BlockSpec((tm, d), row),
                  pl.BlockSpec((None, 3, d), lambda i: (i // tps, 0, 0)),
                  pl.BlockSpec((1, d), lambda i: (0, 0)),
                  pl.BlockSpec((1, A_LATENT), lambda i: (0, 0)),
                  pl.BlockSpec((d, A_COLS_TOTAL), lambda i: (0, 0)),
                  pl.BlockSpec((2, tm, LANES), lambda i: (0, i, 0)),
                  pl.BlockSpec((2, tm, LANES), lambda i: (0, i, 0))],
        out_specs=[pl.BlockSpec((None, qbt, slabs, A_QPAD, slab_w), lambda i: (i // tps, i % tps, 0, 0, 0)),
                   pl.BlockSpec((tm, A_QPAD), row),
                   pl.BlockSpec((None, None, V_ROWS, tm), lambda i: (i // tps, i % tps, 0, 0)),
                   pl.BlockSpec((tm, LANES), row),
                   pl.BlockSpec((None, qbt, LANES, IDX_HEADS * Q_BLOCK), lambda i: (i // tps, i % tps, 0, 0)),
                   pl.BlockSpec((tm, LANES), row),
                   pl.BlockSpec((tm, A_WIDTH), row)],
        out_shape=[jax.ShapeDtypeStruct((bsz, nq, slabs, A_QPAD, slab_w), jnp.bfloat16),
                   jax.ShapeDtypeStruct((n, A_QPAD), jnp.bfloat16),
                   jax.ShapeDtypeStruct((bsz, seq // tm, V_ROWS, tm), jnp.bfloat16),
                   jax.ShapeDtypeStruct((n, LANES), jnp.bfloat16),
                   jax.ShapeDtypeStruct((bsz, nq, LANES, IDX_HEADS * Q_BLOCK), jnp.bfloat16),
                   jax.ShapeDtypeStruct((n, LANES), jnp.float32),
                   jax.ShapeDtypeStruct((n, A_WIDTH), jnp.bfloat16)],
        compiler_params=pltpu.CompilerParams(vmem_limit_bytes=VMEM_LIMIT),
        name="proj_a",
    )(x2, mod, g, kvg, w, cos, sin)


def _attn_a_kernel(qcat_ref, qidx_ref, aux_ref, keys_ref, vt_ref, kdup_ref, gate_ref, x_ref, mod_ref, wuv_ref,
                   wout_ref, out_ref, sc_ref, hi_ref, lo_ref, lo2_ref, bias_ref, sa_ref, sb_ref, mxa_ref, mxb_ref, m_ref, acc_ref, og_ref, *, topk):
    i = pl.program_id(1)
    n_chunks = ((i + 1) * Q_BLOCK + KEY_CHUNK - 1) // KEY_CHUNK
    seq = keys_ref.shape[0] * KEY_CHUNK

    w_t = aux_ref[...].T
    qi_all = qidx_ref[...]
    qpos = i * Q_BLOCK + lax.broadcasted_iota(jnp.int32, (KEY_CHUNK, Q_BLOCK), 1)
    krow = lax.broadcasted_iota(jnp.int32, (KEY_CHUNK, Q_BLOCK), 0)

    n_pairs = n_chunks // 2
    odd = n_chunks % 2 == 1

    def for_chunk_pairs(fn, init):
        def body(t, carry):
            return fn(2 * t + 1, fn(2 * t, carry))
        carry = lax.fori_loop(0, n_pairs, body, init)
        return lax.cond(odd, lambda cr: fn(n_chunks - 1, cr), lambda cr: cr, carry)

    def score_chunk(c, carry):
        rel = _dot(kdup_ref[c], qi_all)
        acc = jnp.zeros((KEY_CHUNK, Q_BLOCK), jnp.float32)
        for h in range(IDX_HEADS):
            acc = acc + w_t[AUX_W_OFF + h:AUX_W_OFF + h + 1, :] * jnp.maximum(rel[:, h * Q_BLOCK:(h + 1) * Q_BLOCK], 0.0)
        bits = lax.bitcast_convert_type(acc, jnp.int32)
        key = bits ^ ((bits >> 31) & 0x7FFFFFFF)
        key = jnp.where(c * KEY_CHUNK + krow <= qpos, key, INT_MIN)
        sc_ref[c] = key
        hi_ref[c] = (key >> 16).astype(jnp.int16)
        lo_ref[c] = ((key & 0xFFFF) - HALF).astype(jnp.int16)
        return carry

    for_chunk_pairs(score_chunk, 0)

    def count(pred):
        def body(c, acc):
            hit = jnp.where(pred(sc_ref[c], c), 1, 0)
            return acc + jnp.sum(hit.reshape(KEY_CHUNK // 32, 32, Q_BLOCK), axis=0)
        acc = for_chunk_pairs(body, jnp.zeros((32, Q_BLOCK), jnp.int32))
        return jnp.sum(acc, axis=0, keepdims=True)

    def count16(ref16, pred):
        def body(c, acc):
            hit = jnp.where(pred(ref16[c]), jnp.int16(1), jnp.int16(0))
            words = pltpu.bitcast(hit, jnp.int32)
            return acc + jnp.sum(words.reshape(KEY_CHUNK // 64, 32, Q_BLOCK), axis=0)
        acc = for_chunk_pairs(body, jnp.zeros((32, Q_BLOCK), jnp.int32))
        acc = jnp.sum(acc, axis=0, keepdims=True)
        return (acc & 0xFFFF) + (acc >> 16)

    def bisect16(ref16, target, n_bits):
        def bit(b, t):
            cand = t + lax.shift_left(jnp.int32(1), 15 - b)
            cand16 = cand.astype(jnp.int16)
            return jnp.where(count16(ref16, lambda v: v >= cand16) >= target, cand, t)
        return lax.fori_loop(0, n_bits, bit, jnp.full((1, Q_BLOCK), -HALF, jnp.int32))

    all_selected = (i + 1) * Q_BLOCK <= topk
    n_bits = jnp.where(all_selected, 0, 16)
    t_hi = bisect16(hi_ref, topk, n_bits)
    t_hi16 = t_hi.astype(jnp.int16)
    above = count16(hi_ref, lambda v: v > t_hi16)

    def low_chunk(c, carry):
        lo2_ref[c] = jnp.where(hi_ref[c] == t_hi16, lo_ref[c], jnp.int16(-HALF))
        return carry

    for_chunk_pairs(low_chunk, 0)
    t_lo = bisect16(lo2_ref, topk - above, n_bits)
    thr = lax.shift_left(t_hi, 16) + (t_lo + HALF)

    need = topk - count(lambda k, c: k > thr)
    n_eq = count(lambda k, c: k == thr)
    excess = jnp.max(jnp.where(n_eq > need, 1, 0)) > 0
    idx_bits = (seq - 1).bit_length()

    def tie_bound():
        def bit(b, bound):
            cand = bound + lax.shift_left(jnp.int32(1), idx_bits - 1 - b)
            below = count(lambda k, c: jnp.where(k == thr, c * KEY_CHUNK + krow, seq) < cand)
            return jnp.where(below < need, cand, bound)
        return lax.fori_loop(0, idx_bits, bit, jnp.zeros((1, Q_BLOCK), jnp.int32))

    tie_idx = lax.cond(jnp.logical_and(excess, jnp.logical_not(all_selected)), tie_bound,
                       lambda: jnp.where(all_selected, -1, seq) + jnp.zeros((1, Q_BLOCK), jnp.int32))

    def bias_chunk(c, carry):
        k = sc_ref[c]
        tie_ok = jnp.where(c * KEY_CHUNK + krow <= tie_idx, 0.0, NEG)
        bias_ref[c] = jnp.where(k > thr, 0.0, jnp.where(k == thr, tie_ok, NEG))
        return carry

    for_chunk_pairs(bias_chunk, 0)

    m_ref[...] = jnp.full(m_ref.shape, NEG, jnp.float32)
    acc_ref[...] = jnp.zeros(acc_ref.shape, jnp.float32)

    slabs = A_HEADS // HEADS_PER_SLAB
    last_chunk = n_chunks - 1

    def scores(c, s_buf, mx_buf):
        c = jnp.minimum(c, last_chunk)
        kc = keys_ref[c]
        bias2 = jnp.concatenate([bias_ref[c]] * HEADS_PER_SLAB, axis=1)
        for pr in range(slabs):
            s = _dot(kc, qcat_ref[pr]) + bias2
            s_buf[pr] = s
            mx_buf[pr] = jnp.max(s, axis=0, keepdims=True)

    def accumulate(c, s_buf, mx_buf):
        vt = vt_ref[c]
        for pr in range(slabs):
            m_old = m_ref[pr]
            m_new = jnp.maximum(m_old, mx_buf[pr])
            alpha = jnp.exp2(m_old - m_new)
            p = jnp.exp2(s_buf[pr] - m_new)
            m_ref[pr] = m_new
            acc_ref[pr] = acc_ref[pr] * alpha + _dot(vt, _bf16(p))

    scores(0, sa_ref, mxa_ref)

    def attend_pair(t, carry):
        scores(2 * t + 1, sb_ref, mxb_ref)
        accumulate(2 * t, sa_ref, mxa_ref)
        scores(2 * t + 2, sa_ref, mxa_ref)
        accumulate(2 * t + 1, sb_ref, mxb_ref)
        return carry

    lax.fori_loop(0, n_pairs, attend_pair, 0)

    @pl.when(odd)
    def _():
        accumulate(n_chunks - 1, sa_ref, mxa_ref)

    for pr in range(slabs):
        o_t = acc_ref[pr, 0:A_LATENT, :] / acc_ref[pr, A_LATENT:A_LATENT + 1, :]
        for t in range(HEADS_PER_SLAB):
            h = pr * HEADS_PER_SLAB + t
            o_lat = _bf16(o_t[:, t * Q_BLOCK:(t + 1) * Q_BLOCK].T)
            o = _dot(o_lat, wuv_ref[h]) * gate_ref[:, h * A_VDIM:(h + 1) * A_VDIM]
            og_ref[:, h * A_VDIM:(h + 1) * A_VDIM] = _bf16(o)
    out_ref[...] = x_ref[...] + mod_ref[2:3, :] * _dot(og_ref[...], wout_ref[...])


def _attn_a(qcat, qidx, aux, keys, vt, kdup, gate, x2, mod, w_uv, w_out, topk):
    bsz, nq = qcat.shape[:2]
    seq = nq * Q_BLOCK
    d = x2.shape[1]
    nkc = seq // KEY_CHUNK
    row = lambda b, i: (b * nq + i, 0)
    slabs = A_HEADS // HEADS_PER_SLAB
    slab_w = HEADS_PER_SLAB * Q_BLOCK
    once = pl.Buffered(1)
    return pl.pallas_call(
        functools.partial(_attn_a_kernel, topk=topk),
        grid=(bsz, nq),
        in_specs=[pl.BlockSpec((None, None, slabs, A_QPAD, slab_w), lambda b, i: (b, i, 0, 0, 0)),
                  pl.BlockSpec((None, None, LANES, IDX_HEADS * Q_BLOCK), lambda b, i: (b, i, 0, 0)),
                  pl.BlockSpec((Q_BLOCK, LANES), row),
                  pl.BlockSpec((None, nkc, KEY_CHUNK, A_QPAD), lambda b, i: (b, 0, 0, 0), pipeline_mode=once),
                  pl.BlockSpec((None, nkc, V_ROWS, KEY_CHUNK), lambda b, i: (b, 0, 0, 0), pipeline_mode=once),
                  pl.BlockSpec((None, nkc, KEY_CHUNK, LANES), lambda b, i: (b, 0, 0, 0), pipeline_mode=once),
                  pl.BlockSpec((Q_BLOCK, A_WIDTH), row),
                  pl.BlockSpec((Q_BLOCK, d), row),
                  pl.BlockSpec((None, 3, d), lambda b, i: (b, 0, 0)),
                  pl.BlockSpec((A_HEADS, A_LATENT, A_VDIM), lambda b, i: (0, 0, 0), pipeline_mode=once),
                  pl.BlockSpec((A_WIDTH, d), lambda b, i: (0, 0), pipeline_mode=once)],
        out_specs=pl.BlockSpec((Q_BLOCK, d), row),
        out_shape=jax.ShapeDtypeStruct(x2.shape, jnp.float32),
        scratch_shapes=[pltpu.VMEM((nkc, KEY_CHUNK, Q_BLOCK), jnp.int32),
                        pltpu.VMEM((nkc, KEY_CHUNK, Q_BLOCK), jnp.int16),
                        pltpu.VMEM((nkc, KEY_CHUNK, Q_BLOCK), jnp.int16),
                        pltpu.VMEM((nkc, KEY_CHUNK, Q_BLOCK), jnp.int16),
                        pltpu.VMEM((nkc, KEY_CHUNK, Q_BLOCK), jnp.float32),
                        pltpu.VMEM((slabs, KEY_CHUNK, slab_w), jnp.float32),
                        pltpu.VMEM((slabs, KEY_CHUNK, slab_w), jnp.float32),
                        pltpu.VMEM((slabs, 1, slab_w), jnp.float32),
                        pltpu.VMEM((slabs, 1, slab_w), jnp.float32),
                        pltpu.VMEM((slabs, 1, slab_w), jnp.float32),
                        pltpu.VMEM((slabs, V_ROWS, slab_w), jnp.float32),
                        pltpu.VMEM((Q_BLOCK, A_WIDTH), jnp.bfloat16)],
        compiler_params=pltpu.CompilerParams(vmem_limit_bytes=VMEM_LIMIT),
        name="attn_a",
    )(qcat, qidx, aux, keys.reshape(bsz, nkc, KEY_CHUNK, A_QPAD), vt, kdup.reshape(bsz, nkc, KEY_CHUNK, LANES),
      gate, x2, mod, w_uv, w_out)


_PERM_DILS = tuple(dil for _, dil in B_GROUPS if dil > 1)


def _dilation_perms(tm):
    dst = jnp.arange(tm)
    perms = []
    for dil in _PERM_DILS:
        src = (dst % (tm // dil)) * dil + dst // (tm // dil)
        perms.append((src[:, None] == dst[None, :]).astype(jnp.bfloat16))
    return jnp.stack(perms)


_MATMUL_REORDER_DIL = 16


def _store_dilated(val, out_refs, tmp_ref, perm_ref):
    vb = _bf16(val)
    tm, width = vb.shape
    heads, _, hdim = tmp_ref.shape
    in_tmp = False
    for out_ref in out_refs:
        dil = out_ref.shape[1] // width
        rows = tm // dil
        if dil == 1:
            out_ref[...] = vb
        elif dil >= _MATMUL_REORDER_DIL:
            res = _bf16(_dot(perm_ref[_PERM_DILS.index(dil)], vb))
            for r in range(dil):
                out_ref[:, r * width:(r + 1) * width] = res[r * rows:(r + 1) * rows, :]
        else:
            if not in_tmp:
                for hd in range(heads):
                    tmp_ref[hd] = val[:, hd * hdim:(hd + 1) * hdim]
                in_tmp = True
            for r in range(dil):
                for hd in range(heads):
                    col = r * width + hd * hdim
                    out_ref[:, col:col + hdim] = _bf16(tmp_ref[hd, pl.ds(r, rows, stride=dil), :])


def _proj_b_kernel(*refs, with_kv):
    n_g = len(B_GROUPS)
    if with_kv:
        h_ref, mod_ref, g_ref, kvg_ref, wkv_ref, win_ref, cos_ref, sin_ref, perm_ref = refs[:9]
        outs = refs[9:]
        k_refs, v_refs, outs = outs[:n_g], outs[n_g:2 * n_g], outs[2 * n_g:]
    else:
        h_ref, mod_ref, g_ref, win_ref, cos_ref, sin_ref, perm_ref = refs[:7]
        outs = refs[7:]
    q_refs, gate_ref, tmp_ref = outs[:n_g], outs[n_g], outs[n_g + 1]
    h = h_ref[...]
    cos, sin = cos_ref[...], sin_ref[...]
    width = B_WIDTH

    def rope(t, scale):
        return jnp.concatenate([_rope_lanes(t[:, hd * B_HDIM:(hd + 1) * B_HDIM], cos, sin, B_HDIM) * scale
                                for hd in range(B_HEADS)], axis=1)

    if with_kv:
        kvn = _bf16(h * lax.rsqrt(jnp.mean(h * h, axis=-1, keepdims=True) + EPS) * kvg_ref[...])
        _store_dilated(rope(_dot(kvn, wkv_ref[:, 0:width]), 1.0), k_refs, tmp_ref, perm_ref)
        _store_dilated(_dot(kvn, wkv_ref[:, width:2 * width]), v_refs, tmp_ref, perm_ref)
    hn = _bf16(_prenorm(h, g_ref[...], mod_ref))
    qscale = B_HDIM ** -0.5 * LOG2E
    for gi, q_ref in enumerate(q_refs):
        _store_dilated(rope(_dot(hn, win_ref[:, gi * width:(gi + 1) * width]), qscale), [q_ref], tmp_ref, perm_ref)
    gate_ref[...] = _bf16(_silu(_dot(hn, win_ref[:, n_g * width:(n_g + 1) * width])))


def _proj_b(h2, mod, g, kvg, w_kv, w_in, cos, sin, bsz, seq, with_kv, tm=512):
    n, d = h2.shape
    tps = seq // tm
    row = lambda i: (i, 0)
    const = lambda i: (0, 0)
    in_specs = [pl.BlockSpec((tm, d), row), pl.BlockSpec((None, 3, d), lambda i: (i // tps, 0, 0)),
                pl.BlockSpec((1, d), const)]
    args = [h2, mod, g]
    if with_kv:
        in_specs += [pl.BlockSpec((1, d), const), pl.BlockSpec(w_kv.shape, const, pipeline_mode=pl.Buffered(1))]
        args += [kvg, w_kv]
    perms = _dilation_perms(tm)
    table = pl.BlockSpec((None, tm, LANES), lambda i: (ROPE_DIMS.index(B_HDIM), i, 0))
    in_specs += [pl.BlockSpec(w_in.shape, const, pipeline_mode=pl.Buffered(1)), table, table,
                 pl.BlockSpec(perms.shape, lambda i: (0, 0, 0), pipeline_mode=pl.Buffered(1))]
    args += [w_in, cos, sin, perms]
    dil_specs = [pl.BlockSpec((None, tm // dil, dil * B_WIDTH), lambda i: (i // tps, i % tps, 0)) for _, dil in B_GROUPS]
    dil_shapes = [jax.ShapeDtypeStruct((bsz, seq // dil, dil * B_WIDTH), jnp.bfloat16) for _, dil in B_GROUPS]
    n_sets = 3 if with_kv else 1
    return pl.pallas_call(
        functools.partial(_proj_b_kernel, with_kv=with_kv),
        grid=(n // tm,),
        in_specs=in_specs,
        out_specs=dil_specs * n_sets + [pl.BlockSpec((tm, B_WIDTH), row)],
        out_shape=dil_shapes * n_sets + [jax.ShapeDtypeStruct((n, B_WIDTH), jnp.bfloat16)],
        scratch_shapes=[pltpu.VMEM((B_HEADS, tm, B_HDIM), jnp.float32)],
        compiler_params=pltpu.CompilerParams(vmem_limit_bytes=VMEM_LIMIT),
        name="proj_b",
    )(*args)


def _dilated_kernel(q_ref, kc_ref, vc_ref, o_ref, lse_ref, kp_ref, vp_ref):
    j = pl.program_id(2)
    w = q_ref.shape[0]

    @pl.when(j == 0)
    def _():
        kp_ref[...] = jnp.zeros(kp_ref.shape, kp_ref.dtype)
        vp_ref[...] = jnp.zeros(vp_ref.shape, vp_ref.dtype)

    qi = lax.broadcasted_iota(jnp.int32, (w, w), 0)
    ki = lax.broadcasted_iota(jnp.int32, (w, w), 1)
    bias = jnp.concatenate([jnp.where(jnp.logical_and(ki >= qi, j > 0), 0.0, NEG),
                            jnp.where(ki <= qi, 0.0, NEG)], axis=1)
    ones = jnp.ones((2 * w, B_HDIM), jnp.bfloat16)
    lses = []
    for hd in range(B_HEADS):
        sl = slice(hd * B_HDIM, (hd + 1) * B_HDIM)
        keys = jnp.concatenate([kp_ref[:, sl], kc_ref[:, sl]], axis=0)
        vals = jnp.concatenate([vp_ref[:, sl], vc_ref[:, sl]], axis=0)
        s = _dot_nt(q_ref[:, sl], keys) + bias
        m = jnp.max(s, axis=-1, keepdims=True)
        p = _bf16(jnp.exp2(s - m))
        ol = _dot(p, jnp.concatenate([vals, ones], axis=1))
        l = ol[:, B_HDIM:]
        o_ref[:, sl] = _bf16(ol[:, :B_HDIM] / l)
        lses.append(m + jnp.log2(l[:, 0:1]))
    lse_ref[...] = jnp.concatenate(lses + [jnp.zeros((w, LANES - B_HEADS), jnp.float32)], axis=1)
    kp_ref[...] = kc_ref[...]
    vp_ref[...] = vc_ref[...]


def _dilated(q, k, v, window, dil):
    bsz, n, _ = q.shape
    w = window // dil
    nb = n // w
    cur = lambda b, r, j: (b, j, r)
    blk = pl.BlockSpec((None, w, B_WIDTH), cur)
    return pl.pallas_call(
        _dilated_kernel,
        grid=(bsz, dil, nb),
        in_specs=[blk, blk, blk],
        out_specs=[blk, pl.BlockSpec((None, None, w, LANES), lambda b, r, j: (b, r, j, 0))],
        out_shape=[jax.ShapeDtypeStruct(q.shape, jnp.bfloat16),
                   jax.ShapeDtypeStruct((bsz, dil, n, LANES), jnp.float32)],
        scratch_shapes=[pltpu.VMEM((w, B_WIDTH), jnp.bfloat16), pltpu.VMEM((w, B_WIDTH), jnp.bfloat16)],
        compiler_params=pltpu.CompilerParams(dimension_semantics=("arbitrary", "arbitrary", "arbitrary")),
        name=f"dilated_{dil}",
    )(q, k, v)


def _merge_kernel(*refs, final):
    n_g = len(B_GROUPS)
    o_refs, lse_refs = refs[:n_g], refs[n_g:2 * n_g]
    gate_ref, h_ref, mod_ref, wout_ref, fin_ref, unperm_ref, out_ref, ls_ref, og_ref = refs[2 * n_g:]
    tm = h_ref.shape[0]
    o_seq = []
    for gi, (_, dil) in enumerate(B_GROUPS):
        rows = tm // dil
        for r in range(dil):
            dst = pl.ds(r, rows, stride=dil) if dil > 1 else slice(None)
            ls_ref[gi, dst, :] = lse_refs[gi][r]
        if dil == 1:
            o_seq.append(o_refs[gi][...].astype(jnp.float32))
        else:
            stacked = jnp.concatenate([o_refs[gi][:, r * B_WIDTH:(r + 1) * B_WIDTH] for r in range(dil)], axis=0)
            o_seq.append(_dot(unperm_ref[_PERM_DILS.index(dil)], stacked))
    lses = [ls_ref[gi] for gi in range(n_g)]
    m = functools.reduce(jnp.maximum, lses)
    e = [jnp.exp2(t - m) for t in lses]
    inv = 1.0 / functools.reduce(lambda a, b: a + b, e)
    for hd in range(B_HEADS):
        sl = slice(hd * B_HDIM, (hd + 1) * B_HDIM)
        o = functools.reduce(lambda a, b: a + b,
                             [(e[gi] * inv)[:, hd:hd + 1] * o_seq[gi][:, sl] for gi in range(n_g)])
        og_ref[:, sl] = _bf16(o * gate_ref[:, sl])
    h = h_ref[...] + mod_ref[2:3, :] * _dot(og_ref[...], wout_ref[...])
    if final:
        h = h * lax.rsqrt(jnp.mean(h * h, axis=-1, keepdims=True) + EPS) * fin_ref[...]
    out_ref[...] = h


def _merge(os_, lses, gate, h2, mod, w_out, fin, seq, final, tm=512):
    n, d = h2.shape
    tps = seq // tm
    row = lambda i: (i, 0)
    rows = pl.BlockSpec((tm, d), row)
    n_g = len(B_GROUPS)
    o_specs = [pl.BlockSpec((None, tm // dil, dil * B_WIDTH), lambda i: (i // tps, i % tps, 0)) for _, dil in B_GROUPS]
    lse_specs = [pl.BlockSpec((None, dil, tm // dil, LANES), lambda i: (i // tps, 0, i % tps, 0)) for _, dil in B_GROUPS]
    unperms = jnp.swapaxes(_dilation_perms(tm), 1, 2)
    return pl.pallas_call(
        functools.partial(_merge_kernel, final=final),
        grid=(n // tm,),
        in_specs=o_specs + lse_specs + [rows, rows, pl.BlockSpec((None, 3, d), lambda i: (i // tps, 0, 0)),
                                        pl.BlockSpec(w_out.shape, lambda i: (0, 0)),
                                        pl.BlockSpec((1, d), lambda i: (0, 0)),
                                        pl.BlockSpec(unperms.shape, lambda i: (0, 0, 0))],
        out_specs=rows,
        out_shape=jax.ShapeDtypeStruct((n, d), jnp.float32),
        scratch_shapes=[pltpu.VMEM((n_g, tm, LANES), jnp.float32),
                        pltpu.VMEM((tm, B_WIDTH), jnp.bfloat16)],
        compiler_params=pltpu.CompilerParams(vmem_limit_bytes=VMEM_LIMIT),
        name="merge_out",
    )(*os_, *lses, gate, h2, mod, w_out, fin, unperms)


def _final_norm_kernel(h_ref, g_ref, out_ref):
    h = h_ref[...]
    out_ref[...] = h * lax.rsqrt(jnp.mean(h * h, axis=-1, keepdims=True) + EPS) * g_ref[...]


def kernel(x, c, positions, a_norm, a_ada_w, a_ada_b, a_w_in, a_kv_norm, a_w_uv, a_w_out, kv_norm, w_kv, b_norm, b_ada_w, b_ada_b, b_w_in, b_w_out, final_norm):
    bsz, seq, d = x.shape
    n = bsz * seq
    n_a, n_b = a_norm.shape[0], b_norm.shape[0]
    topk = min(TOPK_MAX, seq // 4)
    assert topk % Q_BLOCK == 0 and seq % KEY_CHUNK == 0 and d == B_WIDTH
    assert all(seq % window == 0 for window, _ in B_GROUPS)

    cos, sin = _rope_tables(positions, ROPE_DIMS)
    h = x.reshape(n, d)
    for li in range(n_a):
        mod = _ada(c, a_ada_w[li], a_ada_b[li])
        qcat, keys, vt, kdup, qidx, aux, gate = _proj_a(
            h, mod, a_norm[li].reshape(1, d), a_kv_norm[li].reshape(1, A_LATENT), _a_weight(a_w_in[li]),
            cos, sin, bsz, seq)
        h = _attn_a(qcat, qidx, aux, keys, vt, kdup, gate, h, mod, _bf16(a_w_uv[li]), _bf16(a_w_out[li]), topk)
    k_sh = v_sh = None
    for li in range(n_b):
        mod = _ada(c, b_ada_w[li], b_ada_b[li])
        outs = _proj_b(h, mod, b_norm[li].reshape(1, d), kv_norm.reshape(1, d), _bf16(w_kv), _bf16(b_w_in[li]),
                       cos, sin, bsz, seq, with_kv=(li == 0))
        n_g = len(B_GROUPS)
        if li == 0:
            k_sh, v_sh = outs[:n_g], outs[n_g:2 * n_g]
            outs = outs[2 * n_g:]
        q_groups, gate = outs[:n_g], outs[n_g]
        res = [_dilated(qg, kg, vg, window, dil)
               for qg, kg, vg, (window, dil) in zip(q_groups, k_sh, v_sh, B_GROUPS)]
        h = _merge([r[0] for r in res], [r[1] for r in res], gate, h, mod, _bf16(b_w_out[li]),
                   final_norm.reshape(1, d), seq, final=(li == n_b - 1))
    if n_b == 0:
        h = pl.pallas_call(
            _final_norm_kernel, grid=(n // 256,),
            in_specs=[pl.BlockSpec((256, d), lambda i: (i, 0)), pl.BlockSpec((1, d), lambda i: (0, 0))],
            out_specs=pl.BlockSpec((256, d), lambda i: (i, 0)),
            out_shape=jax.ShapeDtypeStruct((n, d), jnp.float32), name="final_norm",
        )(h, final_norm.reshape(1, d))
    return h.reshape(bsz, seq, d)
```
